```python
import math
import jax, jax.numpy as jnp
from jax import lax
import numpy as np

D_MODEL = 1024
BATCH = 4
SEQ = 8192
DEPTH = 4

CTX_LEN = 256
GRID_W = 64
N_MIXERS = 3
N_LAYERS_CONV = (DEPTH + 2) // 3
N_LAYERS_S5 = (DEPTH + 1) // 3
N_LAYERS_LRU = DEPTH // 3
CONV_WIDTH = 31
S5_GROUP = 16
S5_GROUPS = D_MODEL // S5_GROUP
S5_STATE = 64
SCAN_CHUNK = 128
LRU_WIDTH = D_MODEL
LRU_HEADS = 8
LRU_BLOCK = LRU_WIDTH // LRU_HEADS
LRU_CONV = 4
LRU_C = 8.0
N_EXPERTS = 16
EC_CAPACITY = 2
D_EXPERT = D_MODEL
EPS = 1e-6
POS_BASE = 10000.0

kernel_name = "hybrid_conv_s5_rglru_ecmoe_diffusion_trunk"

F32 = jnp.float32


def rmsnorm(x, g):
    xf = x.astype(F32)
    y = xf * lax.rsqrt(jnp.mean(xf * xf, axis=-1, keepdims=True) + EPS)
    return (y * g).astype(x.dtype)


def layernorm(x, g, b):
    xf = x.astype(F32)
    mu = jnp.mean(xf, axis=-1, keepdims=True)
    xc = xf - mu
    y = xc * lax.rsqrt(jnp.mean(xc * xc, axis=-1, keepdims=True) + EPS)
    return (y * g + b).astype(x.dtype)


def adaln(cond, w, b):
    return jnp.split(jax.nn.silu(cond) @ w + b, 6, axis=-1)


def depthwise_conv(x, w, pad):
    return lax.conv_general_dilated(
        x, w[:, None, :].astype(x.dtype), window_strides=(1,), padding=[pad],
        dimension_numbers=("NWC", "WIO", "NWC"), feature_group_count=x.shape[-1])


def grid_pos_embed(n, d):
    rows = n // GRID_W
    quarter = d // 4
    omega = 1.0 / (POS_BASE ** (jnp.arange(quarter, dtype=F32) / quarter))
    r = jnp.arange(rows, dtype=F32)[:, None] * omega
    cc = jnp.arange(GRID_W, dtype=F32)[:, None] * omega
    row_emb = jnp.concatenate([jnp.sin(r), jnp.cos(r)], axis=-1)
    col_emb = jnp.concatenate([jnp.sin(cc), jnp.cos(cc)], axis=-1)
    emb = jnp.concatenate([
        jnp.broadcast_to(row_emb[:, None, :], (rows, GRID_W, d // 2)),
        jnp.broadcast_to(col_emb[None, :, :], (rows, GRID_W, d // 2))], axis=-1)
    return emb.reshape(rows * GRID_W, d)


def conformer_conv(h, w_in, b_in, dw, dw_b, ln_g, ln_b, w_out, b_out):
    u = h @ w_in + b_in
    a, g = jnp.split(u, 2, axis=-1)
    u = a * jax.nn.sigmoid(g)
    u = depthwise_conv(u, dw, (CONV_WIDTH // 2, CONV_WIDTH // 2)) + dw_b
    u = layernorm(u, ln_g, ln_b)
    return jax.nn.silu(u) @ w_out + b_out


def s5_discretize(lam_re, lam_im, log_dt, b_re, b_im):
    lam_re = lam_re.astype(F32); lam_im = lam_im.astype(F32)
    dt = jnp.exp(log_dt.astype(F32))[:, None]
    mag = jnp.exp(lam_re * dt)
    ang = lam_im * dt
    lb_re = mag * jnp.cos(ang)
    lb_im = mag * jnp.sin(ang)
    nr = lb_re - 1.0
    ni = lb_im
    den = lam_re * lam_re + lam_im * lam_im
    coef_re = ((nr * lam_re + ni * lam_im) / den)[:, :, None]
    coef_im = ((ni * lam_re - nr * lam_im) / den)[:, :, None]
    b_re = b_re.astype(F32); b_im = b_im.astype(F32)
    bb_re = coef_re * b_re - coef_im * b_im
    bb_im = coef_re * b_im + coef_im * b_re
    return lb_re, lb_im, bb_re, bb_im


def s5_combine(e1, e2):
    a1r, a1i, b1r, b1i = e1
    a2r, a2i, b2r, b2i = e2
    return (a1r * a2r - a1i * a2i,
            a1r * a2i + a1i * a2r,
            a2r * b1r - a2i * b1i + b2r,
            a2r * b1i + a2i * b1r + b2i)


def s5_scan(u, lb_re, lb_im, bb_re, bb_im, c_re, c_im, h0, want_y):
    bsz, n, d = u.shape
    nc = n // SCAN_CHUNK
    uc = u.reshape(bsz, nc, SCAN_CHUNK, S5_GROUPS, S5_GROUP).transpose(1, 0, 2, 3, 4)

    def step(h, uk):
        hr0, hi0 = h
        bur = jnp.einsum("bsgk,gpk->bsgp", uk, bb_re)
        bui = jnp.einsum("bsgk,gpk->bsgp", uk, bb_im)
        ar = jnp.broadcast_to(lb_re, bur.shape)
        ai = jnp.broadcast_to(lb_im, bur.shape)
        cr, ci, sr, si = lax.associative_scan(s5_combine, (ar, ai, bur, bui), axis=1)
        hr = sr + cr * hr0[:, None] - ci * hi0[:, None]
        hi = si + cr * hi0[:, None] + ci * hr0[:, None]
        carry = (hr[:, -1], hi[:, -1])
        if want_y:
            y = (jnp.einsum("bsgp,gkp->bsgk", hr, c_re)
                 - jnp.einsum("bsgp,gkp->bsgk", hi, c_im))
            return carry, y
        return carry, None

    h_last, ys = lax.scan(step, h0, uc)
    y = ys.transpose(1, 0, 2, 3, 4).reshape(bsz, n, d) if want_y else None
    return y, h_last


def s5_glu(y, w, b, dtype):
    z = jax.nn.gelu(y).astype(dtype) @ w + b
    a, g = jnp.split(z, 2, axis=-1)
    return a * jax.nn.sigmoid(g)


def s5_mixer(h_lat, h_ctx, lam_re, lam_im, log_dt, b_re, b_im, c_re, c_im,
             d_skip, w_glu, b_glu, ctx_out):
    dtype = h_lat.dtype
    ul = h_lat.astype(F32)
    uc = h_ctx.astype(F32)
    dsk = d_skip.astype(F32)
    y_lat = dsk * ul
    y_ctx = dsk * uc if ctx_out else None
    bsz = h_ctx.shape[0]
    h0 = (jnp.zeros((bsz, S5_GROUPS, S5_STATE), F32),
          jnp.zeros((bsz, S5_GROUPS, S5_STATE), F32))
    for direction in range(2):
        lb_re, lb_im, bb_re, bb_im = s5_discretize(
            lam_re[direction], lam_im[direction], log_dt[direction],
            b_re[direction], b_im[direction])
        cre = c_re[direction].astype(F32)
        cim = c_im[direction].astype(F32)
        rev = direction == 1
        ucd = uc[:, ::-1] if rev else uc
        uld = ul[:, ::-1] if rev else ul
        yc, hc = s5_scan(ucd, lb_re, lb_im, bb_re, bb_im, cre, cim, h0, ctx_out)
        yl, _ = s5_scan(uld, lb_re, lb_im, bb_re, bb_im, cre, cim, hc, True)
        y_lat = y_lat + (yl[:, ::-1] if rev else yl)
        if ctx_out:
            y_ctx = y_ctx + (yc[:, ::-1] if rev else yc)
    out_lat = s5_glu(y_lat, w_glu, b_glu, dtype)
    out_ctx = s5_glu(y_ctx, w_glu, b_glu, dtype) if ctx_out else None
    return out_lat, out_ctx


def lru_gates(xb, w_a, b_a, w_i, b_i, lam):
    bsz, n, w = xb.shape
    xh = xb.reshape(bsz, n, LRU_HEADS, LRU_BLOCK)
    r = jax.nn.sigmoid((jnp.einsum("bshi,hij->bshj", xh, w_a).reshape(bsz, n, w) + b_a).astype(F32))
    ig = jax.nn.sigmoid((jnp.einsum("bshi,hij->bshj", xh, w_i).reshape(bsz, n, w) + b_i).astype(F32))
    log_a = -LRU_C * r * jax.nn.softplus(-lam.astype(F32))
    a = jnp.exp(log_a)
    b = jnp.sqrt(-jnp.expm1(2.0 * log_a)) * (ig * xb.astype(F32))
    return a, b


def linear_scan(a, b, h0):
    ca, cb = lax.associative_scan(
        lambda e1, e2: (e1[0] * e2[0], e2[0] * e1[1] + e2[1]), (a, b), axis=1)
    return cb + ca * h0[:, None]


def lru_mixer(h_lat, h_ctx, w_y, b_y, w_x, b_x, conv_w, conv_b, w_a, b_a, w_i, b_i,
              lam, w_out, b_out, ctx_out):
    dtype = h_lat.dtype
    pad = (LRU_CONV // 2, LRU_CONV - 1 - LRU_CONV // 2)

    def branch(h):
        return depthwise_conv(h @ w_x + b_x, conv_w, pad) + conv_b

    xl = branch(h_lat)
    xc = branch(h_ctx)
    bsz = h_ctx.shape[0]
    r_lat = jnp.zeros(xl.shape, F32)
    r_ctx = jnp.zeros(xc.shape, F32) if ctx_out else None
    for direction in range(2):
        al, bl = lru_gates(xl, w_a[direction], b_a[direction], w_i[direction], b_i[direction], lam[direction])
        ac, bc = lru_gates(xc, w_a[direction], b_a[direction], w_i[direction], b_i[direction], lam[direction])
        rev = direction == 1
        if rev:
            al, bl, ac, bc = al[:, ::-1], bl[:, ::-1], ac[:, ::-1], bc[:, ::-1]
        hc = linear_scan(ac, bc, jnp.zeros((bsz, LRU_WIDTH), F32))
        hl = linear_scan(al, bl, hc[:, -1])
        r_lat = r_lat + (hl[:, ::-1] if rev else hl)
        if ctx_out:
            r_ctx = r_ctx + (hc[:, ::-1] if rev else hc)

    def finish(h, r):
        gate = jax.nn.gelu((h @ w_y + b_y).astype(F32))
        return (r * gate).astype(dtype) @ w_out + b_out

    out_lat = finish(h_lat, r_lat)
    out_ctx = finish(h_ctx, r_ctx) if ctx_out else None
    return out_lat, out_ctx


def ec_moe(x, w_router, w1, w3, w2):
    bsz, n, d = x.shape
    cap = EC_CAPACITY * n // N_EXPERTS
    aff = jax.nn.softmax((x @ w_router).astype(F32), axis=-1)
    g, idx = lax.top_k(jnp.swapaxes(aff, 1, 2), cap)
    xs = jax.vmap(lambda xb, ib: xb[ib])(x, idx)
    h = jax.nn.silu(jnp.einsum("becd,edf->becf", xs, w1)) * jnp.einsum("becd,edf->becf", xs, w3)
    y = jnp.einsum("becf,efd->becd", h, w2) * g[..., None].astype(x.dtype)
    return jax.vmap(lambda yb, ib: jnp.zeros((n, d), y.dtype).at[ib.reshape(-1)].add(yb.reshape(-1, d)))(y, idx)


def setup_inputs(seed: int = 0) -> dict:
    key = jax.random.key(seed)
    split = jax.random.split(key, 64)
    keys = iter([split[i] for i in range(64)])

    def nrm(shape, scale):
        return scale * jax.random.normal(next(keys), shape, F32)

    D = D_MODEL
    NA, NB, NC = N_LAYERS_CONV, N_LAYERS_S5, N_LAYERS_LRU
    G, P, K = S5_GROUPS, S5_STATE, S5_GROUP
    W, H, BL = LRU_WIDTH, LRU_HEADS, LRU_BLOCK
    E, DE = N_EXPERTS, D_EXPERT
    inv_sqrt2 = 1.0 / math.sqrt(2.0)

    lam_im0 = jnp.pi * jnp.arange(P, dtype=F32)
    u_a = jax.random.uniform(next(keys), (NC, 2, W), F32, minval=0.9, maxval=0.999)
    s_a = u_a ** (1.0 / LRU_C)

    return {
        "x": nrm((BATCH, SEQ, D), 1.0),
        "c": nrm((BATCH, D), 1.0),
        "ctx": nrm((BATCH, CTX_LEN, D), 1.0),
        "c_ctx": nrm((D,), 1.0),
        "w_mod": nrm((DEPTH, D, 6 * D), 0.5 * D ** -0.5),
        "b_mod": nrm((DEPTH, 6 * D), 0.02),
        "g_mix": 1.0 + nrm((DEPTH, D), 0.02),
        "g_ffn": 1.0 + nrm((DEPTH, D), 0.02),
        "conv_w_in": nrm((NA, D, 2 * D), D ** -0.5),
        "conv_b_in": nrm((NA, 2 * D), 0.02),
        "conv_dw": nrm((NA, CONV_WIDTH, D), CONV_WIDTH ** -0.5),
        "conv_dw_b": nrm((NA, D), 0.02),
        "conv_ln_g": 1.0 + nrm((NA, D), 0.02),
        "conv_ln_b": nrm((NA, D), 0.02),
        "conv_w_out": nrm((NA, D, D), D ** -0.5),
        "conv_b_out": nrm((NA, D), 0.02),
        "s5_lam_re": -0.5 + nrm((NB, 2, G, P), 0.01),
        "s5_lam_im": lam_im0 + nrm((NB, 2, G, P), 0.01),
        "s5_log_dt": jax.random.uniform(next(keys), (NB, 2, G), F32,
                                        minval=math.log(1e-3), maxval=math.log(1e-1)),
        "s5_b_re": nrm((NB, 2, G, P, K), inv_sqrt2 * K ** -0.5),
        "s5_b_im": nrm((NB, 2, G, P, K), inv_sqrt2 * K ** -0.5),
        "s5_c_re": nrm((NB, 2, G, K, P), inv_sqrt2 * P ** -0.5),
        "s5_c_im": nrm((NB, 2, G, K, P), inv_sqrt2 * P ** -0.5),
        "s5_d": nrm((NB, D), 1.0),
        "s5_w_glu": nrm((NB, D, 2 * D), D ** -0.5),
        "s5_b_glu": nrm((NB, 2 * D), 0.02),
        "lru_w_y": nrm((NC, D, W), D ** -0.5),
        "lru_b_y": nrm((NC, W), 0.02),
        "lru_w_x": nrm((NC, D, W), D ** -0.5),
        "lru_b_x": nrm((NC, W), 0.02),
        "lru_conv_w": nrm((NC, LRU_CONV, W), LRU_CONV ** -0.5),
        "lru_conv_b": nrm((NC, W), 0.02),
        "lru_w_a": nrm((NC, 2, H, BL, BL), BL ** -0.5),
        "lru_b_a": nrm((NC, 2, W), 0.02),
        "lru_w_i": nrm((NC, 2, H, BL, BL), BL ** -0.5),
        "lru_b_i": nrm((NC, 2, W), 0.02),
        "lru_lam": jnp.log(s_a) - jnp.log1p(-s_a),
        "lru_w_out": nrm((NC, W, D), W ** -0.5),
        "lru_b_out": nrm((NC, D), 0.02),
        "moe_router": nrm((DEPTH, D, E), D ** -0.5),
        "moe_w1": nrm((DEPTH, E, D, DE), D ** -0.5),
        "moe_w3": nrm((DEPTH, E, D, DE), D ** -0.5),
        "moe_w2": nrm((DEPTH, E, DE, D), DE ** -0.5),
        "final_g": 1.0 + nrm((D,), 0.02),
    }


def reference(x, c, ctx, c_ctx, w_mod, b_mod, g_mix, g_ffn,
              conv_w_in, conv_b_in, conv_dw, conv_dw_b, conv_ln_g, conv_ln_b, conv_w_out, conv_b_out,
              s5_lam_re, s5_lam_im, s5_log_dt, s5_b_re, s5_b_im, s5_c_re, s5_c_im, s5_d, s5_w_glu, s5_b_glu,
              lru_w_y, lru_b_y, lru_w_x, lru_b_x, lru_conv_w, lru_conv_b, lru_w_a, lru_b_a, lru_w_i, lru_b_i,
              lru_lam, lru_w_out, lru_b_out,
              moe_router, moe_w1, moe_w3, moe_w2, final_g):
    reader_layers = [i for i in range(DEPTH) if i % N_MIXERS != 0]
    last_reader = max(reader_layers) if reader_layers else -1

    x_lat = x + grid_pos_embed(x.shape[1], x.shape[2]).astype(x.dtype)
    x_ctx = ctx
    for i in range(DEPTH):
        kind = i % N_MIXERS
        j = i // N_MIXERS
        ctx_in = i <= last_reader
        ctx_out = i < last_reader

        sh1, sc1, g1, sh2, sc2, g2 = [t[:, None, :] for t in adaln(c, w_mod[i], b_mod[i])]
        h_lat = rmsnorm(x_lat, g_mix[i]) * (1.0 + sc1) + sh1
        h_ctx = None
        if ctx_in:
            csh1, csc1, cg1, csh2, csc2, cg2 = [t[None, None, :] for t in adaln(c_ctx, w_mod[i], b_mod[i])]
            h_ctx = rmsnorm(x_ctx, g_mix[i]) * (1.0 + csc1) + csh1

        if kind == 0:
            conv_args = (conv_w_in[j], conv_b_in[j], conv_dw[j], conv_dw_b[j],
                         conv_ln_g[j], conv_ln_b[j], conv_w_out[j], conv_b_out[j])
            y_lat = conformer_conv(h_lat, *conv_args)
            y_ctx = conformer_conv(h_ctx, *conv_args) if ctx_out else None
        elif kind == 1:
            y_lat, y_ctx = s5_mixer(h_lat, h_ctx, s5_lam_re[j], s5_lam_im[j], s5_log_dt[j],
                                    s5_b_re[j], s5_b_im[j], s5_c_re[j], s5_c_im[j],
                                    s5_d[j], s5_w_glu[j], s5_b_glu[j], ctx_out)
        else:
            y_lat, y_ctx = lru_mixer(h_lat, h_ctx, lru_w_y[j], lru_b_y[j], lru_w_x[j], lru_b_x[j],
                                     lru_conv_w[j], lru_conv_b[j], lru_w_a[j], lru_b_a[j],
                                     lru_w_i[j], lru_b_i[j], lru_lam[j], lru_w_out[j], lru_b_out[j],
                                     ctx_out)

        x_lat = x_lat + g1 * y_lat
        m_lat = rmsnorm(x_lat, g_ffn[i]) * (1.0 + sc2) + sh2
        x_lat = x_lat + g2 * ec_moe(m_lat, moe_router[i], moe_w1[i], moe_w3[i], moe_w2[i])
        if ctx_out:
            x_ctx = x_ctx + cg1 * y_ctx
            m_ctx = rmsnorm(x_ctx, g_ffn[i]) * (1.0 + csc2) + csh2
            x_ctx = x_ctx + cg2 * ec_moe(m_ctx, moe_router[i], moe_w1[i], moe_w3[i], moe_w2[i])
    return rmsnorm(x_lat, final_g)
```

```python
import functools
import math

import jax
import jax.numpy as jnp
from jax import lax
from jax.experimental import pallas as pl
from jax.experimental.pallas import tpu as pltpu

F32 = jnp.float32
BF16 = jnp.bfloat16
I32 = jnp.int32
HIGHEST = lax.Precision.HIGHEST

EPS = 1e-6
N_EXPERTS = 16
EC_CAPACITY = 2
CONV_WIDTH = 31
LRU_CONV = 4
LRU_HEADS = 8
LRU_C = 8.0
S5_GROUP = 16
S5_STATE = 64
S5_CHUNK = 16
GRID_W = 64
POS_BASE = 10000.0

LANES = 128
SUBLANES = 8
ROW_BLOCK = 512
VMEM_LIMIT = 56 * 1024 * 1024


def _cparams(*sem):
    return pltpu.CompilerParams(dimension_semantics=sem, vmem_limit_bytes=VMEM_LIMIT)


def _full(shape):
    nd = len(shape)
    return pl.BlockSpec(shape, lambda *_: (0,) * nd)


def _rms_mod(x, g, scale, shift):
    y = x * lax.rsqrt(jnp.mean(x * x, axis=-1, keepdims=True) + EPS)
    return (y * g) * (1.0 + scale) + shift


def _silu(x):
    return x * jax.nn.sigmoid(x)


def _row_block(n):
    return min(ROW_BLOCK, n)


def _mod_kernel(c_ref, w_ref, b_ref, o_ref):
    s = _silu(c_ref[...])
    o_ref[0] = jnp.dot(s, w_ref[0], precision=HIGHEST, preferred_element_type=F32) + b_ref[0]


def _modulation(cond, w_mod, b_mod):
    depth, d, d6 = w_mod.shape
    rows = cond.shape[0]
    tn = 1536
    return pl.pallas_call(
        _mod_kernel,
        out_shape=jax.ShapeDtypeStruct((depth, rows, d6), F32),
        grid=(depth, d6 // tn),
        in_specs=[pl.BlockSpec((rows, d), lambda i, j: (0, 0)),
                  pl.BlockSpec((1, d, tn), lambda i, j: (i, 0, j)),
                  pl.BlockSpec((1, 1, tn), lambda i, j: (i, 0, j))],
        out_specs=pl.BlockSpec((1, rows, tn), lambda i, j: (i, 0, j)),
        compiler_params=_cparams("parallel", "parallel"),
        name="modulation",
    )(cond, w_mod, b_mod.reshape(depth, 1, d6))


def _post(xn, mod_ref, gffn_ref, wr_ref, xo_ref, m_ref, aff_ref):
    xo_ref[0] = xn
    m = _rms_mod(xn, gffn_ref[...], mod_ref[0, 4:5, :], mod_ref[0, 3:4, :])
    m_ref[0] = m
    logits = lax.dot_general(wr_ref[...], m, (((1,), (1,)), ((), ())),
                             precision=HIGHEST, preferred_element_type=F32)
    ex = jnp.exp(logits - jnp.max(logits, axis=0, keepdims=True))
    aff_ref[0] = ex / jnp.sum(ex, axis=0, keepdims=True)


def _post_specs(bsz, n, d, tm):
    out_shape = (jax.ShapeDtypeStruct((bsz, n, d), F32),
                 jax.ShapeDtypeStruct((bsz, n, d), F32),
                 jax.ShapeDtypeStruct((bsz, N_EXPERTS, n), F32))
    out_specs = (pl.BlockSpec((1, tm, d), lambda b, i: (b, i, 0)),
                 pl.BlockSpec((1, tm, d), lambda b, i: (b, i, 0)),
                 pl.BlockSpec((1, N_EXPERTS, tm), lambda b, i: (b, 0, i)))
    return out_shape, out_specs


def _conv_in_kernel(*refs, has_pe):
    if has_pe:
        x_ref, pe_ref, mod_ref, g_ref, w_ref, b_ref, u_ref = refs
        x = x_ref[0] + pe_ref[...]
    else:
        x_ref, mod_ref, g_ref, w_ref, b_ref, u_ref = refs
        x = x_ref[0]
    h = _rms_mod(x, g_ref[...], mod_ref[0, 1:2, :], mod_ref[0, 0:1, :]).astype(BF16)
    z = jnp.dot(h, w_ref[...], preferred_element_type=F32) + b_ref[...]
    d = z.shape[1] // 2
    u_ref[0] = z[:, :d] * jax.nn.sigmoid(z[:, d:])


def _conv_in(x, pe, mod, g_mix, w_in, b_in):
    bsz, n, d = x.shape
    tm = _row_block(n)
    row = pl.BlockSpec((1, tm, d), lambda b, i: (b, i, 0))
    in_specs = [row]
    args = [x]
    if pe is not None:
        in_specs.append(pl.BlockSpec((tm, d), lambda b, i: (i, 0)))
        args.append(pe)
    in_specs += [pl.BlockSpec((1, 6, d), lambda b, i: (b, 0, 0)), _full((1, d)),
                 _full((d, 2 * d)), _full((1, 2 * d))]
    args += [mod, g_mix.reshape(1, d), w_in.astype(BF16), b_in.reshape(1, 2 * d)]
    return pl.pallas_call(
        functools.partial(_conv_in_kernel, has_pe=pe is not None),
        out_shape=jax.ShapeDtypeStruct((bsz, n, d), F32),
        grid=(bsz, n // tm), in_specs=in_specs, out_specs=row,
        compiler_params=_cparams("parallel", "parallel"), name="conv_in",
    )(*args)


CONV_HALO = 16
CONV_ROWS = 16


def _conv_out_kernel(*refs, has_pe, tm):
    if has_pe:
        (u_ref, up_ref, un_ref, x_ref, pe_ref, mod_ref, dw_ref, dwb_ref, lng_ref, lnb_ref,
         w_ref, b_ref, gffn_ref, wr_ref, xo_ref, m_ref, aff_ref, ext_ref, cv_ref) = refs
        x = x_ref[0] + pe_ref[...]
    else:
        (u_ref, up_ref, un_ref, x_ref, mod_ref, dw_ref, dwb_ref, lng_ref, lnb_ref,
         w_ref, b_ref, gffn_ref, wr_ref, xo_ref, m_ref, aff_ref, ext_ref, cv_ref) = refs
        x = x_ref[0]
    i = pl.program_id(1)
    last = pl.num_programs(1) - 1
    ext_ref[0:CONV_HALO, :] = jnp.where(i > 0, up_ref[0], 0.0)
    ext_ref[CONV_HALO:CONV_HALO + tm, :] = u_ref[0]
    ext_ref[CONV_HALO + tm:2 * CONV_HALO + tm, :] = jnp.where(i < last, un_ref[0], 0.0)

    first_tap = CONV_HALO - CONV_WIDTH // 2

    for c in range(tm // CONV_ROWS):
        base = c * CONV_ROWS
        acc = jnp.zeros((CONV_ROWS, ext_ref.shape[1]), F32)
        for k in range(CONV_WIDTH):
            lo = base + first_tap + k
            acc = acc + dw_ref[k:k + 1, :] * ext_ref[lo:lo + CONV_ROWS, :]
        cv_ref[base:base + CONV_ROWS, :] = acc
    cv = cv_ref[...] + dwb_ref[...]
    mu = jnp.mean(cv, axis=-1, keepdims=True)
    xc = cv - mu
    ln = xc * lax.rsqrt(jnp.mean(xc * xc, axis=-1, keepdims=True) + EPS) * lng_ref[...] + lnb_ref[...]
    y = jnp.dot(_silu(ln).astype(BF16), w_ref[...], preferred_element_type=F32) + b_ref[...]
    _post(x + mod_ref[0, 2:3, :] * y, mod_ref, gffn_ref, wr_ref, xo_ref, m_ref, aff_ref)


def _conv_out(u, x, pe, mod, dw, dw_b, ln_g, ln_b, w_out, b_out, g_ffn, w_router):
    bsz, n, d = x.shape
    tm = _row_block(n)
    hb = tm // CONV_HALO
    nh = n // CONV_HALO
    row = pl.BlockSpec((1, tm, d), lambda b, i: (b, i, 0))
    in_specs = [row,
                pl.BlockSpec((1, CONV_HALO, d), lambda b, i: (b, jnp.maximum(i * hb - 1, 0), 0)),
                pl.BlockSpec((1, CONV_HALO, d), lambda b, i: (b, jnp.minimum((i + 1) * hb, nh - 1), 0)),
                row]
    args = [u, u, u, x]
    if pe is not None:
        in_specs.append(pl.BlockSpec((tm, d), lambda b, i: (i, 0)))
        args.append(pe)
    vec = _full((1, d))
    in_specs += [pl.BlockSpec((1, 6, d), lambda b, i: (b, 0, 0)), _full((CONV_WIDTH, d)), vec, vec, vec,
                 _full((d, d)), vec, vec, _full((N_EXPERTS, d))]
    args += [mod, dw, dw_b.reshape(1, d), ln_g.reshape(1, d), ln_b.reshape(1, d),
             w_out.astype(BF16), b_out.reshape(1, d), g_ffn.reshape(1, d), w_router.T]
    out_shape, out_specs = _post_specs(bsz, n, d, tm)
    return pl.pallas_call(
        functools.partial(_conv_out_kernel, has_pe=pe is not None, tm=tm),
        out_shape=out_shape, grid=(bsz, n // tm), in_specs=in_specs, out_specs=out_specs,
        scratch_shapes=[pltpu.VMEM((tm + 2 * CONV_HALO, d), F32), pltpu.VMEM((tm, d), F32)],
        compiler_params=_cparams("parallel", "parallel"), name="conv_out",
    )(*args)


def _norm_kernel(x_ref, mod_ref, g_ref, u_ref):
    u_ref[0] = _rms_mod(x_ref[0], g_ref[...], mod_ref[0, 1:2, :], mod_ref[0, 0:1, :])


def _norm_mod(x, mod, g_mix):
    bsz, n, d = x.shape
    tm = _row_block(n)
    row = pl.BlockSpec((1, tm, d), lambda b, i: (b, i, 0))
    return pl.pallas_call(
        _norm_kernel, out_shape=jax.ShapeDtypeStruct((bsz, n, d), F32),
        grid=(bsz, n // tm),
        in_specs=[row, pl.BlockSpec((1, 6, d), lambda b, i: (b, 0, 0)), _full((1, d))],
        out_specs=row, compiler_params=_cparams("parallel", "parallel"), name="norm_mod",
    )(x, mod, g_mix.reshape(1, d))


def _s5_tables(lam_re, lam_im, log_dt, b_re, b_im, c_re, c_im):
    L = S5_CHUNK
    dt = jnp.exp(log_dt)[:, :, None]
    mag = jnp.exp(lam_re * dt)
    ang = lam_im * dt
    lb_re = mag * jnp.cos(ang)
    lb_im = mag * jnp.sin(ang)
    nr = lb_re - 1.0
    ni = lb_im
    den = lam_re * lam_re + lam_im * lam_im
    coef_re = ((nr * lam_re + ni * lam_im) / den)[..., None]
    coef_im = ((ni * lam_re - nr * lam_im) / den)[..., None]
    bb_re = coef_re * b_re - coef_im * b_im
    bb_im = coef_re * b_im + coef_im * b_re

    def powers(steps):
        st = steps.astype(F32)
        m = jnp.exp((lam_re * dt)[..., None] * st)
        a = (lam_im * dt)[..., None] * st
        return m * jnp.cos(a), m * jnp.sin(a)

    pw_re, pw_im = powers(jnp.arange(L + 1))
    cp_re = c_re[:, :, None] * pw_re.transpose(0, 1, 3, 2)[:, :, :, None, :] \
        - c_im[:, :, None] * pw_im.transpose(0, 1, 3, 2)[:, :, :, None, :]
    cp_im = c_re[:, :, None] * pw_im.transpose(0, 1, 3, 2)[:, :, :, None, :] \
        + c_im[:, :, None] * pw_re.transpose(0, 1, 3, 2)[:, :, :, None, :]
    kern = (jnp.einsum("dgtkp,dgpj->dgtkj", cp_re[:, :, :L], bb_re, precision=HIGHEST)
            - jnp.einsum("dgtkp,dgpj->dgtkj", cp_im[:, :, :L], bb_im, precision=HIGHEST))
    lbb_re = pw_re.transpose(0, 1, 3, 2)[..., None] * bb_re[:, :, None] \
        - pw_im.transpose(0, 1, 3, 2)[..., None] * bb_im[:, :, None]
    lbb_im = pw_re.transpose(0, 1, 3, 2)[..., None] * bb_im[:, :, None] \
        + pw_im.transpose(0, 1, 3, 2)[..., None] * bb_re[:, :, None]

    s_idx = jnp.arange(L)[:, None]
    t_idx = jnp.arange(L)[None, :]
    G = lam_re.shape[1]
    K = S5_GROUP
    P = S5_STATE
    tes, fos = [], []
    for direction in range(2):
        if direction == 0:
            lag = t_idx - s_idx
            e_pow = (L - 1) - jnp.arange(L)
            f_pow = jnp.arange(L) + 1
        else:
            lag = s_idx - t_idx
            e_pow = jnp.arange(L)
            f_pow = L - jnp.arange(L)
        valid = lag >= 0
        kd = kern[direction][:, jnp.clip(lag, 0, L - 1)]
        kd = jnp.where(valid[None, :, :, None, None], kd, 0.0)
        tmat = kd.transpose(0, 1, 4, 2, 3).reshape(G, L * K, L * K)
        e_re = lbb_re[direction][:, e_pow]
        e_im = lbb_im[direction][:, e_pow]
        emat = jnp.concatenate([e_re.transpose(0, 1, 3, 2).reshape(G, L * K, P),
                                e_im.transpose(0, 1, 3, 2).reshape(G, L * K, P)], axis=-1)
        tes.append(jnp.concatenate([tmat, emat], axis=-1))
        f_re = cp_re[direction][:, f_pow]
        f_im = cp_im[direction][:, f_pow]
        fos.append(jnp.concatenate([f_re.transpose(0, 3, 1, 2).reshape(G, P, L * K),
                                    -f_im.transpose(0, 3, 1, 2).reshape(G, P, L * K)], axis=1))
    te = jnp.stack(tes).astype(BF16)
    fo = jnp.stack(fos).astype(BF16)
    a_re, a_im = powers(L * (jnp.arange(SUBLANES) + 1))
    a_re = a_re.transpose(0, 3, 1, 2)
    a_im = a_im.transpose(0, 3, 1, 2)
    cat = jnp.concatenate([a_re, a_re], axis=-1).reshape(2, SUBLANES, G * 2 * P)
    swp = jnp.concatenate([-a_im, a_im], axis=-1).reshape(2, SUBLANES, G * 2 * P)
    cat = jnp.stack([cat[0], cat[1, ::-1]])
    swp = jnp.stack([swp[0], swp[1, ::-1]])
    tab = jnp.concatenate([cat, swp], axis=1)
    return te, fo, tab


S5_GROUPS_PER_STEP = 8


def _cmul(cat, swp, v):
    return cat * v + swp * pltpu.roll(v, S5_STATE, 1)


def _s5_core_kernel(u_ref, tef_ref, teb_ref, fof_ref, fob_ref, tab_ref, yl_ref, yc_ref,
                    yf_ref, yb_ref, hf_ref, hb_ref, *, nctx, nlat):
    gb = S5_GROUPS_PER_STEP
    rows = nctx + nlat
    w = 2 * S5_STATE
    lw = S5_CHUNK * S5_GROUP
    for g in range(gb):
        u = u_ref[0, g]
        zf = jnp.dot(u[0:rows], tef_ref[g], preferred_element_type=F32)
        yf_ref[g] = zf[:, :lw]
        hf_ref[:, g * w:(g + 1) * w] = zf[:, lw:]
        zb = jnp.dot(u[nctx:nctx + rows], teb_ref[g], preferred_element_type=F32)
        yb_ref[g] = zb[:, :lw]
        hb_ref[:, g * w:(g + 1) * w] = zb[:, lw:]

    nblk = rows // SUBLANES
    rid = lax.broadcasted_iota(I32, (SUBLANES, w), 0)

    def scan_block(i, carry):
        cf, cb = carry
        new_f, new_b = [], []
        rf = pl.multiple_of(i * SUBLANES, SUBLANES)
        rb = pl.multiple_of((nblk - 1 - i) * SUBLANES, SUBLANES)
        for g in range(gb):
            sl = slice(g * w, (g + 1) * w)
            x = hf_ref[pl.ds(rf, SUBLANES), sl]
            for s in (1, 2, 4):
                sh = jnp.where(rid >= s, pltpu.roll(x, s, 0), 0.0)
                x = x + _cmul(tab_ref[0, s - 1:s, sl], tab_ref[0, SUBLANES + s - 1:SUBLANES + s, sl], sh)
            cbc = jnp.broadcast_to(cf[g], (SUBLANES, w))
            x = x + _cmul(tab_ref[0, 0:SUBLANES, sl], tab_ref[0, SUBLANES:2 * SUBLANES, sl], cbc)
            hf_ref[pl.ds(rf, SUBLANES), sl] = jnp.where(rid == 0, cbc, pltpu.roll(x, 1, 0))
            new_f.append(x[SUBLANES - 1:SUBLANES, :])
            x = hb_ref[pl.ds(rb, SUBLANES), sl]
            for s in (1, 2, 4):
                sh = jnp.where(rid < SUBLANES - s, pltpu.roll(x, SUBLANES - s, 0), 0.0)
                x = x + _cmul(tab_ref[1, SUBLANES - s:SUBLANES - s + 1, sl],
                              tab_ref[1, 2 * SUBLANES - s:2 * SUBLANES - s + 1, sl], sh)
            cbc = jnp.broadcast_to(cb[g], (SUBLANES, w))
            x = x + _cmul(tab_ref[1, 0:SUBLANES, sl], tab_ref[1, SUBLANES:2 * SUBLANES, sl], cbc)
            hb_ref[pl.ds(rb, SUBLANES), sl] = jnp.where(rid == SUBLANES - 1, cbc,
                                                        pltpu.roll(x, SUBLANES - 1, 0))
            new_b.append(x[0:1, :])
        return tuple(new_f), tuple(new_b)

    zero = tuple(jnp.zeros((1, w), F32) for _ in range(gb))
    lax.fori_loop(0, nblk, scan_block, (zero, zero))

    for g in range(gb):
        sl = slice(g * w, (g + 1) * w)
        yf = yf_ref[g] + jnp.dot(hf_ref[:, sl].astype(BF16), fof_ref[g], preferred_element_type=F32)
        yb = yb_ref[g] + jnp.dot(hb_ref[:, sl].astype(BF16), fob_ref[g], preferred_element_type=F32)
        yl_ref[0, g] = yf[nctx:nctx + nlat] + yb[0:nlat]
        yc_ref[0, g] = yf[0:nctx] + yb[nlat:nlat + nctx]


def _s5_core(u_all, te, fo, tab, nctx, nlat):
    bsz, G, rtot, lw = u_all.shape
    gb = S5_GROUPS_PER_STEP
    rows = nctx + nlat
    w = 2 * S5_STATE
    gspec3 = lambda shape: pl.BlockSpec((gb,) + shape, lambda b, j: (j, 0, 0))
    return pl.pallas_call(
        functools.partial(_s5_core_kernel, nctx=nctx, nlat=nlat),
        out_shape=(jax.ShapeDtypeStruct((bsz, G, nlat, lw), F32),
                   jax.ShapeDtypeStruct((bsz, G, nctx, lw), F32)),
        grid=(bsz, G // gb),
        in_specs=[pl.BlockSpec((1, gb, rtot, lw), lambda b, j: (b, j, 0, 0)),
                  gspec3((lw, lw + w)), gspec3((lw, lw + w)), gspec3((w, lw)), gspec3((w, lw)),
                  pl.BlockSpec((2, 2 * SUBLANES, gb * w), lambda b, j: (0, 0, j))],
        out_specs=(pl.BlockSpec((1, gb, nlat, lw), lambda b, j: (b, j, 0, 0)),
                   pl.BlockSpec((1, gb, nctx, lw), lambda b, j: (b, j, 0, 0))),
        scratch_shapes=[pltpu.VMEM((gb, rows, lw), F32), pltpu.VMEM((gb, rows, lw), F32),
                        pltpu.VMEM((rows, gb * w), F32), pltpu.VMEM((rows, gb * w), F32)],
        compiler_params=_cparams("parallel", "parallel"), name="s5_core",
    )(u_all, te[0], te[1], fo[0], fo[1], tab)


def _s5_out_kernel(y_ref, u_ref, x_ref, mod_ref, dsk_ref, w_ref, b_ref, gffn_ref, wr_ref,
                   xo_ref, m_ref, aff_ref):
    y = y_ref[0] + dsk_ref[...] * u_ref[0]
    z = jnp.dot(jax.nn.gelu(y).astype(BF16), w_ref[...], preferred_element_type=F32) + b_ref[...]
    d = z.shape[1] // 2
    out = z[:, :d] * jax.nn.sigmoid(z[:, d:])
    _post(x_ref[0] + mod_ref[0, 2:3, :] * out, mod_ref, gffn_ref, wr_ref, xo_ref, m_ref, aff_ref)


def _s5_out(y, u, x, mod, d_skip, w_glu, b_glu, g_ffn, w_router):
    bsz, n, d = x.shape
    tm = _row_block(n)
    row = pl.BlockSpec((1, tm, d), lambda b, i: (b, i, 0))
    vec = _full((1, d))
    out_shape, out_specs = _post_specs(bsz, n, d, tm)
    return pl.pallas_call(
        _s5_out_kernel, out_shape=out_shape, grid=(bsz, n // tm),
        in_specs=[row, row, row, pl.BlockSpec((1, 6, d), lambda b, i: (b, 0, 0)), vec,
                  _full((d, 2 * d)), _full((1, 2 * d)), vec, _full((N_EXPERTS, d))],
        out_specs=out_specs, compiler_params=_cparams("parallel", "parallel"), name="s5_out",
    )(y, u, x, mod, d_skip.reshape(1, d), w_glu.astype(BF16), b_glu.reshape(1, 2 * d),
      g_ffn.reshape(1, d), w_router.T)


def _to_chunks(u):
    bsz, n, d = u.shape
    g = d // S5_GROUP
    return (u.astype(BF16).reshape(bsz, n // S5_CHUNK, S5_CHUNK, g, S5_GROUP)
            .transpose(0, 3, 1, 2, 4).reshape(bsz, g, n // S5_CHUNK, S5_CHUNK * S5_GROUP))


def _from_chunks(y):
    bsz, g, nc, _ = y.shape
    return (y.reshape(bsz, g, nc, S5_CHUNK, S5_GROUP).transpose(0, 2, 3, 1, 4)
            .reshape(bsz, nc * S5_CHUNK, g * S5_GROUP))


def _lru_in_kernel(x_ref, mod_ref, g_ref, wx_ref, bx_ref, wy_ref, by_ref, xp_ref, gate_ref):
    h = _rms_mod(x_ref[0], g_ref[...], mod_ref[0, 1:2, :], mod_ref[0, 0:1, :]).astype(BF16)
    xp_ref[0] = jnp.dot(h, wx_ref[...], preferred_element_type=F32) + bx_ref[...]
    gate_ref[0] = jax.nn.gelu(jnp.dot(h, wy_ref[...], preferred_element_type=F32) + by_ref[...])


def _lru_in(x, mod, g_mix, w_x, b_x, w_y, b_y):
    bsz, n, d = x.shape
    tm = _row_block(n)
    row = pl.BlockSpec((1, tm, d), lambda b, i: (b, i, 0))
    vec = _full((1, d))
    return pl.pallas_call(
        _lru_in_kernel,
        out_shape=(jax.ShapeDtypeStruct((bsz, n, d), F32), jax.ShapeDtypeStruct((bsz, n, d), F32)),
        grid=(bsz, n // tm),
        in_specs=[row, pl.BlockSpec((1, 6, d), lambda b, i: (b, 0, 0)), vec,
                  _full((d, d)), vec, _full((d, d)), vec],
        out_specs=(row, row), compiler_params=_cparams("parallel", "parallel"), name="lru_in",
    )(x, mod, g_mix.reshape(1, d), w_x.astype(BF16), b_x.reshape(1, d), w_y.astype(BF16), b_y.reshape(1, d))


LRU_HALO = 8


def _lru_scan_kernel(xf_ref, xfp_ref, xfn_ref, xb_ref, xbp_ref, xbn_ref, h0_ref, cw_ref, cb_ref,
                     wai_ref, bai_ref, lam_ref, rf_ref, rb_ref, hfin_ref,
                     ext_ref, a_ref, b_ref, h_ref, *, tm):
    i = pl.program_id(1)
    last = pl.num_programs(1) - 1
    d = a_ref.shape[1]
    hw = d // LRU_HEADS

    @pl.when(i == 0)
    def _():
        h_ref[...] = h0_ref[0]

    lam = lam_ref[...]
    nl = -lam
    softplus = jnp.maximum(nl, 0.0) + jnp.log1p(jnp.exp(-jnp.abs(nl)))
    c8 = -LRU_C * softplus
    rid = lax.broadcasted_iota(I32, (SUBLANES, d), 0)

    def direction(dr, main_ref, prev_ref, next_ref, blk, out_ref):
        ext_ref[0:LRU_HALO, :] = jnp.where(blk > 0, prev_ref[0], 0.0)
        ext_ref[LRU_HALO:LRU_HALO + tm, :] = main_ref[0]
        ext_ref[LRU_HALO + tm:2 * LRU_HALO + tm, :] = jnp.where(blk < last, next_ref[0], 0.0)
        xl = cb_ref[...]
        for k in range(LRU_CONV):
            off = LRU_HALO - LRU_CONV // 2 + k
            xl = xl + cw_ref[k:k + 1, :] * ext_ref[off:off + tm, :]
        xlb = xl.astype(BF16)
        for hd in range(LRU_HEADS):
            sl = slice(hd * hw, (hd + 1) * hw)
            z = jnp.dot(xlb[:, sl], wai_ref[dr, hd], preferred_element_type=F32)
            r = jax.nn.sigmoid(z[:, :hw] + bai_ref[dr, 0:1, sl])
            ig = jax.nn.sigmoid(z[:, hw:] + bai_ref[dr, 1:2, sl])
            log_a = c8[dr:dr + 1, sl] * r
            a_ref[:, sl] = jnp.exp(log_a)
            th = jnp.tanh(log_a)
            b_ref[:, sl] = jnp.sqrt(-2.0 * th / (1.0 - th)) * (ig * xl[:, sl])

        nblk = tm // SUBLANES

        def scan_block(j, carry):
            jj = (nblk - 1 - j) if dr == 1 else j
            r0 = pl.multiple_of(jj * SUBLANES, SUBLANES)
            a = a_ref[pl.ds(r0, SUBLANES), :]
            b = b_ref[pl.ds(r0, SUBLANES), :]
            for s in (1, 2, 4):
                if dr == 0:
                    keep = rid >= s
                    shift = s
                else:
                    keep = rid < SUBLANES - s
                    shift = SUBLANES - s
                a_sh = pltpu.roll(a, shift, 0)
                b_sh = pltpu.roll(b, shift, 0)
                b = jnp.where(keep, a * b_sh + b, b)
                a = jnp.where(keep, a * a_sh, a)
            h = b + a * carry
            out_ref[0, pl.ds(r0, SUBLANES), :] = h
            return h[0:1, :] if dr == 1 else h[SUBLANES - 1:SUBLANES, :]

        h_ref[dr:dr + 1, :] = lax.fori_loop(0, nblk, scan_block, h_ref[dr:dr + 1, :])

    direction(0, xf_ref, xfp_ref, xfn_ref, i, rf_ref)
    direction(1, xb_ref, xbp_ref, xbn_ref, last - i, rb_ref)
    hfin_ref[0] = h_ref[...]


def _lru_scan(xp, h0, conv_w, conv_b, w_a, b_a, w_i, b_i, lam):
    bsz, n, d = xp.shape
    tm = _row_block(n)
    nb = n // tm
    hb = tm // LRU_HALO
    nh = n // LRU_HALO
    fwd = lambda b, i: (b, i, 0)
    bwd = lambda b, i: (b, nb - 1 - i, 0)
    halo = lambda f: pl.BlockSpec((1, LRU_HALO, d), f)
    wai = jnp.concatenate([w_a, w_i], axis=-1).astype(BF16)
    bai = jnp.stack([b_a, b_i], axis=1)
    return pl.pallas_call(
        functools.partial(_lru_scan_kernel, tm=tm),
        out_shape=(jax.ShapeDtypeStruct((bsz, n, d), F32), jax.ShapeDtypeStruct((bsz, n, d), F32),
                   jax.ShapeDtypeStruct((bsz, 2, d), F32)),
        grid=(bsz, nb),
        in_specs=[pl.BlockSpec((1, tm, d), fwd),
                  halo(lambda b, i: (b, jnp.maximum(i * hb - 1, 0), 0)),
                  halo(lambda b, i: (b, jnp.minimum((i + 1) * hb, nh - 1), 0)),
                  pl.BlockSpec((1, tm, d), bwd),
                  halo(lambda b, i: (b, jnp.maximum((nb - 1 - i) * hb - 1, 0), 0)),
                  halo(lambda b, i: (b, jnp.minimum((nb - i) * hb, nh - 1), 0)),
                  pl.BlockSpec((1, 2, d), lambda b, i: (b, 0, 0)),
                  _full((LRU_CONV, d)), _full((1, d)), _full(wai.shape), _full(bai.shape), _full((2, d))],
        out_specs=(pl.BlockSpec((1, tm, d), fwd), pl.BlockSpec((1, tm, d), bwd),
                   pl.BlockSpec((1, 2, d), lambda b, i: (b, 0, 0))),
        scratch_shapes=[pltpu.VMEM((tm + 2 * LRU_HALO, d), F32), pltpu.VMEM((tm, d), F32),
                        pltpu.VMEM((tm, d), F32), pltpu.VMEM((2, d), F32)],
        compiler_params=_cparams("parallel", "arbitrary"), name="lru_scan",
    )(xp, xp, xp, xp, xp, xp, h0, conv_w, conv_b.reshape(1, d), wai, bai, lam)


def _lru_out_kernel(rf_ref, rb_ref, gate_ref, x_ref, mod_ref, w_ref, b_ref, gffn_ref, wr_ref,
                    xo_ref, m_ref, aff_ref):
    r = (rf_ref[0] + rb_ref[0]) * gate_ref[0]
    y = jnp.dot(r.astype(BF16), w_ref[...], preferred_element_type=F32) + b_ref[...]
    _post(x_ref[0] + mod_ref[0, 2:3, :] * y, mod_ref, gffn_ref, wr_ref, xo_ref, m_ref, aff_ref)


def _lru_out(rf, rb, gate, x, mod, w_out, b_out, g_ffn, w_router):
    bsz, n, d = x.shape
    tm = _row_block(n)
    row = pl.BlockSpec((1, tm, d), lambda b, i: (b, i, 0))
    vec = _full((1, d))
    out_shape, out_specs = _post_specs(bsz, n, d, tm)
    return pl.pallas_call(
        _lru_out_kernel, out_shape=out_shape, grid=(bsz, n // tm),
        in_specs=[row, row, row, row, pl.BlockSpec((1, 6, d), lambda b, i: (b, 0, 0)),
                  _full((d, d)), vec, vec, _full((N_EXPERTS, d))],
        out_specs=out_specs, compiler_params=_cparams("parallel", "parallel"), name="lru_out",
    )(rf, rb, gate, x, mod, w_out.astype(BF16), b_out.reshape(1, d), g_ffn.reshape(1, d), w_router.T)


INF_BITS = 0x7F800000


def _tile_cumsum(mask, tri):
    return jnp.dot(mask.astype(BF16), tri, preferred_element_type=F32)


def _select_kernel(aff_ref, lpos_ref, off_ref, sel_ref, *, n, cap):
    nt = n // LANES
    bits = pltpu.bitcast(aff_ref[0], I32)
    capf = float(cap)

    def bisect(_, lohi):
        lo, hi = lohi
        mid = lo + ((hi - lo + 1) >> 1)
        cnt = jnp.sum(jnp.where(bits >= mid, 1.0, 0.0), axis=1, keepdims=True)
        ok = cnt >= capf
        return jnp.where(ok, mid, lo), jnp.where(ok, hi, mid - 1)

    e = bits.shape[0]
    thr, _ = lax.fori_loop(0, 31, bisect, (jnp.zeros((e, 1), I32), jnp.full((e, 1), INF_BITS, I32)))
    gt = jnp.where(bits > thr, 1.0, 0.0)
    eq = jnp.where(bits == thr, 1.0, 0.0)
    need = capf - jnp.sum(gt, axis=1, keepdims=True)

    r_i = lax.broadcasted_iota(I32, (LANES, LANES), 0)
    c_i = lax.broadcasted_iota(I32, (LANES, LANES), 1)
    tri = jnp.where(r_i <= c_i, 1.0, 0.0).astype(BF16)
    sup = jnp.where(r_i < c_i, 1.0, 0.0).astype(BF16)
    t_i = lax.broadcasted_iota(I32, (n, LANES), 0)
    k_i = lax.broadcasted_iota(I32, (n, LANES), 1)
    tile_of = jnp.where((t_i >> 7) == k_i, 1.0, 0.0).astype(BF16)

    def tile_offsets(mask):
        counts = jnp.dot(mask.astype(BF16), tile_of, preferred_element_type=F32)
        return jnp.dot(counts.astype(BF16), sup, preferred_element_type=F32)

    eq_off = tile_offsets(eq)
    for k in range(nt):
        sl = slice(k * LANES, (k + 1) * LANES)
        eqk = eq[:, sl]
        rank = eq_off[:, k:k + 1] + _tile_cumsum(eqk, tri) - eqk
        sel_ref[:, sl] = gt[:, sl] + eqk * jnp.where(rank < need, 1.0, 0.0)
    sel = sel_ref[...]
    off_ref[0] = tile_offsets(sel).astype(I32)
    for k in range(nt):
        sl = slice(k * LANES, (k + 1) * LANES)
        sk = sel[:, sl]
        lpos_ref[0, :, sl] = jnp.where(sk > 0.0, _tile_cumsum(sk, tri) - 1.0, -1.0).astype(I32)


def _select(aff_t, cap):
    bsz, e, n = aff_t.shape
    return pl.pallas_call(
        functools.partial(_select_kernel, n=n, cap=cap),
        out_shape=(jax.ShapeDtypeStruct((bsz, e, n), I32), jax.ShapeDtypeStruct((bsz, e, LANES), I32)),
        grid=(bsz,),
        in_specs=[pl.BlockSpec((1, e, n), lambda b: (b, 0, 0))],
        out_specs=(pl.BlockSpec((1, e, n), lambda b: (b, 0, 0)),
                   pl.BlockSpec((1, e, LANES), lambda b: (b, 0, 0))),
        scratch_shapes=[pltpu.VMEM((e, n), F32)],
        compiler_params=_cparams("parallel"), name="moe_select",
    )(aff_t)


def _slots_kernel(off_ref, lpos_ref, aff_ref, out_ref, scr_ref, *, nt, cap):
    b = pl.program_id(0)
    e = pl.program_id(1)
    base = (b * pl.num_programs(1) + e) * LANES
    j_i = lax.broadcasted_iota(I32, (LANES, LANES), 0)
    row = lax.broadcasted_iota(I32, (LANES, LANES), 0)
    lane = lax.broadcasted_iota(I32, (LANES, LANES), 1)

    def tile(k, carry):
        lp = lpos_ref[0, 0, k]
        av = aff_ref[0, 0, k]
        onehot = jnp.where(lp == j_i, 1.0, 0.0).astype(BF16)
        a1 = av.astype(BF16).astype(F32)
        r1 = av - a1
        a2 = r1.astype(BF16).astype(F32)
        a3 = r1 - a2
        q = jnp.where(row == 0, lane.astype(F32),
                      jnp.where(row == 1, a1, jnp.where(row == 2, a2, jnp.where(row == 3, a3, 0.0))))
        res = lax.dot_general(onehot, q.astype(BF16), (((1,), (1,)), ((), ())),
                              preferred_element_type=F32)
        res = res + jnp.where(lane == 0, (k * LANES).astype(F32), 0.0)
        scr_ref[pl.ds(off_ref[base + k], LANES), :] = res
        return carry

    lax.fori_loop(0, nt, tile, 0)
    out_ref[0, 0] = scr_ref[0:cap, :]


def _slots(lpos, tile_off, aff_t, cap):
    bsz, e, n = lpos.shape
    nt = n // LANES
    lp5 = lpos.reshape(bsz, e, nt, 1, LANES)
    af5 = aff_t.reshape(bsz, e, nt, 1, LANES)
    blk = pl.BlockSpec((1, 1, nt, 1, LANES), lambda b, j, off: (b, j, 0, 0, 0))
    return pl.pallas_call(
        functools.partial(_slots_kernel, nt=nt, cap=cap),
        out_shape=jax.ShapeDtypeStruct((bsz, e, cap, LANES), F32),
        grid_spec=pltpu.PrefetchScalarGridSpec(
            num_scalar_prefetch=1, grid=(bsz, e), in_specs=[blk, blk],
            out_specs=pl.BlockSpec((1, 1, cap, LANES), lambda b, j, off: (b, j, 0, 0)),
            scratch_shapes=[pltpu.VMEM((cap + LANES, LANES), F32)]),
        compiler_params=_cparams("parallel", "parallel"), name="moe_slots",
    )(tile_off.reshape(-1), lp5, af5)


GATHER_SPLIT = 2


def _gather_kernel(idx_ref, m_ref, xs_ref, scr_ref, *, cap):
    b = pl.program_id(0)
    e = pl.program_id(2)
    base = (b * pl.num_programs(2) + e) * cap

    def rows(j8, carry):
        j0 = pl.multiple_of(j8 * SUBLANES, SUBLANES)
        for r in range(SUBLANES):
            t = idx_ref[base + j0 + r]
            scr_ref[pl.ds(j0 + r, 1), :] = m_ref[0, pl.ds(t, 1), :]
        return carry

    lax.fori_loop(0, cap // SUBLANES, rows, 0)
    xs_ref[0, 0] = scr_ref[...].astype(BF16)


def _gather(idx_flat, m, e, cap):
    bsz, n, d = m.shape
    dh = d // GATHER_SPLIT
    return pl.pallas_call(
        functools.partial(_gather_kernel, cap=cap),
        out_shape=jax.ShapeDtypeStruct((bsz, e, cap, d), BF16),
        grid_spec=pltpu.PrefetchScalarGridSpec(
            num_scalar_prefetch=1, grid=(bsz, GATHER_SPLIT, e),
            in_specs=[pl.BlockSpec((1, n, dh), lambda b, h, j, idx: (b, 0, h))],
            out_specs=pl.BlockSpec((1, 1, cap, dh), lambda b, h, j, idx: (b, j, 0, h)),
            scratch_shapes=[pltpu.VMEM((cap, dh), F32)]),
        compiler_params=_cparams("parallel", "parallel", "arbitrary"), name="moe_gather",
    )(idx_flat, m)


FFN_CHUNK = 512


def _ffn_kernel(xs_ref, gs_ref, w1_ref, w3_ref, w2_ref, y_ref):
    xs = xs_ref[0, 0]
    de = w1_ref.shape[2]
    fc = min(FFN_CHUNK, de)
    acc = None
    for c in range(de // fc):
        sl = slice(c * fc, (c + 1) * fc)
        h1 = jnp.dot(xs, w1_ref[0, :, sl], preferred_element_type=F32)
        h3 = jnp.dot(xs, w3_ref[0, :, sl], preferred_element_type=F32)
        h = (_silu(h1) * h3).astype(BF16)
        part = jnp.dot(h, w2_ref[0, sl, :], preferred_element_type=F32)
        acc = part if acc is None else acc + part
    gs = gs_ref[0, 0]
    gate = gs[:, 1:2] + gs[:, 2:3] + gs[:, 3:4]
    y_ref[0, 0] = acc * gate


def _ffn(xs, slots, w1, w3, w2):
    bsz, e, cap, d = xs.shape
    de = w1.shape[2]
    return pl.pallas_call(
        _ffn_kernel, out_shape=jax.ShapeDtypeStruct((bsz, e, cap, d), F32),
        grid=(bsz, e),
        in_specs=[pl.BlockSpec((1, 1, cap, d), lambda b, j: (b, j, 0, 0)),
                  pl.BlockSpec((1, 1, cap, LANES), lambda b, j: (b, j, 0, 0)),
                  pl.BlockSpec((1, d, de), lambda b, j: (j, 0, 0)),
                  pl.BlockSpec((1, d, de), lambda b, j: (j, 0, 0)),
                  pl.BlockSpec((1, de, d), lambda b, j: (j, 0, 0))],
        out_specs=pl.BlockSpec((1, 1, cap, d), lambda b, j: (b, j, 0, 0)),
        compiler_params=_cparams("parallel", "parallel"), name="moe_ffn",
    )(xs, slots, w1, w3, w2)


COMBINE_UNROLL = 4


def _combine_kernel(idx_ref, y_ref, acc_ref, *, cap):
    b = pl.program_id(0)
    e = pl.program_id(2)
    base = (b * pl.num_programs(2) + e) * cap

    @pl.when(e == 0)
    def _():
        acc_ref[...] = jnp.zeros(acc_ref.shape, F32)

    def rows(ju, carry):
        j0 = ju * COMBINE_UNROLL
        ts = [idx_ref[base + j0 + r] for r in range(COMBINE_UNROLL)]
        vals = [acc_ref[0, pl.ds(ts[r], 1), :] + y_ref[0, 0, pl.ds(j0 + r, 1), :]
                for r in range(COMBINE_UNROLL)]
        for r in range(COMBINE_UNROLL):
            acc_ref[0, pl.ds(ts[r], 1), :] = vals[r]
        return carry

    lax.fori_loop(0, cap // COMBINE_UNROLL, rows, 0)


def _combine(idx_flat, y, n):
    bsz, e, cap, d = y.shape
    dh = d // GATHER_SPLIT
    return pl.pallas_call(
        functools.partial(_combine_kernel, cap=cap),
        out_shape=jax.ShapeDtypeStruct((bsz, n, d), F32),
        grid_spec=pltpu.PrefetchScalarGridSpec(
            num_scalar_prefetch=1, grid=(bsz, GATHER_SPLIT, e),
            in_specs=[pl.BlockSpec((1, 1, cap, dh), lambda b, h, j, idx: (b, j, 0, h))],
            out_specs=pl.BlockSpec((1, n, dh), lambda b, h, j, idx: (b, 0, h))),
        compiler_params=_cparams("parallel", "parallel", "arbitrary"), name="moe_combine",
    )(idx_flat, y)


def _resid_kernel(*refs, final):
    if final:
        x_ref, y_ref, mod_ref, g_ref, o_ref = refs
    else:
        x_ref, y_ref, mod_ref, o_ref = refs
    x = x_ref[0] + mod_ref[0, 5:6, :] * y_ref[0]
    if final:
        x = x * lax.rsqrt(jnp.mean(x * x, axis=-1, keepdims=True) + EPS) * g_ref[...]
    o_ref[0] = x


def _resid(x, y, mod, final_g=None):
    bsz, n, d = x.shape
    tm = _row_block(n)
    row = pl.BlockSpec((1, tm, d), lambda b, i: (b, i, 0))
    in_specs = [row, row, pl.BlockSpec((1, 6, d), lambda b, i: (b, 0, 0))]
    args = [x, y, mod]
    if final_g is not None:
        in_specs.append(_full((1, d)))
        args.append(final_g.reshape(1, d))
    return pl.pallas_call(
        functools.partial(_resid_kernel, final=final_g is not None),
        out_shape=jax.ShapeDtypeStruct((bsz, n, d), F32), grid=(bsz, n // tm),
        in_specs=in_specs, out_specs=row,
        compiler_params=_cparams("parallel", "parallel"), name="moe_resid",
    )(*args)


def _moe(x_new, m, aff_t, mod, w1, w3, w2, final_g=None):
    bsz, n, d = m.shape
    e = aff_t.shape[1]
    cap = EC_CAPACITY * n // e
    lpos, tile_off = _select(aff_t, cap)
    slots = _slots(lpos, tile_off, aff_t, cap)
    idx_flat = slots[..., 0].astype(I32).reshape(-1)
    xs = _gather(idx_flat, m, e, cap)
    y = _ffn(xs, slots, w1, w3, w2)
    moe = _combine(idx_flat, y, n)
    return _resid(x_new, moe, mod, final_g)


def _pos_embed(n, d):
    rows = n // GRID_W
    quarter = d // 4
    omega = 1.0 / (POS_BASE ** (jnp.arange(quarter, dtype=F32) / quarter))
    r = jnp.arange(rows, dtype=F32)[:, None] * omega
    cc = jnp.arange(GRID_W, dtype=F32)[:, None] * omega
    row_emb = jnp.concatenate([jnp.sin(r), jnp.cos(r)], axis=-1)
    col_emb = jnp.concatenate([jnp.sin(cc), jnp.cos(cc)], axis=-1)
    emb = jnp.concatenate([
        jnp.broadcast_to(row_emb[:, None, :], (rows, GRID_W, d // 2)),
        jnp.broadcast_to(col_emb[None, :, :], (rows, GRID_W, d // 2))], axis=-1)
    return emb.reshape(rows * GRID_W, d)


def kernel(x, c, ctx, c_ctx, w_mod, b_mod, g_mix, g_ffn, conv_w_in, conv_b_in, conv_dw, conv_dw_b, conv_ln_g, conv_ln_b, conv_w_out, conv_b_out, s5_lam_re, s5_lam_im, s5_log_dt, s5_b_re, s5_b_im, s5_c_re, s5_c_im, s5_d, s5_w_glu, s5_b_glu, lru_w_y, lru_b_y, lru_w_x, lru_b_x, lru_conv_w, lru_conv_b, lru_w_a, lru_b_a, lru_w_i, lru_b_i, lru_lam, lru_w_out, lru_b_out, moe_router, moe_w1, moe_w3, moe_w2, final_g):
    bsz, n, d = x.shape
    nctx = ctx.shape[1]
    depth = w_mod.shape[0]
    n_mixers = 3
    reader_layers = [i for i in range(depth) if i % n_mixers != 0]
    last_reader = max(reader_layers) if reader_layers else -1

    rows = -(-(bsz + 1) // SUBLANES) * SUBLANES
    cond = jnp.zeros((rows, d), F32).at[:bsz].set(c).at[bsz].set(c_ctx)
    mods = _modulation(cond, w_mod, b_mod).reshape(depth, rows, 6, d)
    pe = _pos_embed(n, d)
    w1b, w3b, w2b = moe_w1.astype(BF16), moe_w3.astype(BF16), moe_w2.astype(BF16)

    x_lat, x_ctx = x, ctx
    for i in range(depth):
        kind, j = i % n_mixers, i // n_mixers
        ctx_in = i <= last_reader
        ctx_out = i < last_reader
        mod_lat = mods[i, :bsz]
        mod_ctx = jnp.broadcast_to(mods[i, bsz][None], (bsz, 6, d))
        lat_pe = pe if i == 0 else None
        streams = [(x_lat, mod_lat, lat_pe)]
        if ctx_out:
            streams.append((x_ctx, mod_ctx, None))

        if kind == 0:
            post = []
            for xs_, mod_, pe_ in streams:
                u = _conv_in(xs_, pe_, mod_, g_mix[i], conv_w_in[j], conv_b_in[j])
                post.append(_conv_out(u, xs_, pe_, mod_, conv_dw[j], conv_dw_b[j], conv_ln_g[j],
                                      conv_ln_b[j], conv_w_out[j], conv_b_out[j], g_ffn[i], moe_router[i]))
        elif kind == 1:
            te, fo, tab = _s5_tables(s5_lam_re[j], s5_lam_im[j], s5_log_dt[j], s5_b_re[j], s5_b_im[j],
                                     s5_c_re[j], s5_c_im[j])
            u_lat = _norm_mod(x_lat, mod_lat, g_mix[i])
            u_ctx = _norm_mod(x_ctx, mod_ctx, g_mix[i])
            uc = _to_chunks(u_ctx)
            u_all = jnp.concatenate([uc, _to_chunks(u_lat), uc], axis=2)
            y_lat, y_ctx = _s5_core(u_all, te, fo, tab, nctx // S5_CHUNK, n // S5_CHUNK)
            post = [_s5_out(_from_chunks(y_lat), u_lat, x_lat, mod_lat, s5_d[j], s5_w_glu[j], s5_b_glu[j],
                            g_ffn[i], moe_router[i])]
            if ctx_out:
                post.append(_s5_out(_from_chunks(y_ctx), u_ctx, x_ctx, mod_ctx, s5_d[j], s5_w_glu[j],
                                    s5_b_glu[j], g_ffn[i], moe_router[i]))
        else:
            scan_args = (lru_conv_w[j], lru_conv_b[j], lru_w_a[j], lru_b_a[j], lru_w_i[j], lru_b_i[j], lru_lam[j])
            xp_ctx, gate_ctx = _lru_in(x_ctx, mod_ctx, g_mix[i], lru_w_x[j], lru_b_x[j], lru_w_y[j], lru_b_y[j])
            rf_c, rb_c, h_ctx = _lru_scan(xp_ctx, jnp.zeros((bsz, 2, d), F32), *scan_args)
            xp_lat, gate_lat = _lru_in(x_lat, mod_lat, g_mix[i], lru_w_x[j], lru_b_x[j], lru_w_y[j], lru_b_y[j])
            rf_l, rb_l, _ = _lru_scan(xp_lat, h_ctx, *scan_args)
            post = [_lru_out(rf_l, rb_l, gate_lat, x_lat, mod_lat, lru_w_out[j], lru_b_out[j],
                             g_ffn[i], moe_router[i])]
            if ctx_out:
                post.append(_lru_out(rf_c, rb_c, gate_ctx, x_ctx, mod_ctx, lru_w_out[j], lru_b_out[j],
                                     g_ffn[i], moe_router[i]))

        xn, m, aff = post[0]
        x_lat = _moe(xn, m, aff, mod_lat, w1b[i], w3b[i], w2b[i],
                     final_g if i == depth - 1 else None)
        if ctx_out:
            xn, m, aff = post[1]
            x_ctx = _moe(xn, m, aff, mod_ctx, w1b[i], w3b[i], w2b[i])
    return x_lat
```

```python
import functools
import math

import jax
import jax.numpy as jnp
from jax import lax
from jax.experimental import pallas as pl
from jax.experimental.pallas import tpu as pltpu

F32 = jnp.float32
BF16 = jnp.bfloat16
I32 = jnp.int32
HIGHEST = lax.Precision.HIGHEST

EPS = 1e-6
N_EXPERTS = 16
EC_CAPACITY = 2
CONV_WIDTH = 31
LRU_CONV = 4
LRU_HEADS = 8
LRU_C = 8.0
S5_GROUP = 16
S5_STATE = 64
S5_CHUNK = 16
GRID_W = 64
POS_BASE = 10000.0

LANES = 128
SUBLANES = 8
ROW_BLOCK = 512
VMEM_LIMIT = 56 * 1024 * 1024


def _cparams(*sem):
    return pltpu.CompilerParams(dimension_semantics=sem, vmem_limit_bytes=VMEM_LIMIT)


def _full(shape):
    nd = len(shape)
    return pl.BlockSpec(shape, lambda *_: (0,) * nd)


def _rms_mod(x, g, scale, shift):
    y = x * lax.rsqrt(jnp.mean(x * x, axis=-1, keepdims=True) + EPS)
    return (y * g) * (1.0 + scale) + shift


def _silu(x):
    return x * jax.nn.sigmoid(x)


def _row_block(n):
    return min(ROW_BLOCK, n)


def _mod_kernel(c_ref, w_ref, b_ref, o_ref):
    s = _silu(c_ref[...])
    o_ref[0] = jnp.dot(s, w_ref[0], precision=HIGHEST, preferred_element_type=F32) + b_ref[0]


def _modulation(cond, w_mod, b_mod):
    depth, d, d6 = w_mod.shape
    rows = cond.shape[0]
    tn = 1536
    return pl.pallas_call(
        _mod_kernel,
        out_shape=jax.ShapeDtypeStruct((depth, rows, d6), F32),
        grid=(depth, d6 // tn),
        in_specs=[pl.BlockSpec((rows, d), lambda i, j: (0, 0)),
                  pl.BlockSpec((1, d, tn), lambda i, j: (i, 0, j)),
                  pl.BlockSpec((1, 1, tn), lambda i, j: (i, 0, j))],
        out_specs=pl.BlockSpec((1, rows, tn), lambda i, j: (i, 0, j)),
        compiler_params=_cparams("parallel", "parallel"),
        name="modulation",
    )(cond, w_mod, b_mod.reshape(depth, 1, d6))


def _store_token_major(ref, lead, val):
    rows, d = val.shape
    nl = d // LANES
    for c in range(nl):
        ref[lead + (pl.ds(c, rows, stride=nl), slice(None))] = val[:, c * LANES:(c + 1) * LANES]


def _load_token_major(ref, lead, rows, d):
    nl = d // LANES
    return jnp.concatenate([ref[lead + (pl.ds(c, rows, stride=nl), slice(None))] for c in range(nl)], axis=1)


def _post(xn, mod_ref, gffn_ref, wr_ref, xo_ref, m_ref, aff_ref):
    xo_ref[0] = xn
    m = _rms_mod(xn, gffn_ref[...], mod_ref[0, 4:5, :], mod_ref[0, 3:4, :])
    _store_token_major(m_ref, (0,), m)
    logits = lax.dot_general(wr_ref[...], m, (((1,), (1,)), ((), ())),
                             precision=HIGHEST, preferred_element_type=F32)
    ex = jnp.exp(logits - jnp.max(logits, axis=0, keepdims=True))
    aff_ref[0] = ex / jnp.sum(ex, axis=0, keepdims=True)


def _post_specs(bsz, n, d, tm):
    nl = d // LANES
    out_shape = (jax.ShapeDtypeStruct((bsz, n, d), F32),
                 jax.ShapeDtypeStruct((bsz, n * nl, LANES), F32),
                 jax.ShapeDtypeStruct((bsz, N_EXPERTS, n), F32))
    out_specs = (pl.BlockSpec((1, tm, d), lambda b, i: (b, i, 0)),
                 pl.BlockSpec((1, tm * nl, LANES), lambda b, i: (b, i, 0)),
                 pl.BlockSpec((1, N_EXPERTS, tm), lambda b, i: (b, 0, i)))
    return out_shape, out_specs


def _conv_in_kernel(*refs, has_pe):
    if has_pe:
        x_ref, pe_ref, mod_ref, g_ref, w_ref, b_ref, u_ref = refs
        x = x_ref[0] + pe_ref[...]
    else:
        x_ref, mod_ref, g_ref, w_ref, b_ref, u_ref = refs
        x = x_ref[0]
    h = _rms_mod(x, g_ref[...], mod_ref[0, 1:2, :], mod_ref[0, 0:1, :]).astype(BF16)
    z = jnp.dot(h, w_ref[...], preferred_element_type=F32) + b_ref[...]
    d = z.shape[1] // 2
    u_ref[0] = z[:, :d] * jax.nn.sigmoid(z[:, d:])


def _conv_in(x, pe, mod, g_mix, w_in, b_in):
    bsz, n, d = x.shape
    tm = _row_block(n)
    row = pl.BlockSpec((1, tm, d), lambda b, i: (b, i, 0))
    in_specs = [row]
    args = [x]
    if pe is not None:
        in_specs.append(pl.BlockSpec((tm, d), lambda b, i: (i, 0)))
        args.append(pe)
    in_specs += [pl.BlockSpec((1, 6, d), lambda b, i: (b, 0, 0)), _full((1, d)),
                 _full((d, 2 * d)), _full((1, 2 * d))]
    args += [mod, g_mix.reshape(1, d), w_in.astype(BF16), b_in.reshape(1, 2 * d)]
    return pl.pallas_call(
        functools.partial(_conv_in_kernel, has_pe=pe is not None),
        out_shape=jax.ShapeDtypeStruct((bsz, n, d), F32),
        grid=(bsz, n // tm), in_specs=in_specs, out_specs=row,
        compiler_params=_cparams("parallel", "parallel"), name="conv_in",
    )(*args)


CONV_HALO = 16
CONV_ROWS = 128


def _conv_out_kernel(*refs, has_pe, tm):
    if has_pe:
        (u_ref, up_ref, un_ref, x_ref, pe_ref, mod_ref, dw_ref, dwb_ref, lng_ref, lnb_ref,
         w_ref, b_ref, gffn_ref, wr_ref, xo_ref, m_ref, aff_ref, ext_ref, cv_ref) = refs
        x = x_ref[0] + pe_ref[...]
    else:
        (u_ref, up_ref, un_ref, x_ref, mod_ref, dw_ref, dwb_ref, lng_ref, lnb_ref,
         w_ref, b_ref, gffn_ref, wr_ref, xo_ref, m_ref, aff_ref, ext_ref, cv_ref) = refs
        x = x_ref[0]
    i = pl.program_id(1)
    last = pl.num_programs(1) - 1
    ext_ref[0:CONV_HALO, :] = jnp.where(i > 0, up_ref[0], 0.0)
    ext_ref[CONV_HALO:CONV_HALO + tm, :] = u_ref[0]
    ext_ref[CONV_HALO + tm:2 * CONV_HALO + tm, :] = jnp.where(i < last, un_ref[0], 0.0)

    first_tap = CONV_HALO - CONV_WIDTH // 2

    rows = min(CONV_ROWS, tm)
    for lt in range(ext_ref.shape[1] // LANES):
        ls = slice(lt * LANES, (lt + 1) * LANES)
        for c in range(tm // rows):
            base = c * rows
            acc = jnp.zeros((rows, LANES), F32)
            for s in range(SUBLANES):
                part = None
                for o in range(first_tap, first_tap + CONV_WIDTH):
                    if o % SUBLANES != s:
                        continue
                    lo = base + o - s
                    term = dw_ref[o - first_tap:o - first_tap + 1, ls] * ext_ref[lo:lo + rows + SUBLANES, ls]
                    part = term if part is None else part + term
                acc = acc + part[s:s + rows]
            cv_ref[base:base + rows, ls] = acc
    cv = cv_ref[...] + dwb_ref[...]
    mu = jnp.mean(cv, axis=-1, keepdims=True)
    xc = cv - mu
    ln = xc * lax.rsqrt(jnp.mean(xc * xc, axis=-1, keepdims=True) + EPS) * lng_ref[...] + lnb_ref[...]
    y = jnp.dot(_silu(ln).astype(BF16), w_ref[...], preferred_element_type=F32) + b_ref[...]
    _post(x + mod_ref[0, 2:3, :] * y, mod_ref, gffn_ref, wr_ref, xo_ref, m_ref, aff_ref)


def _conv_out(u, x, pe, mod, dw, dw_b, ln_g, ln_b, w_out, b_out, g_ffn, w_router):
    bsz, n, d = x.shape
    tm = _row_block(n)
    hb = tm // CONV_HALO
    nh = n // CONV_HALO
    row = pl.BlockSpec((1, tm, d), lambda b, i: (b, i, 0))
    in_specs = [row,
                pl.BlockSpec((1, CONV_HALO, d), lambda b, i: (b, jnp.maximum(i * hb - 1, 0), 0)),
                pl.BlockSpec((1, CONV_HALO, d), lambda b, i: (b, jnp.minimum((i + 1) * hb, nh - 1), 0)),
                row]
    args = [u, u, u, x]
    if pe is not None:
        in_specs.append(pl.BlockSpec((tm, d), lambda b, i: (i, 0)))
        args.append(pe)
    vec = _full((1, d))
    in_specs += [pl.BlockSpec((1, 6, d), lambda b, i: (b, 0, 0)), _full((CONV_WIDTH, d)), vec, vec, vec,
                 _full((d, d)), vec, vec, _full((N_EXPERTS, d))]
    args += [mod, dw, dw_b.reshape(1, d), ln_g.reshape(1, d), ln_b.reshape(1, d),
             w_out.astype(BF16), b_out.reshape(1, d), g_ffn.reshape(1, d), w_router.T]
    out_shape, out_specs = _post_specs(bsz, n, d, tm)
    return pl.pallas_call(
        functools.partial(_conv_out_kernel, has_pe=pe is not None, tm=tm),
        out_shape=out_shape, grid=(bsz, n // tm), in_specs=in_specs, out_specs=out_specs,
        scratch_shapes=[pltpu.VMEM((tm + 2 * CONV_HALO, d), F32), pltpu.VMEM((tm, d), F32)],
        compiler_params=_cparams("parallel", "parallel"), name="conv_out",
    )(*args)


def _norm_kernel(x_ref, mod_ref, g_ref, u_ref):
    u_ref[0] = _rms_mod(x_ref[0], g_ref[...], mod_ref[0, 1:2, :], mod_ref[0, 0:1, :])


def _norm_mod(x, mod, g_mix):
    bsz, n, d = x.shape
    tm = _row_block(n)
    row = pl.BlockSpec((1, tm, d), lambda b, i: (b, i, 0))
    return pl.pallas_call(
        _norm_kernel, out_shape=jax.ShapeDtypeStruct((bsz, n, d), F32),
        grid=(bsz, n // tm),
        in_specs=[row, pl.BlockSpec((1, 6, d), lambda b, i: (b, 0, 0)), _full((1, d))],
        out_specs=row, compiler_params=_cparams("parallel", "parallel"), name="norm_mod",
    )(x, mod, g_mix.reshape(1, d))


def _s5_tables(lam_re, lam_im, log_dt, b_re, b_im, c_re, c_im):
    L = S5_CHUNK
    dt = jnp.exp(log_dt)[:, :, None]
    mag = jnp.exp(lam_re * dt)
    ang = lam_im * dt
    lb_re = mag * jnp.cos(ang)
    lb_im = mag * jnp.sin(ang)
    nr = lb_re - 1.0
    ni = lb_im
    den = lam_re * lam_re + lam_im * lam_im
    coef_re = ((nr * lam_re + ni * lam_im) / den)[..., None]
    coef_im = ((ni * lam_re - nr * lam_im) / den)[..., None]
    bb_re = coef_re * b_re - coef_im * b_im
    bb_im = coef_re * b_im + coef_im * b_re

    def powers(steps):
        st = steps.astype(F32)
        m = jnp.exp((lam_re * dt)[..., None] * st)
        a = (lam_im * dt)[..., None] * st
        return m * jnp.cos(a), m * jnp.sin(a)

    pw_re, pw_im = powers(jnp.arange(L + 1))
    cp_re = c_re[:, :, None] * pw_re.transpose(0, 1, 3, 2)[:, :, :, None, :] \
        - c_im[:, :, None] * pw_im.transpose(0, 1, 3, 2)[:, :, :, None, :]
    cp_im = c_re[:, :, None] * pw_im.transpose(0, 1, 3, 2)[:, :, :, None, :] \
        + c_im[:, :, None] * pw_re.transpose(0, 1, 3, 2)[:, :, :, None, :]
    kern = (jnp.einsum("dgtkp,dgpj->dgtkj", cp_re[:, :, :L], bb_re, precision=HIGHEST)
            - jnp.einsum("dgtkp,dgpj->dgtkj", cp_im[:, :, :L], bb_im, precision=HIGHEST))
    lbb_re = pw_re.transpose(0, 1, 3, 2)[..., None] * bb_re[:, :, None] \
        - pw_im.transpose(0, 1, 3, 2)[..., None] * bb_im[:, :, None]
    lbb_im = pw_re.transpose(0, 1, 3, 2)[..., None] * bb_im[:, :, None] \
        + pw_im.transpose(0, 1, 3, 2)[..., None] * bb_re[:, :, None]

    s_idx = jnp.arange(L)[:, None]
    t_idx = jnp.arange(L)[None, :]
    G = lam_re.shape[1]
    K = S5_GROUP
    P = S5_STATE
    tes, fos = [], []
    for direction in range(2):
        if direction == 0:
            lag = t_idx - s_idx
            e_pow = (L - 1) - jnp.arange(L)
            f_pow = jnp.arange(L) + 1
        else:
            lag = s_idx - t_idx
            e_pow = jnp.arange(L)
            f_pow = L - jnp.arange(L)
        valid = lag >= 0
        kd = kern[direction][:, jnp.clip(lag, 0, L - 1)]
        kd = jnp.where(valid[None, :, :, None, None], kd, 0.0)
        tmat = kd.transpose(0, 1, 4, 2, 3).reshape(G, L * K, L * K)
        e_re = lbb_re[direction][:, e_pow]
        e_im = lbb_im[direction][:, e_pow]
        emat = jnp.concatenate([e_re.transpose(0, 1, 3, 2).reshape(G, L * K, P),
                                e_im.transpose(0, 1, 3, 2).reshape(G, L * K, P)], axis=-1)
        tes.append(jnp.concatenate([tmat, emat], axis=-1))
        f_re = cp_re[direction][:, f_pow]
        f_im = cp_im[direction][:, f_pow]
        fos.append(jnp.concatenate([f_re.transpose(0, 3, 1, 2).reshape(G, P, L * K),
                                    -f_im.transpose(0, 3, 1, 2).reshape(G, P, L * K)], axis=1))
    te = jnp.stack(tes).astype(BF16)
    fo = jnp.stack(fos).astype(BF16)
    a_re, a_im = powers(L * (jnp.arange(SUBLANES) + 1))
    a_re = a_re.transpose(0, 3, 1, 2)
    a_im = a_im.transpose(0, 3, 1, 2)
    cat = jnp.concatenate([a_re, a_re], axis=-1).reshape(2, SUBLANES, G * 2 * P)
    swp = jnp.concatenate([-a_im, a_im], axis=-1).reshape(2, SUBLANES, G * 2 * P)
    cat = jnp.stack([cat[0], cat[1, ::-1]])
    swp = jnp.stack([swp[0], swp[1, ::-1]])
    tab = jnp.concatenate([cat, swp], axis=1)
    return te, fo, tab


S5_GROUPS_PER_STEP = 8


def _cmul(cat, swp, v):
    return cat * v + swp * pltpu.roll(v, S5_STATE, 1)


def _s5_core_kernel(u_ref, tef_ref, teb_ref, fof_ref, fob_ref, tab_ref, yl_ref, yc_ref,
                    yf_ref, yb_ref, hf_ref, hb_ref, *, nctx, nlat):
    gb = S5_GROUPS_PER_STEP
    rows = nctx + nlat
    w = 2 * S5_STATE
    lw = S5_CHUNK * S5_GROUP
    for g in range(gb):
        u = u_ref[0, g]
        zf = jnp.dot(u[0:rows], tef_ref[g], preferred_element_type=F32)
        yf_ref[g] = zf[:, :lw]
        hf_ref[:, g * w:(g + 1) * w] = zf[:, lw:]
        zb = jnp.dot(u[nctx:nctx + rows], teb_ref[g], preferred_element_type=F32)
        yb_ref[g] = zb[:, :lw]
        hb_ref[:, g * w:(g + 1) * w] = zb[:, lw:]

    nblk = rows // SUBLANES
    rid = lax.broadcasted_iota(I32, (SUBLANES, w), 0)

    def scan_block(i, carry):
        cf, cb = carry
        new_f, new_b = [], []
        rf = pl.multiple_of(i * SUBLANES, SUBLANES)
        rb = pl.multiple_of((nblk - 1 - i) * SUBLANES, SUBLANES)
        for g in range(gb):
            sl = slice(g * w, (g + 1) * w)
            x = hf_ref[pl.ds(rf, SUBLANES), sl]
            for s in (1, 2, 4):
                sh = jnp.where(rid >= s, pltpu.roll(x, s, 0), 0.0)
                x = x + _cmul(tab_ref[0, s - 1:s, sl], tab_ref[0, SUBLANES + s - 1:SUBLANES + s, sl], sh)
            cbc = jnp.broadcast_to(cf[g], (SUBLANES, w))
            x = x + _cmul(tab_ref[0, 0:SUBLANES, sl], tab_ref[0, SUBLANES:2 * SUBLANES, sl], cbc)
            hf_ref[pl.ds(rf, SUBLANES), sl] = jnp.where(rid == 0, cbc, pltpu.roll(x, 1, 0))
            new_f.append(x[SUBLANES - 1:SUBLANES, :])
            x = hb_ref[pl.ds(rb, SUBLANES), sl]
            for s in (1, 2, 4):
                sh = jnp.where(rid < SUBLANES - s, pltpu.roll(x, SUBLANES - s, 0), 0.0)
                x = x + _cmul(tab_ref[1, SUBLANES - s:SUBLANES - s + 1, sl],
                              tab_ref[1, 2 * SUBLANES - s:2 * SUBLANES - s + 1, sl], sh)
            cbc = jnp.broadcast_to(cb[g], (SUBLANES, w))
            x = x + _cmul(tab_ref[1, 0:SUBLANES, sl], tab_ref[1, SUBLANES:2 * SUBLANES, sl], cbc)
            hb_ref[pl.ds(rb, SUBLANES), sl] = jnp.where(rid == SUBLANES - 1, cbc,
                                                        pltpu.roll(x, SUBLANES - 1, 0))
            new_b.append(x[0:1, :])
        return tuple(new_f), tuple(new_b)

    zero = tuple(jnp.zeros((1, w), F32) for _ in range(gb))
    lax.fori_loop(0, nblk, scan_block, (zero, zero))

    for g in range(gb):
        sl = slice(g * w, (g + 1) * w)
        yf = yf_ref[g] + jnp.dot(hf_ref[:, sl].astype(BF16), fof_ref[g], preferred_element_type=F32)
        yb = yb_ref[g] + jnp.dot(hb_ref[:, sl].astype(BF16), fob_ref[g], preferred_element_type=F32)
        yl_ref[0, g] = yf[nctx:nctx + nlat] + yb[0:nlat]
        yc_ref[0, g] = yf[0:nctx] + yb[nlat:nlat + nctx]


def _s5_core(u_all, te, fo, tab, nctx, nlat):
    bsz, G, rtot, lw = u_all.shape
    gb = S5_GROUPS_PER_STEP
    rows = nctx + nlat
    w = 2 * S5_STATE
    gspec3 = lambda shape: pl.BlockSpec((gb,) + shape, lambda b, j: (j, 0, 0))
    return pl.pallas_call(
        functools.partial(_s5_core_kernel, nctx=nctx, nlat=nlat),
        out_shape=(jax.ShapeDtypeStruct((bsz, G, nlat, lw), F32),
                   jax.ShapeDtypeStruct((bsz, G, nctx, lw), F32)),
        grid=(bsz, G // gb),
        in_specs=[pl.BlockSpec((1, gb, rtot, lw), lambda b, j: (b, j, 0, 0)),
                  gspec3((lw, lw + w)), gspec3((lw, lw + w)), gspec3((w, lw)), gspec3((w, lw)),
                  pl.BlockSpec((2, 2 * SUBLANES, gb * w), lambda b, j: (0, 0, j))],
        out_specs=(pl.BlockSpec((1, gb, nlat, lw), lambda b, j: (b, j, 0, 0)),
                   pl.BlockSpec((1, gb, nctx, lw), lambda b, j: (b, j, 0, 0))),
        scratch_shapes=[pltpu.VMEM((gb, rows, lw), F32), pltpu.VMEM((gb, rows, lw), F32),
                        pltpu.VMEM((rows, gb * w), F32), pltpu.VMEM((rows, gb * w), F32)],
        compiler_params=_cparams("parallel", "parallel"), name="s5_core",
    )(u_all, te[0], te[1], fo[0], fo[1], tab)


def _s5_out_kernel(y_ref, u_ref, x_ref, mod_ref, dsk_ref, w_ref, b_ref, gffn_ref, wr_ref,
                   xo_ref, m_ref, aff_ref):
    y = y_ref[0] + dsk_ref[...] * u_ref[0]
    z = jnp.dot(jax.nn.gelu(y).astype(BF16), w_ref[...], preferred_element_type=F32) + b_ref[...]
    d = z.shape[1] // 2
    out = z[:, :d] * jax.nn.sigmoid(z[:, d:])
    _post(x_ref[0] + mod_ref[0, 2:3, :] * out, mod_ref, gffn_ref, wr_ref, xo_ref, m_ref, aff_ref)


def _s5_out(y, u, x, mod, d_skip, w_glu, b_glu, g_ffn, w_router):
    bsz, n, d = x.shape
    tm = _row_block(n)
    row = pl.BlockSpec((1, tm, d), lambda b, i: (b, i, 0))
    vec = _full((1, d))
    out_shape, out_specs = _post_specs(bsz, n, d, tm)
    return pl.pallas_call(
        _s5_out_kernel, out_shape=out_shape, grid=(bsz, n // tm),
        in_specs=[row, row, row, pl.BlockSpec((1, 6, d), lambda b, i: (b, 0, 0)), vec,
                  _full((d, 2 * d)), _full((1, 2 * d)), vec, _full((N_EXPERTS, d))],
        out_specs=out_specs, compiler_params=_cparams("parallel", "parallel"), name="s5_out",
    )(y, u, x, mod, d_skip.reshape(1, d), w_glu.astype(BF16), b_glu.reshape(1, 2 * d),
      g_ffn.reshape(1, d), w_router.T)


def _to_chunks(u):
    bsz, n, d = u.shape
    g = d // S5_GROUP
    return (u.astype(BF16).reshape(bsz, n // S5_CHUNK, S5_CHUNK, g, S5_GROUP)
            .transpose(0, 3, 1, 2, 4).reshape(bsz, g, n // S5_CHUNK, S5_CHUNK * S5_GROUP))


def _from_chunks(y):
    bsz, g, nc, _ = y.shape
    return (y.reshape(bsz, g, nc, S5_CHUNK, S5_GROUP).transpose(0, 2, 3, 1, 4)
            .reshape(bsz, nc * S5_CHUNK, g * S5_GROUP))


def _lru_in_kernel(x_ref, mod_ref, g_ref, wx_ref, bx_ref, wy_ref, by_ref, xp_ref, gate_ref):
    h = _rms_mod(x_ref[0], g_ref[...], mod_ref[0, 1:2, :], mod_ref[0, 0:1, :]).astype(BF16)
    xp_ref[0] = jnp.dot(h, wx_ref[...], preferred_element_type=F32) + bx_ref[...]
    gate_ref[0] = jax.nn.gelu(jnp.dot(h, wy_ref[...], preferred_element_type=F32) + by_ref[...])


def _lru_in(x, mod, g_mix, w_x, b_x, w_y, b_y):
    bsz, n, d = x.shape
    tm = _row_block(n)
    row = pl.BlockSpec((1, tm, d), lambda b, i: (b, i, 0))
    vec = _full((1, d))
    return pl.pallas_call(
        _lru_in_kernel,
        out_shape=(jax.ShapeDtypeStruct((bsz, n, d), F32), jax.ShapeDtypeStruct((bsz, n, d), F32)),
        grid=(bsz, n // tm),
        in_specs=[row, pl.BlockSpec((1, 6, d), lambda b, i: (b, 0, 0)), vec,
                  _full((d, d)), vec, _full((d, d)), vec],
        out_specs=(row, row), compiler_params=_cparams("parallel", "parallel"), name="lru_in",
    )(x, mod, g_mix.reshape(1, d), w_x.astype(BF16), b_x.reshape(1, d), w_y.astype(BF16), b_y.reshape(1, d))


LRU_HALO = 8


def _lru_scan_kernel(xf_ref, xfp_ref, xfn_ref, xb_ref, xbp_ref, xbn_ref, h0_ref, cw_ref, cb_ref,
                     wai_ref, bai_ref, lam_ref, rf_ref, rb_ref, hfin_ref,
                     ext_ref, a_ref, b_ref, h_ref, *, tm):
    i = pl.program_id(1)
    last = pl.num_programs(1) - 1
    d = a_ref.shape[1]
    hw = d // LRU_HEADS

    @pl.when(i == 0)
    def _():
        h_ref[...] = h0_ref[0]

    lam = lam_ref[...]
    nl = -lam
    softplus = jnp.maximum(nl, 0.0) + jnp.log1p(jnp.exp(-jnp.abs(nl)))
    c8 = -LRU_C * softplus
    rid = lax.broadcasted_iota(I32, (SUBLANES, d), 0)

    def direction(dr, main_ref, prev_ref, next_ref, blk, out_ref):
        ext_ref[0:LRU_HALO, :] = jnp.where(blk > 0, prev_ref[0], 0.0)
        ext_ref[LRU_HALO:LRU_HALO + tm, :] = main_ref[0]
        ext_ref[LRU_HALO + tm:2 * LRU_HALO + tm, :] = jnp.where(blk < last, next_ref[0], 0.0)
        xl = cb_ref[...]
        for k in range(LRU_CONV):
            off = LRU_HALO - LRU_CONV // 2 + k
            xl = xl + cw_ref[k:k + 1, :] * ext_ref[off:off + tm, :]
        xlb = xl.astype(BF16)
        for hd in range(LRU_HEADS):
            sl = slice(hd * hw, (hd + 1) * hw)
            z = jnp.dot(xlb[:, sl], wai_ref[dr, hd], preferred_element_type=F32)
            r = jax.nn.sigmoid(z[:, :hw] + bai_ref[dr, 0:1, sl])
            ig = jax.nn.sigmoid(z[:, hw:] + bai_ref[dr, 1:2, sl])
            log_a = c8[dr:dr + 1, sl] * r
            a_ref[:, sl] = jnp.exp(log_a)
            th = jnp.tanh(log_a)
            b_ref[:, sl] = jnp.sqrt(-2.0 * th / (1.0 - th)) * (ig * xl[:, sl])

        nblk = tm // SUBLANES

        def scan_block(j, carry):
            jj = (nblk - 1 - j) if dr == 1 else j
            r0 = pl.multiple_of(jj * SUBLANES, SUBLANES)
            a = a_ref[pl.ds(r0, SUBLANES), :]
            b = b_ref[pl.ds(r0, SUBLANES), :]
            for s in (1, 2, 4):
                if dr == 0:
                    keep = rid >= s
                    shift = s
                else:
                    keep = rid < SUBLANES - s
                    shift = SUBLANES - s
                a_sh = pltpu.roll(a, shift, 0)
                b_sh = pltpu.roll(b, shift, 0)
                b = jnp.where(keep, a * b_sh + b, b)
                a = jnp.where(keep, a * a_sh, a)
            h = b + a * carry
            out_ref[0, pl.ds(r0, SUBLANES), :] = h
            return h[0:1, :] if dr == 1 else h[SUBLANES - 1:SUBLANES, :]

        h_ref[dr:dr + 1, :] = lax.fori_loop(0, nblk, scan_block, h_ref[dr:dr + 1, :])

    direction(0, xf_ref, xfp_ref, xfn_ref, i, rf_ref)
    direction(1, xb_ref, xbp_ref, xbn_ref, last - i, rb_ref)
    hfin_ref[0] = h_ref[...]


def _lru_scan(xp, h0, conv_w, conv_b, w_a, b_a, w_i, b_i, lam):
    bsz, n, d = xp.shape
    tm = _row_block(n)
    nb = n // tm
    hb = tm // LRU_HALO
    nh = n // LRU_HALO
    fwd = lambda b, i: (b, i, 0)
    bwd = lambda b, i: (b, nb - 1 - i, 0)
    halo = lambda f: pl.BlockSpec((1, LRU_HALO, d), f)
    wai = jnp.concatenate([w_a, w_i], axis=-1).astype(BF16)
    bai = jnp.stack([b_a, b_i], axis=1)
    return pl.pallas_call(
        functools.partial(_lru_scan_kernel, tm=tm),
        out_shape=(jax.ShapeDtypeStruct((bsz, n, d), F32), jax.ShapeDtypeStruct((bsz, n, d), F32),
                   jax.ShapeDtypeStruct((bsz, 2, d), F32)),
        grid=(bsz, nb),
        in_specs=[pl.BlockSpec((1, tm, d), fwd),
                  halo(lambda b, i: (b, jnp.maximum(i * hb - 1, 0), 0)),
                  halo(lambda b, i: (b, jnp.minimum((i + 1) * hb, nh - 1), 0)),
                  pl.BlockSpec((1, tm, d), bwd),
                  halo(lambda b, i: (b, jnp.maximum((nb - 1 - i) * hb - 1, 0), 0)),
                  halo(lambda b, i: (b, jnp.minimum((nb - i) * hb, nh - 1), 0)),
                  pl.BlockSpec((1, 2, d), lambda b, i: (b, 0, 0)),
                  _full((LRU_CONV, d)), _full((1, d)), _full(wai.shape), _full(bai.shape), _full((2, d))],
        out_specs=(pl.BlockSpec((1, tm, d), fwd), pl.BlockSpec((1, tm, d), bwd),
                   pl.BlockSpec((1, 2, d), lambda b, i: (b, 0, 0))),
        scratch_shapes=[pltpu.VMEM((tm + 2 * LRU_HALO, d), F32), pltpu.VMEM((tm, d), F32),
                        pltpu.VMEM((tm, d), F32), pltpu.VMEM((2, d), F32)],
        compiler_params=_cparams("parallel", "arbitrary"), name="lru_scan",
    )(xp, xp, xp, xp, xp, xp, h0, conv_w, conv_b.reshape(1, d), wai, bai, lam)


def _lru_out_kernel(rf_ref, rb_ref, gate_ref, x_ref, mod_ref, w_ref, b_ref, gffn_ref, wr_ref,
                    xo_ref, m_ref, aff_ref):
    r = (rf_ref[0] + rb_ref[0]) * gate_ref[0]
    y = jnp.dot(r.astype(BF16), w_ref[...], preferred_element_type=F32) + b_ref[...]
    _post(x_ref[0] + mod_ref[0, 2:3, :] * y, mod_ref, gffn_ref, wr_ref, xo_ref, m_ref, aff_ref)


def _lru_out(rf, rb, gate, x, mod, w_out, b_out, g_ffn, w_router):
    bsz, n, d = x.shape
    tm = _row_block(n)
    row = pl.BlockSpec((1, tm, d), lambda b, i: (b, i, 0))
    vec = _full((1, d))
    out_shape, out_specs = _post_specs(bsz, n, d, tm)
    return pl.pallas_call(
        _lru_out_kernel, out_shape=out_shape, grid=(bsz, n // tm),
        in_specs=[row, row, row, row, pl.BlockSpec((1, 6, d), lambda b, i: (b, 0, 0)),
                  _full((d, d)), vec, vec, _full((N_EXPERTS, d))],
        out_specs=out_specs, compiler_params=_cparams("parallel", "parallel"), name="lru_out",
    )(rf, rb, gate, x, mod, w_out.astype(BF16), b_out.reshape(1, d), g_ffn.reshape(1, d), w_router.T)


INF_BITS = 0x7F800000
SLOT_UNROLL = 8


def _tile_cumsum(mask, tri):
    return jnp.dot(mask.astype(BF16), tri, preferred_element_type=F32)


def _select_kernel(aff_ref, lpos_ref, off_ref, sel_ref, *, n, cap):
    nt = n // LANES
    bits = pltpu.bitcast(aff_ref[0], I32)
    capf = float(cap)

    def bisect(_, lohi):
        lo, hi = lohi
        mid = lo + ((hi - lo + 1) >> 1)
        cnt = jnp.sum(jnp.where(bits >= mid, 1.0, 0.0), axis=1, keepdims=True)
        ok = cnt >= capf
        return jnp.where(ok, mid, lo), jnp.where(ok, hi, mid - 1)

    e = bits.shape[0]
    thr, _ = lax.fori_loop(0, 31, bisect, (jnp.zeros((e, 1), I32), jnp.full((e, 1), INF_BITS, I32)))
    gt = jnp.where(bits > thr, 1.0, 0.0)
    eq = jnp.where(bits == thr, 1.0, 0.0)
    need = capf - jnp.sum(gt, axis=1, keepdims=True)

    r_i = lax.broadcasted_iota(I32, (LANES, LANES), 0)
    c_i = lax.broadcasted_iota(I32, (LANES, LANES), 1)
    tri = jnp.where(r_i <= c_i, 1.0, 0.0).astype(BF16)
    sup = jnp.where(r_i < c_i, 1.0, 0.0).astype(BF16)
    t_i = lax.broadcasted_iota(I32, (n, LANES), 0)
    k_i = lax.broadcasted_iota(I32, (n, LANES), 1)
    tile_of = jnp.where((t_i >> 7) == k_i, 1.0, 0.0).astype(BF16)

    def tile_offsets(mask):
        counts = jnp.dot(mask.astype(BF16), tile_of, preferred_element_type=F32)
        return jnp.dot(counts.astype(BF16), sup, preferred_element_type=F32)

    eq_off = tile_offsets(eq)
    for k in range(nt):
        sl = slice(k * LANES, (k + 1) * LANES)
        eqk = eq[:, sl]
        rank = eq_off[:, k:k + 1] + _tile_cumsum(eqk, tri) - eqk
        sel_ref[:, sl] = gt[:, sl] + eqk * jnp.where(rank < need, 1.0, 0.0)
    sel = sel_ref[...]
    off_ref[0] = tile_offsets(sel).astype(I32)
    for k in range(nt):
        sl = slice(k * LANES, (k + 1) * LANES)
        sk = sel[:, sl]
        lpos_ref[0, :, sl] = jnp.where(sk > 0.0, _tile_cumsum(sk, tri) - 1.0, -1.0).astype(I32)


def _select(aff_t, cap):
    bsz, e, n = aff_t.shape
    return pl.pallas_call(
        functools.partial(_select_kernel, n=n, cap=cap),
        out_shape=(jax.ShapeDtypeStruct((bsz, e, n), I32), jax.ShapeDtypeStruct((bsz, e, LANES), I32)),
        grid=(bsz,),
        in_specs=[pl.BlockSpec((1, e, n), lambda b: (b, 0, 0))],
        out_specs=(pl.BlockSpec((1, e, n), lambda b: (b, 0, 0)),
                   pl.BlockSpec((1, e, LANES), lambda b: (b, 0, 0))),
        scratch_shapes=[pltpu.VMEM((e, n), F32)],
        compiler_params=_cparams("parallel"), name="moe_select",
    )(aff_t)


def _slots_kernel(off_ref, lpos_ref, aff_ref, out_ref, scr_ref, *, nt, cap):
    b = pl.program_id(0)
    e = pl.program_id(1)
    base = (b * pl.num_programs(1) + e) * LANES
    j_i = lax.broadcasted_iota(I32, (LANES, LANES), 0)
    row = lax.broadcasted_iota(I32, (LANES, LANES), 0)
    lane = lax.broadcasted_iota(I32, (LANES, LANES), 1)

    def tile(k, carry):
        lp = lpos_ref[0, 0, k]
        av = aff_ref[0, 0, k]
        onehot = jnp.where(lp == j_i, 1.0, 0.0).astype(BF16)
        a1 = av.astype(BF16).astype(F32)
        r1 = av - a1
        a2 = r1.astype(BF16).astype(F32)
        a3 = r1 - a2
        q = jnp.where(row == 0, lane.astype(F32),
                      jnp.where(row == 1, a1, jnp.where(row == 2, a2, jnp.where(row == 3, a3, 0.0))))
        res = lax.dot_general(onehot, q.astype(BF16), (((1,), (1,)), ((), ())),
                              preferred_element_type=F32)
        res = res + jnp.where(lane == 0, (k * LANES).astype(F32), 0.0)
        scr_ref[pl.ds(off_ref[base + k], LANES), :] = res
        return carry

    lax.fori_loop(0, nt, tile, 0, unroll=min(SLOT_UNROLL, nt))
    out_ref[0, 0] = scr_ref[0:cap, :]


def _slots(lpos, tile_off, aff_t, cap):
    bsz, e, n = lpos.shape
    nt = n // LANES
    lp5 = lpos.reshape(bsz, e, nt, 1, LANES)
    af5 = aff_t.reshape(bsz, e, nt, 1, LANES)
    blk = pl.BlockSpec((1, 1, nt, 1, LANES), lambda b, j, off: (b, j, 0, 0, 0))
    return pl.pallas_call(
        functools.partial(_slots_kernel, nt=nt, cap=cap),
        out_shape=jax.ShapeDtypeStruct((bsz, e, cap, LANES), F32),
        grid_spec=pltpu.PrefetchScalarGridSpec(
            num_scalar_prefetch=1, grid=(bsz, e), in_specs=[blk, blk],
            out_specs=pl.BlockSpec((1, 1, cap, LANES), lambda b, j, off: (b, j, 0, 0)),
            scratch_shapes=[pltpu.VMEM((cap + LANES, LANES), F32)]),
        compiler_params=_cparams("parallel", "parallel"), name="moe_slots",
    )(tile_off.reshape(-1), lp5, af5)


ROW_UNROLL = 8


def _gather_kernel(idx_ref, m_ref, xs_ref, scr_ref, *, cap, nl):
    b = pl.program_id(0)
    e = pl.program_id(1)
    base = (b * pl.num_programs(1) + e) * cap

    def rows(ju, carry):
        j0 = ju * ROW_UNROLL
        for r in range(ROW_UNROLL):
            t = idx_ref[base + j0 + r]
            scr_ref[pl.ds(pl.multiple_of((j0 + r) * nl, nl), nl), :] = \
                m_ref[0, pl.ds(pl.multiple_of(t * nl, nl), nl), :]
        return carry

    lax.fori_loop(0, cap // ROW_UNROLL, rows, 0)
    xs_ref[0, 0] = _load_token_major(scr_ref, (), cap, nl * LANES).astype(BF16)


def _gather(idx_flat, m_tok, e, cap):
    bsz, rows_, _ = m_tok.shape
    nl = SUBLANES
    d = nl * LANES
    return pl.pallas_call(
        functools.partial(_gather_kernel, cap=cap, nl=nl),
        out_shape=jax.ShapeDtypeStruct((bsz, e, cap, d), BF16),
        grid_spec=pltpu.PrefetchScalarGridSpec(
            num_scalar_prefetch=1, grid=(bsz, e),
            in_specs=[pl.BlockSpec((1, rows_, LANES), lambda b, j, idx: (b, 0, 0),
                                   pipeline_mode=pl.Buffered(1))],
            out_specs=pl.BlockSpec((1, 1, cap, d), lambda b, j, idx: (b, j, 0, 0)),
            scratch_shapes=[pltpu.VMEM((cap * nl, LANES), F32)]),
        compiler_params=_cparams("parallel", "arbitrary"), name="moe_gather",
    )(idx_flat, m_tok)


FFN_CHUNK = 512


def _ffn_kernel(xs_ref, gs_ref, w1_ref, w3_ref, w2_ref, y_ref, w1b_ref, w3b_ref, w2b_ref):
    @pl.when(pl.program_id(1) == 0)
    def _():
        w1b_ref[...] = w1_ref[0, 0].astype(BF16)
        w3b_ref[...] = w3_ref[0, 0].astype(BF16)
        w2b_ref[...] = w2_ref[0, 0].astype(BF16)

    xs = xs_ref[0, 0]
    de = w1b_ref.shape[1]
    fc = min(FFN_CHUNK, de)
    acc = None
    for c in range(de // fc):
        sl = slice(c * fc, (c + 1) * fc)
        h1 = jnp.dot(xs, w1b_ref[:, sl], preferred_element_type=F32)
        h3 = jnp.dot(xs, w3b_ref[:, sl], preferred_element_type=F32)
        h = (_silu(h1) * h3).astype(BF16)
        part = jnp.dot(h, w2b_ref[sl, :], preferred_element_type=F32)
        acc = part if acc is None else acc + part
    gs = gs_ref[0, 0]
    gate = gs[:, 1:2] + gs[:, 2:3] + gs[:, 3:4]
    _store_token_major(y_ref, (0, 0), acc * gate)


def _ffn(xs, slots, experts, layer):
    w1, w3, w2 = experts
    bsz, e, cap, d = xs.shape
    de = w1.shape[3]
    nl = d // LANES
    return pl.pallas_call(
        _ffn_kernel, out_shape=jax.ShapeDtypeStruct((bsz, e, cap * nl, LANES), F32),
        grid=(e, bsz),
        in_specs=[pl.BlockSpec((1, 1, cap, d), lambda j, b: (b, j, 0, 0)),
                  pl.BlockSpec((1, 1, cap, LANES), lambda j, b: (b, j, 0, 0)),
                  pl.BlockSpec((1, 1, d, de), lambda j, b: (layer, j, 0, 0)),
                  pl.BlockSpec((1, 1, d, de), lambda j, b: (layer, j, 0, 0)),
                  pl.BlockSpec((1, 1, de, d), lambda j, b: (layer, j, 0, 0))],
        out_specs=pl.BlockSpec((1, 1, cap * nl, LANES), lambda j, b: (b, j, 0, 0)),
        scratch_shapes=[pltpu.VMEM((d, de), BF16), pltpu.VMEM((d, de), BF16), pltpu.VMEM((de, d), BF16)],
        compiler_params=_cparams("parallel", "arbitrary"), name="moe_ffn",
    )(xs, slots, w1, w3, w2)


COMBINE_UNROLL = 4


def _combine_kernel(idx_ref, y_ref, acc_ref, *, cap, nl):
    b = pl.program_id(0)
    e = pl.program_id(1)
    base = (b * pl.num_programs(1) + e) * cap

    @pl.when(e == 0)
    def _():
        acc_ref[...] = jnp.zeros(acc_ref.shape, F32)

    def rows(ju, carry):
        j0 = ju * COMBINE_UNROLL
        ts = [pl.multiple_of(idx_ref[base + j0 + r] * nl, nl) for r in range(COMBINE_UNROLL)]
        vals = [acc_ref[0, pl.ds(ts[r], nl), :] + y_ref[0, 0, pl.ds(pl.multiple_of((j0 + r) * nl, nl), nl), :]
                for r in range(COMBINE_UNROLL)]
        for r in range(COMBINE_UNROLL):
            acc_ref[0, pl.ds(ts[r], nl), :] = vals[r]
        return carry

    lax.fori_loop(0, cap // COMBINE_UNROLL, rows, 0)


def _combine(idx_flat, y_tok, n, cap):
    bsz, e, rows_, _ = y_tok.shape
    nl = rows_ // cap
    return pl.pallas_call(
        functools.partial(_combine_kernel, cap=cap, nl=nl),
        out_shape=jax.ShapeDtypeStruct((bsz, n * nl, LANES), F32),
        grid_spec=pltpu.PrefetchScalarGridSpec(
            num_scalar_prefetch=1, grid=(bsz, e),
            in_specs=[pl.BlockSpec((1, 1, rows_, LANES), lambda b, j, idx: (b, j, 0, 0))],
            out_specs=pl.BlockSpec((1, n * nl, LANES), lambda b, j, idx: (b, 0, 0),
                                   pipeline_mode=pl.Buffered(1))),
        compiler_params=_cparams("parallel", "arbitrary"), name="moe_combine",
    )(idx_flat, y_tok)


def _resid_kernel(*refs, final):
    if final:
        x_ref, y_ref, mod_ref, g_ref, o_ref = refs
    else:
        x_ref, y_ref, mod_ref, o_ref = refs
    tm, d = x_ref.shape[1], x_ref.shape[2]
    x = x_ref[0] + mod_ref[0, 5:6, :] * _load_token_major(y_ref, (0,), tm, d)
    if final:
        x = x * lax.rsqrt(jnp.mean(x * x, axis=-1, keepdims=True) + EPS) * g_ref[...]
    o_ref[0] = x


def _resid(x, y_tok, mod, final_g=None):
    bsz, n, d = x.shape
    tm = _row_block(n)
    nl = d // LANES
    row = pl.BlockSpec((1, tm, d), lambda b, i: (b, i, 0))
    in_specs = [row, pl.BlockSpec((1, tm * nl, LANES), lambda b, i: (b, i, 0)),
                pl.BlockSpec((1, 6, d), lambda b, i: (b, 0, 0))]
    args = [x, y_tok, mod]
    if final_g is not None:
        in_specs.append(_full((1, d)))
        args.append(final_g.reshape(1, d))
    return pl.pallas_call(
        functools.partial(_resid_kernel, final=final_g is not None),
        out_shape=jax.ShapeDtypeStruct((bsz, n, d), F32), grid=(bsz, n // tm),
        in_specs=in_specs, out_specs=row,
        compiler_params=_cparams("parallel", "parallel"), name="moe_resid",
    )(*args)


def _moe(x_new, m_tok, aff_t, mod, experts, layer, final_g=None):
    bsz, n, d = x_new.shape
    e = aff_t.shape[1]
    cap = EC_CAPACITY * n // e
    lpos, tile_off = _select(aff_t, cap)
    slots = _slots(lpos, tile_off, aff_t, cap)
    idx_flat = slots[..., 0].astype(I32).reshape(-1)
    xs = _gather(idx_flat, m_tok, e, cap)
    y_tok = _ffn(xs, slots, experts, layer)
    moe_tok = _combine(idx_flat, y_tok, n, cap)
    return _resid(x_new, moe_tok, mod, final_g)


def _pos_embed(n, d):
    rows = n // GRID_W
    quarter = d // 4
    omega = 1.0 / (POS_BASE ** (jnp.arange(quarter, dtype=F32) / quarter))
    r = jnp.arange(rows, dtype=F32)[:, None] * omega
    cc = jnp.arange(GRID_W, dtype=F32)[:, None] * omega
    row_emb = jnp.concatenate([jnp.sin(r), jnp.cos(r)], axis=-1)
    col_emb = jnp.concatenate([jnp.sin(cc), jnp.cos(cc)], axis=-1)
    emb = jnp.concatenate([
        jnp.broadcast_to(row_emb[:, None, :], (rows, GRID_W, d // 2)),
        jnp.broadcast_to(col_emb[None, :, :], (rows, GRID_W, d // 2))], axis=-1)
    return emb.reshape(rows * GRID_W, d)


def kernel(x, c, ctx, c_ctx, w_mod, b_mod, g_mix, g_ffn, conv_w_in, conv_b_in, conv_dw, conv_dw_b, conv_ln_g, conv_ln_b, conv_w_out, conv_b_out, s5_lam_re, s5_lam_im, s5_log_dt, s5_b_re, s5_b_im, s5_c_re, s5_c_im, s5_d, s5_w_glu, s5_b_glu, lru_w_y, lru_b_y, lru_w_x, lru_b_x, lru_conv_w, lru_conv_b, lru_w_a, lru_b_a, lru_w_i, lru_b_i, lru_lam, lru_w_out, lru_b_out, moe_router, moe_w1, moe_w3, moe_w2, final_g):
    bsz, n, d = x.shape
    nctx = ctx.shape[1]
    depth = w_mod.shape[0]
    n_mixers = 3
    reader_layers = [i for i in range(depth) if i % n_mixers != 0]
    last_reader = max(reader_layers) if reader_layers else -1

    rows = -(-(bsz + 1) // SUBLANES) * SUBLANES
    cond = jnp.zeros((rows, d), F32).at[:bsz].set(c).at[bsz].set(c_ctx)
    mods = _modulation(cond, w_mod, b_mod).reshape(depth, rows, 6, d)
    pe = _pos_embed(n, d)
    assert d == SUBLANES * LANES, "token-major MoE rows assume one (8, 128) tile per token"
    experts = (moe_w1, moe_w3, moe_w2)

    x_lat, x_ctx = x, ctx
    for i in range(depth):
        kind, j = i % n_mixers, i // n_mixers
        ctx_in = i <= last_reader
        ctx_out = i < last_reader
        mod_lat = mods[i, :bsz]
        mod_ctx = jnp.broadcast_to(mods[i, bsz][None], (bsz, 6, d))
        lat_pe = pe if i == 0 else None
        streams = [(x_lat, mod_lat, lat_pe)]
        if ctx_out:
            streams.append((x_ctx, mod_ctx, None))

        if kind == 0:
            post = []
            for xs_, mod_, pe_ in streams:
                u = _conv_in(xs_, pe_, mod_, g_mix[i], conv_w_in[j], conv_b_in[j])
                post.append(_conv_out(u, xs_, pe_, mod_, conv_dw[j], conv_dw_b[j], conv_ln_g[j],
                                      conv_ln_b[j], conv_w_out[j], conv_b_out[j], g_ffn[i], moe_router[i]))
        elif kind == 1:
            te, fo, tab = _s5_tables(s5_lam_re[j], s5_lam_im[j], s5_log_dt[j], s5_b_re[j], s5_b_im[j],
                                     s5_c_re[j], s5_c_im[j])
            u_lat = _norm_mod(x_lat, mod_lat, g_mix[i])
            u_ctx = _norm_mod(x_ctx, mod_ctx, g_mix[i])
            uc = _to_chunks(u_ctx)
            u_all = jnp.concatenate([uc, _to_chunks(u_lat), uc], axis=2)
            y_lat, y_ctx = _s5_core(u_all, te, fo, tab, nctx // S5_CHUNK, n // S5_CHUNK)
            post = [_s5_out(_from_chunks(y_lat), u_lat, x_lat, mod_lat, s5_d[j], s5_w_glu[j], s5_b_glu[j],
                            g_ffn[i], moe_router[i])]
            if ctx_out:
                post.append(_s5_out(_from_chunks(y_ctx), u_ctx, x_ctx, mod_ctx, s5_d[j], s5_w_glu[j],
                                    s5_b_glu[j], g_ffn[i], moe_router[i]))
        else:
            scan_args = (lru_conv_w[j], lru_conv_b[j], lru_w_a[j], lru_b_a[j], lru_w_i[j], lru_b_i[j], lru_lam[j])
            xp_ctx, gate_ctx = _lru_in(x_ctx, mod_ctx, g_mix[i], lru_w_x[j], lru_b_x[j], lru_w_y[j], lru_b_y[j])
            rf_c, rb_c, h_ctx = _lru_scan(xp_ctx, jnp.zeros((bsz, 2, d), F32), *scan_args)
            xp_lat, gate_lat = _lru_in(x_lat, mod_lat, g_mix[i], lru_w_x[j], lru_b_x[j], lru_w_y[j], lru_b_y[j])
            rf_l, rb_l, _ = _lru_scan(xp_lat, h_ctx, *scan_args)
            post = [_lru_out(rf_l, rb_l, gate_lat, x_lat, mod_lat, lru_w_out[j], lru_b_out[j],
                             g_ffn[i], moe_router[i])]
            if ctx_out:
                post.append(_lru_out(rf_c, rb_c, gate_ctx, x_ctx, mod_ctx, lru_w_out[j], lru_b_out[j],
                                     g_ffn[i], moe_router[i]))

        xn, m, aff = post[0]
        x_lat = _moe(xn, m, aff, mod_lat, experts, i, final_g if i == depth - 1 else None)
        if ctx_out:
            xn, m, aff = post[1]
            x_ctx = _moe(xn, m, aff, mod_ctx, experts, i)
    return x_lat
```

```python
import functools
import math

import jax
import jax.numpy as jnp
from jax import lax
from jax.experimental import pallas as pl
from jax.experimental.pallas import tpu as pltpu

F32 = jnp.float32
BF16 = jnp.bfloat16
I32 = jnp.int32
HIGHEST = lax.Precision.HIGHEST

EPS = 1e-6
N_EXPERTS = 16
EC_CAPACITY = 2
CONV_WIDTH = 31
LRU_CONV = 4
LRU_HEADS = 8
LRU_C = 8.0
S5_GROUP = 16
S5_STATE = 64
S5_CHUNK = 16
GRID_W = 64
POS_BASE = 10000.0

LANES = 128
SUBLANES = 8
ROW_BLOCK = 512
VMEM_LIMIT = 56 * 1024 * 1024


def _cparams(*sem):
    return pltpu.CompilerParams(dimension_semantics=sem, vmem_limit_bytes=VMEM_LIMIT)


def _full(shape):
    nd = len(shape)
    return pl.BlockSpec(shape, lambda *_: (0,) * nd)


def _rms_mod(x, g, scale, shift):
    y = x * lax.rsqrt(jnp.mean(x * x, axis=-1, keepdims=True) + EPS)
    return (y * g) * (1.0 + scale) + shift


def _silu(x):
    return x * jax.nn.sigmoid(x)


def _row_block(n):
    return min(ROW_BLOCK, n)


def _mod_kernel(c_ref, w_ref, b_ref, o_ref):
    s = _silu(c_ref[...])
    o_ref[0] = jnp.dot(s, w_ref[0], precision=HIGHEST, preferred_element_type=F32) + b_ref[0]


def _modulation(cond, w_mod, b_mod):
    depth, d, d6 = w_mod.shape
    rows = cond.shape[0]
    tn = 1536
    return pl.pallas_call(
        _mod_kernel,
        out_shape=jax.ShapeDtypeStruct((depth, rows, d6), F32),
        grid=(depth, d6 // tn),
        in_specs=[pl.BlockSpec((rows, d), lambda i, j: (0, 0)),
                  pl.BlockSpec((1, d, tn), lambda i, j: (i, 0, j)),
                  pl.BlockSpec((1, 1, tn), lambda i, j: (i, 0, j))],
        out_specs=pl.BlockSpec((1, rows, tn), lambda i, j: (i, 0, j)),
        compiler_params=_cparams("parallel", "parallel"),
        name="modulation",
    )(cond, w_mod, b_mod.reshape(depth, 1, d6))


def _store_token_major(ref, lead, val):
    rows, d = val.shape
    nl = d // LANES
    for c in range(nl):
        ref[lead + (pl.ds(c, rows, stride=nl), slice(None))] = val[:, c * LANES:(c + 1) * LANES]


def _load_token_major(ref, lead, rows, d):
    nl = d // LANES
    return jnp.concatenate([ref[lead + (pl.ds(c, rows, stride=nl), slice(None))] for c in range(nl)], axis=1)


def _post(xn, mod_ref, gffn_ref, wr_ref, xo_ref, m_ref, aff_ref):
    xo_ref[0] = xn
    m = _rms_mod(xn, gffn_ref[...], mod_ref[0, 4:5, :], mod_ref[0, 3:4, :])
    _store_token_major(m_ref, (0,), m)
    logits = lax.dot_general(wr_ref[...], m, (((1,), (1,)), ((), ())),
                             precision=HIGHEST, preferred_element_type=F32)
    ex = jnp.exp(logits - jnp.max(logits, axis=0, keepdims=True))
    aff_ref[0] = ex / jnp.sum(ex, axis=0, keepdims=True)


def _post_specs(bsz, n, d, tm):
    nl = d // LANES
    out_shape = (jax.ShapeDtypeStruct((bsz, n, d), F32),
                 jax.ShapeDtypeStruct((bsz, n * nl, LANES), F32),
                 jax.ShapeDtypeStruct((bsz, N_EXPERTS, n), F32))
    out_specs = (pl.BlockSpec((1, tm, d), lambda b, i: (b, i, 0)),
                 pl.BlockSpec((1, tm * nl, LANES), lambda b, i: (b, i, 0)),
                 pl.BlockSpec((1, N_EXPERTS, tm), lambda b, i: (b, 0, i)))
    return out_shape, out_specs


def _conv_in_kernel(*refs, has_pe):
    if has_pe:
        x_ref, pe_ref, mod_ref, g_ref, w_ref, b_ref, u_ref = refs
        x = x_ref[0] + pe_ref[...]
    else:
        x_ref, mod_ref, g_ref, w_ref, b_ref, u_ref = refs
        x = x_ref[0]
    h = _rms_mod(x, g_ref[...], mod_ref[0, 1:2, :], mod_ref[0, 0:1, :]).astype(BF16)
    z = jnp.dot(h, w_ref[...], preferred_element_type=F32) + b_ref[...]
    d = z.shape[1] // 2
    u_ref[0] = z[:, :d] * jax.nn.sigmoid(z[:, d:])


def _conv_in(x, pe, mod, g_mix, w_in, b_in):
    bsz, n, d = x.shape
    tm = _row_block(n)
    row = pl.BlockSpec((1, tm, d), lambda b, i: (b, i, 0))
    in_specs = [row]
    args = [x]
    if pe is not None:
        in_specs.append(pl.BlockSpec((tm, d), lambda b, i: (i, 0)))
        args.append(pe)
    in_specs += [pl.BlockSpec((1, 6, d), lambda b, i: (b, 0, 0)), _full((1, d)),
                 _full((d, 2 * d)), _full((1, 2 * d))]
    args += [mod, g_mix.reshape(1, d), w_in.astype(BF16), b_in.reshape(1, 2 * d)]
    return pl.pallas_call(
        functools.partial(_conv_in_kernel, has_pe=pe is not None),
        out_shape=jax.ShapeDtypeStruct((bsz, n, d), F32),
        grid=(bsz, n // tm), in_specs=in_specs, out_specs=row,
        compiler_params=_cparams("parallel", "parallel"), name="conv_in",
    )(*args)


CONV_HALO = 16
CONV_ROWS = 128


def _conv_out_kernel(*refs, has_pe, tm):
    if has_pe:
        (u_ref, up_ref, un_ref, x_ref, pe_ref, mod_ref, dw_ref, dwb_ref, lng_ref, lnb_ref,
         w_ref, b_ref, gffn_ref, wr_ref, xo_ref, m_ref, aff_ref, ext_ref, cv_ref) = refs
        x = x_ref[0] + pe_ref[...]
    else:
        (u_ref, up_ref, un_ref, x_ref, mod_ref, dw_ref, dwb_ref, lng_ref, lnb_ref,
         w_ref, b_ref, gffn_ref, wr_ref, xo_ref, m_ref, aff_ref, ext_ref, cv_ref) = refs
        x = x_ref[0]
    i = pl.program_id(1)
    last = pl.num_programs(1) - 1
    ext_ref[0:CONV_HALO, :] = jnp.where(i > 0, up_ref[0], 0.0)
    ext_ref[CONV_HALO:CONV_HALO + tm, :] = u_ref[0]
    ext_ref[CONV_HALO + tm:2 * CONV_HALO + tm, :] = jnp.where(i < last, un_ref[0], 0.0)

    first_tap = CONV_HALO - CONV_WIDTH // 2

    rows = min(CONV_ROWS, tm)
    for lt in range(ext_ref.shape[1] // LANES):
        ls = slice(lt * LANES, (lt + 1) * LANES)
        for c in range(tm // rows):
            base = c * rows
            acc = jnp.zeros((rows, LANES), F32)
            for s in range(SUBLANES):
                part = None
                for o in range(first_tap, first_tap + CONV_WIDTH):
                    if o % SUBLANES != s:
                        continue
                    lo = base + o - s
                    term = dw_ref[o - first_tap:o - first_tap + 1, ls] * ext_ref[lo:lo + rows + SUBLANES, ls]
                    part = term if part is None else part + term
                acc = acc + part[s:s + rows]
            cv_ref[base:base + rows, ls] = acc
    cv = cv_ref[...] + dwb_ref[...]
    mu = jnp.mean(cv, axis=-1, keepdims=True)
    xc = cv - mu
    ln = xc * lax.rsqrt(jnp.mean(xc * xc, axis=-1, keepdims=True) + EPS) * lng_ref[...] + lnb_ref[...]
    y = jnp.dot(_silu(ln).astype(BF16), w_ref[...], preferred_element_type=F32) + b_ref[...]
    _post(x + mod_ref[0, 2:3, :] * y, mod_ref, gffn_ref, wr_ref, xo_ref, m_ref, aff_ref)


def _conv_out(u, x, pe, mod, dw, dw_b, ln_g, ln_b, w_out, b_out, g_ffn, w_router):
    bsz, n, d = x.shape
    tm = _row_block(n)
    hb = tm // CONV_HALO
    nh = n // CONV_HALO
    row = pl.BlockSpec((1, tm, d), lambda b, i: (b, i, 0))
    in_specs = [row,
                pl.BlockSpec((1, CONV_HALO, d), lambda b, i: (b, jnp.maximum(i * hb - 1, 0), 0)),
                pl.BlockSpec((1, CONV_HALO, d), lambda b, i: (b, jnp.minimum((i + 1) * hb, nh - 1), 0)),
                row]
    args = [u, u, u, x]
    if pe is not None:
        in_specs.append(pl.BlockSpec((tm, d), lambda b, i: (i, 0)))
        args.append(pe)
    vec = _full((1, d))
    in_specs += [pl.BlockSpec((1, 6, d), lambda b, i: (b, 0, 0)), _full((CONV_WIDTH, d)), vec, vec, vec,
                 _full((d, d)), vec, vec, _full((N_EXPERTS, d))]
    args += [mod, dw, dw_b.reshape(1, d), ln_g.reshape(1, d), ln_b.reshape(1, d),
             w_out.astype(BF16), b_out.reshape(1, d), g_ffn.reshape(1, d), w_router.T]
    out_shape, out_specs = _post_specs(bsz, n, d, tm)
    return pl.pallas_call(
        functools.partial(_conv_out_kernel, has_pe=pe is not None, tm=tm),
        out_shape=out_shape, grid=(bsz, n // tm), in_specs=in_specs, out_specs=out_specs,
        scratch_shapes=[pltpu.VMEM((tm + 2 * CONV_HALO, d), F32), pltpu.VMEM((tm, d), F32)],
        compiler_params=_cparams("parallel", "parallel"), name="conv_out",
    )(*args)


GROUPS_PER_TILE = LANES // S5_GROUP
STEPS_PER_TILE = LANES // S5_GROUP


def _s5_in_kernel(x_ref, mod_ref, g_ref, ug_ref, slab_ref, *, tm):
    u = _rms_mod(x_ref[0], g_ref[...], mod_ref[0, 1:2, :], mod_ref[0, 0:1, :])
    nl = u.shape[1] // LANES
    nc = tm // S5_CHUNK
    for lt in range(nl):
        slab_ref[lt] = u[:, lt * LANES:(lt + 1) * LANES]
    lane_step = lax.broadcasted_iota(I32, (nc, LANES), 1) // S5_GROUP
    for lt in range(nl):
        steps = [slab_ref[lt, pl.ds(t, nc, stride=S5_CHUNK), :] for t in range(S5_CHUNK)]
        for g8 in range(GROUPS_PER_TILE):
            for half in range(S5_CHUNK // STEPS_PER_TILE):
                acc = jnp.zeros((nc, LANES), F32)
                for tq in range(STEPS_PER_TILE):
                    shift = ((tq - g8) * S5_GROUP) % LANES
                    src = steps[half * STEPS_PER_TILE + tq]
                    acc = jnp.where(lane_step == tq, pltpu.roll(src, shift, 1) if shift else src, acc)
                ug_ref[0, lt * GROUPS_PER_TILE + g8, :, half * LANES:(half + 1) * LANES] = acc.astype(BF16)


def _s5_in(x, mod, g_mix):
    bsz, n, d = x.shape
    tm = _row_block(n)
    groups = d // S5_GROUP
    lw = S5_CHUNK * S5_GROUP
    return pl.pallas_call(
        functools.partial(_s5_in_kernel, tm=tm),
        out_shape=jax.ShapeDtypeStruct((bsz, groups, n // S5_CHUNK, lw), BF16),
        grid=(bsz, n // tm),
        in_specs=[pl.BlockSpec((1, tm, d), lambda b, i: (b, i, 0)),
                  pl.BlockSpec((1, 6, d), lambda b, i: (b, 0, 0)), _full((1, d))],
        out_specs=pl.BlockSpec((1, groups, tm // S5_CHUNK, lw), lambda b, i: (b, 0, i, 0)),
        scratch_shapes=[pltpu.VMEM((d // LANES, tm, LANES), F32)],
        compiler_params=_cparams("parallel", "parallel"), name="s5_in",
    )(x, mod, g_mix.reshape(1, d))


def _s5_tables(lam_re, lam_im, log_dt, b_re, b_im, c_re, c_im):
    L = S5_CHUNK
    dt = jnp.exp(log_dt)[:, :, None]
    mag = jnp.exp(lam_re * dt)
    ang = lam_im * dt
    lb_re = mag * jnp.cos(ang)
    lb_im = mag * jnp.sin(ang)
    nr = lb_re - 1.0
    ni = lb_im
    den = lam_re * lam_re + lam_im * lam_im
    coef_re = ((nr * lam_re + ni * lam_im) / den)[..., None]
    coef_im = ((ni * lam_re - nr * lam_im) / den)[..., None]
    bb_re = coef_re * b_re - coef_im * b_im
    bb_im = coef_re * b_im + coef_im * b_re

    def powers(steps):
        st = steps.astype(F32)
        m = jnp.exp((lam_re * dt)[..., None] * st)
        a = (lam_im * dt)[..., None] * st
        return m * jnp.cos(a), m * jnp.sin(a)

    pw_re, pw_im = powers(jnp.arange(L + 1))
    cp_re = c_re[:, :, None] * pw_re.transpose(0, 1, 3, 2)[:, :, :, None, :] \
        - c_im[:, :, None] * pw_im.transpose(0, 1, 3, 2)[:, :, :, None, :]
    cp_im = c_re[:, :, None] * pw_im.transpose(0, 1, 3, 2)[:, :, :, None, :] \
        + c_im[:, :, None] * pw_re.transpose(0, 1, 3, 2)[:, :, :, None, :]
    kern = (jnp.einsum("dgtkp,dgpj->dgtkj", cp_re[:, :, :L], bb_re, precision=HIGHEST)
            - jnp.einsum("dgtkp,dgpj->dgtkj", cp_im[:, :, :L], bb_im, precision=HIGHEST))
    lbb_re = pw_re.transpose(0, 1, 3, 2)[..., None] * bb_re[:, :, None] \
        - pw_im.transpose(0, 1, 3, 2)[..., None] * bb_im[:, :, None]
    lbb_im = pw_re.transpose(0, 1, 3, 2)[..., None] * bb_im[:, :, None] \
        + pw_im.transpose(0, 1, 3, 2)[..., None] * bb_re[:, :, None]

    s_idx = jnp.arange(L)[:, None]
    t_idx = jnp.arange(L)[None, :]
    G = lam_re.shape[1]
    K = S5_GROUP
    P = S5_STATE
    tes, fos = [], []
    for direction in range(2):
        if direction == 0:
            lag = t_idx - s_idx
            e_pow = (L - 1) - jnp.arange(L)
            f_pow = jnp.arange(L) + 1
        else:
            lag = s_idx - t_idx
            e_pow = jnp.arange(L)
            f_pow = L - jnp.arange(L)
        valid = lag >= 0
        kd = kern[direction][:, jnp.clip(lag, 0, L - 1)]
        kd = jnp.where(valid[None, :, :, None, None], kd, 0.0)
        tmat = kd.transpose(0, 1, 4, 2, 3).reshape(G, L * K, L * K)
        e_re = lbb_re[direction][:, e_pow]
        e_im = lbb_im[direction][:, e_pow]
        emat = jnp.concatenate([e_re.transpose(0, 1, 3, 2).reshape(G, L * K, P),
                                e_im.transpose(0, 1, 3, 2).reshape(G, L * K, P)], axis=-1)
        tes.append(jnp.concatenate([tmat, emat], axis=-1))
        f_re = cp_re[direction][:, f_pow]
        f_im = cp_im[direction][:, f_pow]
        fos.append(jnp.concatenate([f_re.transpose(0, 3, 1, 2).reshape(G, P, L * K),
                                    -f_im.transpose(0, 3, 1, 2).reshape(G, P, L * K)], axis=1))
    te = jnp.stack(tes).astype(BF16)
    fo = jnp.stack(fos).astype(BF16)
    a_re, a_im = powers(jnp.full((1,), L))
    a_re, a_im = a_re[..., 0], a_im[..., 0]
    tab = jnp.stack([jnp.concatenate([a_re, a_re], axis=-1),
                     jnp.concatenate([-a_im, a_im], axis=-1)], axis=1)
    return te, fo, tab


S5_GROUPS_PER_STEP = 8


S5_SCAN_UNROLL = 4


def _s5_core_kernel(u_ref, tef_ref, teb_ref, fof_ref, fob_ref, tab_ref, yl_ref, yc_ref,
                    yf_ref, yb_ref, hf_ref, hb_ref, sf_ref, sb_ref, *, nctx, nlat):
    gb = S5_GROUPS_PER_STEP
    rows = nctx + nlat
    lw = S5_CHUNK * S5_GROUP
    for g in range(gb):
        u = u_ref[0, g]
        zf = jnp.dot(u[0:rows], tef_ref[g], preferred_element_type=F32)
        yf_ref[g] = zf[:, :lw]
        hf_ref[pl.ds(g, rows, stride=gb), :] = zf[:, lw:]
        sf_ref[pl.ds(g, rows, stride=gb), :] = pltpu.roll(zf[:, lw:], S5_STATE, 1)
        zb = jnp.dot(u[nctx:nctx + rows], teb_ref[g], preferred_element_type=F32)
        yb_ref[g] = zb[:, :lw]
        hb_ref[pl.ds(g, rows, stride=gb), :] = zb[:, lw:]
        sb_ref[pl.ds(g, rows, stride=gb), :] = pltpu.roll(zb[:, lw:], S5_STATE, 1)

    cat_f, swp_f = tab_ref[0, 0], tab_ref[0, 1]
    cat_b, swp_b = tab_ref[1, 0], tab_ref[1, 1]

    def scan_rows(i, carry):
        hf, sf, hb, sb = carry
        rf = pl.multiple_of(i * gb, gb)
        rb = pl.multiple_of((rows - 1 - i) * gb, gb)
        ef, esf = hf_ref[pl.ds(rf, gb), :], sf_ref[pl.ds(rf, gb), :]
        eb, esb = hb_ref[pl.ds(rb, gb), :], sb_ref[pl.ds(rb, gb), :]
        hf_ref[pl.ds(rf, gb), :] = hf
        hb_ref[pl.ds(rb, gb), :] = hb
        return (cat_f * hf + swp_f * sf + ef, cat_f * sf - swp_f * hf + esf,
                cat_b * hb + swp_b * sb + eb, cat_b * sb - swp_b * hb + esb)

    zero = jnp.zeros((gb, 2 * S5_STATE), F32)
    lax.fori_loop(0, rows, scan_rows, (zero, zero, zero, zero), unroll=S5_SCAN_UNROLL)

    for g in range(gb):
        hin_f = hf_ref[pl.ds(g, rows, stride=gb), :].astype(BF16)
        hin_b = hb_ref[pl.ds(g, rows, stride=gb), :].astype(BF16)
        yf = yf_ref[g] + jnp.dot(hin_f, fof_ref[g], preferred_element_type=F32)
        yb = yb_ref[g] + jnp.dot(hin_b, fob_ref[g], preferred_element_type=F32)
        yl_ref[0, g] = yf[nctx:nctx + nlat] + yb[0:nlat]
        yc_ref[0, g] = yf[0:nctx] + yb[nlat:nlat + nctx]


def _s5_core(u_all, te, fo, tab, nctx, nlat):
    bsz, G, rtot, lw = u_all.shape
    gb = S5_GROUPS_PER_STEP
    rows = nctx + nlat
    w = 2 * S5_STATE
    gspec3 = lambda shape: pl.BlockSpec((gb,) + shape, lambda b, j: (j, 0, 0))
    return pl.pallas_call(
        functools.partial(_s5_core_kernel, nctx=nctx, nlat=nlat),
        out_shape=(jax.ShapeDtypeStruct((bsz, G, nlat, lw), F32),
                   jax.ShapeDtypeStruct((bsz, G, nctx, lw), F32)),
        grid=(bsz, G // gb),
        in_specs=[pl.BlockSpec((1, gb, rtot, lw), lambda b, j: (b, j, 0, 0)),
                  gspec3((lw, lw + w)), gspec3((lw, lw + w)), gspec3((w, lw)), gspec3((w, lw)),
                  pl.BlockSpec((2, 2, gb, w), lambda b, j: (0, 0, j, 0))],
        out_specs=(pl.BlockSpec((1, gb, nlat, lw), lambda b, j: (b, j, 0, 0)),
                   pl.BlockSpec((1, gb, nctx, lw), lambda b, j: (b, j, 0, 0))),
        scratch_shapes=[pltpu.VMEM((gb, rows, lw), F32), pltpu.VMEM((gb, rows, lw), F32)]
        + [pltpu.VMEM((rows * gb, w), F32)] * 4,
        compiler_params=_cparams("parallel", "parallel"), name="s5_core",
    )(u_all, te[0], te[1], fo[0], fo[1], tab)


def _s5_out_kernel(yg_ref, x_ref, mod_ref, gmix_ref, dsk_ref, w_ref, b_ref, gffn_ref, wr_ref,
                   xo_ref, m_ref, aff_ref, slab_ref, *, tm):
    nl = x_ref.shape[2] // LANES
    nc = tm // S5_CHUNK
    lane_group = lax.broadcasted_iota(I32, (nc, LANES), 1) // S5_GROUP
    for lt in range(nl):
        for t in range(S5_CHUNK):
            half, tq = divmod(t, STEPS_PER_TILE)
            acc = jnp.zeros((nc, LANES), F32)
            for g8 in range(GROUPS_PER_TILE):
                src = yg_ref[0, lt * GROUPS_PER_TILE + g8, :, half * LANES:(half + 1) * LANES]
                shift = ((g8 - tq) * S5_GROUP) % LANES
                acc = jnp.where(lane_group == g8, pltpu.roll(src, shift, 1) if shift else src, acc)
            slab_ref[lt, pl.ds(t, nc, stride=S5_CHUNK), :] = acc
    ssm = jnp.concatenate([slab_ref[lt] for lt in range(nl)], axis=1)
    x = x_ref[0]
    u = _rms_mod(x, gmix_ref[...], mod_ref[0, 1:2, :], mod_ref[0, 0:1, :])
    y = ssm + dsk_ref[...] * u
    z = jnp.dot(jax.nn.gelu(y).astype(BF16), w_ref[...], preferred_element_type=F32) + b_ref[...]
    d = z.shape[1] // 2
    out = z[:, :d] * jax.nn.sigmoid(z[:, d:])
    _post(x + mod_ref[0, 2:3, :] * out, mod_ref, gffn_ref, wr_ref, xo_ref, m_ref, aff_ref)


def _s5_out(yg, x, mod, g_mix, d_skip, w_glu, b_glu, g_ffn, w_router):
    bsz, n, d = x.shape
    tm = _row_block(n)
    groups = d // S5_GROUP
    lw = S5_CHUNK * S5_GROUP
    row = pl.BlockSpec((1, tm, d), lambda b, i: (b, i, 0))
    vec = _full((1, d))
    out_shape, out_specs = _post_specs(bsz, n, d, tm)
    return pl.pallas_call(
        functools.partial(_s5_out_kernel, tm=tm), out_shape=out_shape, grid=(bsz, n // tm),
        in_specs=[pl.BlockSpec((1, groups, tm // S5_CHUNK, lw), lambda b, i: (b, 0, i, 0)),
                  row, pl.BlockSpec((1, 6, d), lambda b, i: (b, 0, 0)), vec, vec,
                  _full((d, 2 * d)), _full((1, 2 * d)), vec, _full((N_EXPERTS, d))],
        out_specs=out_specs,
        scratch_shapes=[pltpu.VMEM((d // LANES, tm, LANES), F32)],
        compiler_params=_cparams("parallel", "parallel"), name="s5_out",
    )(yg, x, mod, g_mix.reshape(1, d), d_skip.reshape(1, d), w_glu.astype(BF16), b_glu.reshape(1, 2 * d),
      g_ffn.reshape(1, d), w_router.T)


def _lru_in_kernel(x_ref, mod_ref, g_ref, wx_ref, bx_ref, wy_ref, by_ref, xp_ref, gate_ref):
    h = _rms_mod(x_ref[0], g_ref[...], mod_ref[0, 1:2, :], mod_ref[0, 0:1, :]).astype(BF16)
    xp_ref[0] = jnp.dot(h, wx_ref[...], preferred_element_type=F32) + bx_ref[...]
    gate_ref[0] = jax.nn.gelu(jnp.dot(h, wy_ref[...], preferred_element_type=F32) + by_ref[...])


def _lru_in(x, mod, g_mix, w_x, b_x, w_y, b_y):
    bsz, n, d = x.shape
    tm = _row_block(n)
    row = pl.BlockSpec((1, tm, d), lambda b, i: (b, i, 0))
    vec = _full((1, d))
    return pl.pallas_call(
        _lru_in_kernel,
        out_shape=(jax.ShapeDtypeStruct((bsz, n, d), F32), jax.ShapeDtypeStruct((bsz, n, d), F32)),
        grid=(bsz, n // tm),
        in_specs=[row, pl.BlockSpec((1, 6, d), lambda b, i: (b, 0, 0)), vec,
                  _full((d, d)), vec, _full((d, d)), vec],
        out_specs=(row, row), compiler_params=_cparams("parallel", "parallel"), name="lru_in",
    )(x, mod, g_mix.reshape(1, d), w_x.astype(BF16), b_x.reshape(1, d), w_y.astype(BF16), b_y.reshape(1, d))


LRU_HALO = 8
LRU_SCAN_UNROLL = 8


def _lru_scan_kernel(xf_ref, xfp_ref, xfn_ref, xb_ref, xbp_ref, xbn_ref, h0_ref, cw_ref, cb_ref,
                     wai_ref, bai_ref, lam_ref, rf_ref, rb_ref, hfin_ref,
                     ext_ref, a_ref, b_ref, r_ref, h_ref, *, tm):
    i = pl.program_id(1)
    last = pl.num_programs(1) - 1
    d = ext_ref.shape[1]
    hw = d // LRU_HEADS

    @pl.when(i == 0)
    def _():
        h_ref[...] = h0_ref[0]

    lam = lam_ref[...]
    nl = -lam
    softplus = jnp.maximum(nl, 0.0) + jnp.log1p(jnp.exp(-jnp.abs(nl)))
    c8 = -LRU_C * softplus

    def sigmoid(v):
        return 0.5 * jnp.tanh(0.5 * v) + 0.5

    def gates(dr, main_ref, prev_ref, next_ref, blk):
        ext_ref[0:LRU_HALO, :] = jnp.where(blk > 0, prev_ref[0], 0.0)
        ext_ref[LRU_HALO:LRU_HALO + tm, :] = main_ref[0]
        ext_ref[LRU_HALO + tm:2 * LRU_HALO + tm, :] = jnp.where(blk < last, next_ref[0], 0.0)
        xl = cb_ref[...]
        for k in range(LRU_CONV):
            off = LRU_HALO - LRU_CONV // 2 + k
            xl = xl + cw_ref[k:k + 1, :] * ext_ref[off:off + tm, :]
        xlb = xl.astype(BF16)
        for hd in range(LRU_HEADS):
            sl = slice(hd * hw, (hd + 1) * hw)
            z = jnp.dot(xlb[:, sl], wai_ref[dr, hd], preferred_element_type=F32)
            r = sigmoid(z[:, :hw] + bai_ref[dr, 0:1, sl])
            ig = sigmoid(z[:, hw:] + bai_ref[dr, 1:2, sl])
            log_a = c8[dr:dr + 1, sl] * r
            th = jnp.tanh(log_a)
            a_ref[dr, pl.ds(hd, tm, stride=LRU_HEADS), :] = jnp.exp(log_a)
            b_ref[dr, pl.ds(hd, tm, stride=LRU_HEADS), :] = jnp.sqrt(-2.0 * th / (1.0 - th)) * (ig * xl[:, sl])

    gates(0, xf_ref, xfp_ref, xfn_ref, i)
    gates(1, xb_ref, xbp_ref, xbn_ref, last - i)

    def step(t, carry):
        hf, hb = carry
        rf = pl.multiple_of(t * LRU_HEADS, LRU_HEADS)
        rb = pl.multiple_of((tm - 1 - t) * LRU_HEADS, LRU_HEADS)
        hf = a_ref[0, pl.ds(rf, LRU_HEADS), :] * hf + b_ref[0, pl.ds(rf, LRU_HEADS), :]
        hb = a_ref[1, pl.ds(rb, LRU_HEADS), :] * hb + b_ref[1, pl.ds(rb, LRU_HEADS), :]
        r_ref[0, pl.ds(rf, LRU_HEADS), :] = hf
        r_ref[1, pl.ds(rb, LRU_HEADS), :] = hb
        return hf, hb

    hf, hb = lax.fori_loop(0, tm, step, (h_ref[0], h_ref[1]), unroll=LRU_SCAN_UNROLL)
    h_ref[0] = hf
    h_ref[1] = hb
    hfin_ref[0] = h_ref[...]
    rf_ref[0] = _load_token_major(r_ref, (0,), tm, d)
    rb_ref[0] = _load_token_major(r_ref, (1,), tm, d)


def _lru_scan(xp, h0, conv_w, conv_b, w_a, b_a, w_i, b_i, lam):
    bsz, n, d = xp.shape
    tm = _row_block(n)
    nb = n // tm
    hb = tm // LRU_HALO
    nh = n // LRU_HALO
    fwd = lambda b, i: (b, i, 0)
    bwd = lambda b, i: (b, nb - 1 - i, 0)
    halo = lambda f: pl.BlockSpec((1, LRU_HALO, d), f)
    wai = jnp.concatenate([w_a, w_i], axis=-1).astype(BF16)
    bai = jnp.stack([b_a, b_i], axis=1)
    assert d // LRU_HEADS == LANES, "token-major scan assumes one 128-lane tile per head"
    state = pl.BlockSpec((1, 2, LRU_HEADS, LANES), lambda b, i: (b, 0, 0, 0))
    return pl.pallas_call(
        functools.partial(_lru_scan_kernel, tm=tm),
        out_shape=(jax.ShapeDtypeStruct((bsz, n, d), F32), jax.ShapeDtypeStruct((bsz, n, d), F32),
                   jax.ShapeDtypeStruct((bsz, 2, LRU_HEADS, LANES), F32)),
        grid=(bsz, nb),
        in_specs=[pl.BlockSpec((1, tm, d), fwd),
                  halo(lambda b, i: (b, jnp.maximum(i * hb - 1, 0), 0)),
                  halo(lambda b, i: (b, jnp.minimum((i + 1) * hb, nh - 1), 0)),
                  pl.BlockSpec((1, tm, d), bwd),
                  halo(lambda b, i: (b, jnp.maximum((nb - 1 - i) * hb - 1, 0), 0)),
                  halo(lambda b, i: (b, jnp.minimum((nb - i) * hb, nh - 1), 0)),
                  state,
                  _full((LRU_CONV, d)), _full((1, d)), _full(wai.shape), _full(bai.shape), _full((2, d))],
        out_specs=(pl.BlockSpec((1, tm, d), fwd), pl.BlockSpec((1, tm, d), bwd), state),
        scratch_shapes=[pltpu.VMEM((tm + 2 * LRU_HALO, d), F32)]
        + [pltpu.VMEM((2, tm * LRU_HEADS, LANES), F32)] * 3
        + [pltpu.VMEM((2, LRU_HEADS, LANES), F32)],
        compiler_params=_cparams("parallel", "arbitrary"), name="lru_scan",
    )(xp, xp, xp, xp, xp, xp, h0, conv_w, conv_b.reshape(1, d), wai, bai, lam)


def _lru_out_kernel(rf_ref, rb_ref, gate_ref, x_ref, mod_ref, w_ref, b_ref, gffn_ref, wr_ref,
                    xo_ref, m_ref, aff_ref):
    r = (rf_ref[0] + rb_ref[0]) * gate_ref[0]
    y = jnp.dot(r.astype(BF16), w_ref[...], preferred_element_type=F32) + b_ref[...]
    _post(x_ref[0] + mod_ref[0, 2:3, :] * y, mod_ref, gffn_ref, wr_ref, xo_ref, m_ref, aff_ref)


def _lru_out(rf, rb, gate, x, mod, w_out, b_out, g_ffn, w_router):
    bsz, n, d = x.shape
    tm = _row_block(n)
    row = pl.BlockSpec((1, tm, d), lambda b, i: (b, i, 0))
    vec = _full((1, d))
    out_shape, out_specs = _post_specs(bsz, n, d, tm)
    return pl.pallas_call(
        _lru_out_kernel, out_shape=out_shape, grid=(bsz, n // tm),
        in_specs=[row, row, row, row, pl.BlockSpec((1, 6, d), lambda b, i: (b, 0, 0)),
                  _full((d, d)), vec, vec, _full((N_EXPERTS, d))],
        out_specs=out_specs, compiler_params=_cparams("parallel", "parallel"), name="lru_out",
    )(rf, rb, gate, x, mod, w_out.astype(BF16), b_out.reshape(1, d), g_ffn.reshape(1, d), w_router.T)


INF_BITS = 0x7F800000
SLOT_UNROLL = 8


def _tile_cumsum(mask, tri):
    return jnp.dot(mask.astype(BF16), tri, preferred_element_type=F32)


def _select_kernel(aff_ref, lpos_ref, off_ref, sel_ref, *, n, cap):
    nt = n // LANES
    bits = pltpu.bitcast(aff_ref[0], I32)
    capf = float(cap)

    def bisect(_, lohi):
        lo, hi = lohi
        mid = lo + ((hi - lo + 1) >> 1)
        cnt = jnp.sum(jnp.where(bits >= mid, 1.0, 0.0), axis=1, keepdims=True)
        ok = cnt >= capf
        return jnp.where(ok, mid, lo), jnp.where(ok, hi, mid - 1)

    e = bits.shape[0]
    thr, _ = lax.fori_loop(0, 31, bisect, (jnp.zeros((e, 1), I32), jnp.full((e, 1), INF_BITS, I32)))
    gt = jnp.where(bits > thr, 1.0, 0.0)
    eq = jnp.where(bits == thr, 1.0, 0.0)
    need = capf - jnp.sum(gt, axis=1, keepdims=True)

    r_i = lax.broadcasted_iota(I32, (LANES, LANES), 0)
    c_i = lax.broadcasted_iota(I32, (LANES, LANES), 1)
    tri = jnp.where(r_i <= c_i, 1.0, 0.0).astype(BF16)
    sup = jnp.where(r_i < c_i, 1.0, 0.0).astype(BF16)
    t_i = lax.broadcasted_iota(I32, (n, LANES), 0)
    k_i = lax.broadcasted_iota(I32, (n, LANES), 1)
    tile_of = jnp.where((t_i >> 7) == k_i, 1.0, 0.0).astype(BF16)

    def tile_offsets(mask):
        counts = jnp.dot(mask.astype(BF16), tile_of, preferred_element_type=F32)
        return jnp.dot(counts.astype(BF16), sup, preferred_element_type=F32)

    eq_off = tile_offsets(eq)
    for k in range(nt):
        sl = slice(k * LANES, (k + 1) * LANES)
        eqk = eq[:, sl]
        rank = eq_off[:, k:k + 1] + _tile_cumsum(eqk, tri) - eqk
        sel_ref[:, sl] = gt[:, sl] + eqk * jnp.where(rank < need, 1.0, 0.0)
    sel = sel_ref[...]
    off_ref[0] = tile_offsets(sel).astype(I32)
    for k in range(nt):
        sl = slice(k * LANES, (k + 1) * LANES)
        sk = sel[:, sl]
        lpos_ref[0, :, sl] = jnp.where(sk > 0.0, _tile_cumsum(sk, tri) - 1.0, -1.0).astype(I32)


def _select(aff_t, cap):
    bsz, e, n = aff_t.shape
    return pl.pallas_call(
        functools.partial(_select_kernel, n=n, cap=cap),
        out_shape=(jax.ShapeDtypeStruct((bsz, e, n), I32), jax.ShapeDtypeStruct((bsz, e, LANES), I32)),
        grid=(bsz,),
        in_specs=[pl.BlockSpec((1, e, n), lambda b: (b, 0, 0))],
        out_specs=(pl.BlockSpec((1, e, n), lambda b: (b, 0, 0)),
                   pl.BlockSpec((1, e, LANES), lambda b: (b, 0, 0))),
        scratch_shapes=[pltpu.VMEM((e, n), F32)],
        compiler_params=_cparams("parallel"), name="moe_select",
    )(aff_t)


def _slots_kernel(off_ref, lpos_ref, aff_ref, out_ref, scr_ref, *, nt, cap):
    b = pl.program_id(0)
    e = pl.program_id(1)
    base = (b * pl.num_programs(1) + e) * LANES
    j_i = lax.broadcasted_iota(I32, (LANES, LANES), 0)
    row = lax.broadcasted_iota(I32, (LANES, LANES), 0)
    lane = lax.broadcasted_iota(I32, (LANES, LANES), 1)

    def tile(k, carry):
        lp = lpos_ref[0, 0, k]
        av = aff_ref[0, 0, k]
        onehot = jnp.where(lp == j_i, 1.0, 0.0).astype(BF16)
        a1 = av.astype(BF16).astype(F32)
        r1 = av - a1
        a2 = r1.astype(BF16).astype(F32)
        a3 = r1 - a2
        q = jnp.where(row == 0, lane.astype(F32),
                      jnp.where(row == 1, a1, jnp.where(row == 2, a2, jnp.where(row == 3, a3, 0.0))))
        res = lax.dot_general(onehot, q.astype(BF16), (((1,), (1,)), ((), ())),
                              preferred_element_type=F32)
        res = res + jnp.where(lane == 0, jnp.asarray(k * LANES, F32), 0.0)
        scr_ref[pl.ds(off_ref[base + k], LANES), :] = res
        return carry

    lax.fori_loop(0, nt, tile, 0, unroll=min(SLOT_UNROLL, nt))
    out_ref[0, 0] = scr_ref[0:cap, :]


def _slots(lpos, tile_off, aff_t, cap):
    bsz, e, n = lpos.shape
    nt = n // LANES
    lp5 = lpos.reshape(bsz, e, nt, 1, LANES)
    af5 = aff_t.reshape(bsz, e, nt, 1, LANES)
    blk = pl.BlockSpec((1, 1, nt, 1, LANES), lambda b, j, off: (b, j, 0, 0, 0))
    return pl.pallas_call(
        functools.partial(_slots_kernel, nt=nt, cap=cap),
        out_shape=jax.ShapeDtypeStruct((bsz, e, cap, LANES), F32),
        grid_spec=pltpu.PrefetchScalarGridSpec(
            num_scalar_prefetch=1, grid=(bsz, e), in_specs=[blk, blk],
            out_specs=pl.BlockSpec((1, 1, cap, LANES), lambda b, j, off: (b, j, 0, 0)),
            scratch_shapes=[pltpu.VMEM((cap + LANES, LANES), F32)]),
        compiler_params=_cparams("parallel", "parallel"), name="moe_slots",
    )(tile_off.reshape(-1), lp5, af5)


ROW_UNROLL = 8


def _gather_kernel(idx_ref, m_ref, xs_ref, scr_ref, *, cap, nl):
    b = pl.program_id(0)
    e = pl.program_id(1)
    base = (b * pl.num_programs(1) + e) * cap

    def rows(ju, carry):
        j0 = ju * ROW_UNROLL
        for r in range(ROW_UNROLL):
            t = idx_ref[base + j0 + r]
            scr_ref[pl.ds(pl.multiple_of((j0 + r) * nl, nl), nl), :] = \
                m_ref[0, pl.ds(pl.multiple_of(t * nl, nl), nl), :]
        return carry

    lax.fori_loop(0, cap // ROW_UNROLL, rows, 0)
    xs_ref[0, 0] = _load_token_major(scr_ref, (), cap, nl * LANES).astype(BF16)


def _gather(idx_flat, m_tok, e, cap):
    bsz, rows_, _ = m_tok.shape
    nl = SUBLANES
    d = nl * LANES
    return pl.pallas_call(
        functools.partial(_gather_kernel, cap=cap, nl=nl),
        out_shape=jax.ShapeDtypeStruct((bsz, e, cap, d), BF16),
        grid_spec=pltpu.PrefetchScalarGridSpec(
            num_scalar_prefetch=1, grid=(bsz, e),
            in_specs=[pl.BlockSpec((1, rows_, LANES), lambda b, j, idx: (b, 0, 0),
                                   pipeline_mode=pl.Buffered(1))],
            out_specs=pl.BlockSpec((1, 1, cap, d), lambda b, j, idx: (b, j, 0, 0)),
            scratch_shapes=[pltpu.VMEM((cap * nl, LANES), F32)]),
        compiler_params=_cparams("parallel", "arbitrary"), name="moe_gather",
    )(idx_flat, m_tok)


FFN_CHUNK = 512


def _ffn_kernel(xs_ref, gs_ref, w1_ref, w3_ref, w2_ref, y_ref, w1b_ref, w3b_ref, w2b_ref):
    @pl.when(pl.program_id(1) == 0)
    def _():
        w1b_ref[...] = w1_ref[0, 0].astype(BF16)
        w3b_ref[...] = w3_ref[0, 0].astype(BF16)
        w2b_ref[...] = w2_ref[0, 0].astype(BF16)

    xs = xs_ref[0, 0]
    de = w1b_ref.shape[1]
    fc = min(FFN_CHUNK, de)
    acc = None
    for c in range(de // fc):
        sl = slice(c * fc, (c + 1) * fc)
        h1 = jnp.dot(xs, w1b_ref[:, sl], preferred_element_type=F32)
        h3 = jnp.dot(xs, w3b_ref[:, sl], preferred_element_type=F32)
        h = (_silu(h1) * h3).astype(BF16)
        part = jnp.dot(h, w2b_ref[sl, :], preferred_element_type=F32)
        acc = part if acc is None else acc + part
    gs = gs_ref[0, 0]
    gate = gs[:, 1:2] + gs[:, 2:3] + gs[:, 3:4]
    _store_token_major(y_ref, (0, 0), acc * gate)


def _ffn(xs, slots, experts, layer):
    w1, w3, w2 = experts
    bsz, e, cap, d = xs.shape
    de = w1.shape[3]
    nl = d // LANES
    return pl.pallas_call(
        _ffn_kernel, out_shape=jax.ShapeDtypeStruct((bsz, e, cap * nl, LANES), F32),
        grid=(e, bsz),
        in_specs=[pl.BlockSpec((1, 1, cap, d), lambda j, b: (b, j, 0, 0)),
                  pl.BlockSpec((1, 1, cap, LANES), lambda j, b: (b, j, 0, 0)),
                  pl.BlockSpec((1, 1, d, de), lambda j, b: (layer, j, 0, 0)),
                  pl.BlockSpec((1, 1, d, de), lambda j, b: (layer, j, 0, 0)),
                  pl.BlockSpec((1, 1, de, d), lambda j, b: (layer, j, 0, 0))],
        out_specs=pl.BlockSpec((1, 1, cap * nl, LANES), lambda j, b: (b, j, 0, 0)),
        scratch_shapes=[pltpu.VMEM((d, de), BF16), pltpu.VMEM((d, de), BF16), pltpu.VMEM((de, d), BF16)],
        compiler_params=_cparams("parallel", "arbitrary"), name="moe_ffn",
    )(xs, slots, w1, w3, w2)


COMBINE_UNROLL = 4


def _combine_kernel(idx_ref, y_ref, acc_ref, *, cap, nl):
    b = pl.program_id(0)
    e = pl.program_id(1)
    base = (b * pl.num_programs(1) + e) * cap

    @pl.when(e == 0)
    def _():
        acc_ref[...] = jnp.zeros(acc_ref.shape, F32)

    def rows(ju, carry):
        j0 = ju * COMBINE_UNROLL
        ts = [pl.multiple_of(idx_ref[base + j0 + r] * nl, nl) for r in range(COMBINE_UNROLL)]
        vals = [acc_ref[0, pl.ds(ts[r], nl), :] + y_ref[0, 0, pl.ds(pl.multiple_of((j0 + r) * nl, nl), nl), :]
                for r in range(COMBINE_UNROLL)]
        for r in range(COMBINE_UNROLL):
            acc_ref[0, pl.ds(ts[r], nl), :] = vals[r]
        return carry

    lax.fori_loop(0, cap // COMBINE_UNROLL, rows, 0)


def _combine(idx_flat, y_tok, n, cap):
    bsz, e, rows_, _ = y_tok.shape
    nl = rows_ // cap
    return pl.pallas_call(
        functools.partial(_combine_kernel, cap=cap, nl=nl),
        out_shape=jax.ShapeDtypeStruct((bsz, n * nl, LANES), F32),
        grid_spec=pltpu.PrefetchScalarGridSpec(
            num_scalar_prefetch=1, grid=(bsz, e),
            in_specs=[pl.BlockSpec((1, 1, rows_, LANES), lambda b, j, idx: (b, j, 0, 0))],
            out_specs=pl.BlockSpec((1, n * nl, LANES), lambda b, j, idx: (b, 0, 0),
                                   pipeline_mode=pl.Buffered(1))),
        compiler_params=_cparams("parallel", "arbitrary"), name="moe_combine",
    )(idx_flat, y_tok)


def _resid_kernel(*refs, final):
    if final:
        x_ref, y_ref, mod_ref, g_ref, o_ref = refs
    else:
        x_ref, y_ref, mod_ref, o_ref = refs
    tm, d = x_ref.shape[1], x_ref.shape[2]
    x = x_ref[0] + mod_ref[0, 5:6, :] * _load_token_major(y_ref, (0,), tm, d)
    if final:
        x = x * lax.rsqrt(jnp.mean(x * x, axis=-1, keepdims=True) + EPS) * g_ref[...]
    o_ref[0] = x


def _resid(x, y_tok, mod, final_g=None):
    bsz, n, d = x.shape
    tm = _row_block(n)
    nl = d // LANES
    row = pl.BlockSpec((1, tm, d), lambda b, i: (b, i, 0))
    in_specs = [row, pl.BlockSpec((1, tm * nl, LANES), lambda b, i: (b, i, 0)),
                pl.BlockSpec((1, 6, d), lambda b, i: (b, 0, 0))]
    args = [x, y_tok, mod]
    if final_g is not None:
        in_specs.append(_full((1, d)))
        args.append(final_g.reshape(1, d))
    return pl.pallas_call(
        functools.partial(_resid_kernel, final=final_g is not None),
        out_shape=jax.ShapeDtypeStruct((bsz, n, d), F32), grid=(bsz, n // tm),
        in_specs=in_specs, out_specs=row,
        compiler_params=_cparams("parallel", "parallel"), name="moe_resid",
    )(*args)


def _moe(x_new, m_tok, aff_t, mod, experts, layer, final_g=None):
    bsz, n, d = x_new.shape
    e = aff_t.shape[1]
    cap = EC_CAPACITY * n // e
    lpos, tile_off = _select(aff_t, cap)
    slots = _slots(lpos, tile_off, aff_t, cap)
    idx_flat = slots[..., 0].astype(I32).reshape(-1)
    xs = _gather(idx_flat, m_tok, e, cap)
    y_tok = _ffn(xs, slots, experts, layer)
    moe_tok = _combine(idx_flat, y_tok, n, cap)
    return _resid(x_new, moe_tok, mod, final_g)


def _pos_embed(n, d):
    rows = n // GRID_W
    quarter = d // 4
    omega = 1.0 / (POS_BASE ** (jnp.arange(quarter, dtype=F32) / quarter))
    r = jnp.arange(rows, dtype=F32)[:, None] * omega
    cc = jnp.arange(GRID_W, dtype=F32)[:, None] * omega
    row_emb = jnp.concatenate([jnp.sin(r), jnp.cos(r)], axis=-1)
    col_emb = jnp.concatenate([jnp.sin(cc), jnp.cos(cc)], axis=-1)
    emb = jnp.concatenate([
        jnp.broadcast_to(row_emb[:, None, :], (rows, GRID_W, d // 2)),
        jnp.broadcast_to(col_emb[None, :, :], (rows, GRID_W, d // 2))], axis=-1)
    return emb.reshape(rows * GRID_W, d)


def kernel(x, c, ctx, c_ctx, w_mod, b_mod, g_mix, g_ffn, conv_w_in, conv_b_in, conv_dw, conv_dw_b, conv_ln_g, conv_ln_b, conv_w_out, conv_b_out, s5_lam_re, s5_lam_im, s5_log_dt, s5_b_re, s5_b_im, s5_c_re, s5_c_im, s5_d, s5_w_glu, s5_b_glu, lru_w_y, lru_b_y, lru_w_x, lru_b_x, lru_conv_w, lru_conv_b, lru_w_a, lru_b_a, lru_w_i, lru_b_i, lru_lam, lru_w_out, lru_b_out, moe_router, moe_w1, moe_w3, moe_w2, final_g):
    bsz, n, d = x.shape
    nctx = ctx.shape[1]
    depth = w_mod.shape[0]
    n_mixers = 3
    reader_layers = [i for i in range(depth) if i % n_mixers != 0]
    last_reader = max(reader_layers) if reader_layers else -1

    rows = -(-(bsz + 1) // SUBLANES) * SUBLANES
    cond = jnp.zeros((rows, d), F32).at[:bsz].set(c).at[bsz].set(c_ctx)
    mods = _modulation(cond, w_mod, b_mod).reshape(depth, rows, 6, d)
    pe = _pos_embed(n, d)
    assert d == SUBLANES * LANES, "token-major MoE rows assume one (8, 128) tile per token"
    experts = (moe_w1, moe_w3, moe_w2)

    x_lat, x_ctx = x, ctx
    for i in range(depth):
        kind, j = i % n_mixers, i // n_mixers
        ctx_in = i <= last_reader
        ctx_out = i < last_reader
        mod_lat = mods[i, :bsz]
        mod_ctx = jnp.broadcast_to(mods[i, bsz][None], (bsz, 6, d))
        lat_pe = pe if i == 0 else None
        streams = [(x_lat, mod_lat, lat_pe)]
        if ctx_out:
            streams.append((x_ctx, mod_ctx, None))

        if kind == 0:
            post = []
            for xs_, mod_, pe_ in streams:
                u = _conv_in(xs_, pe_, mod_, g_mix[i], conv_w_in[j], conv_b_in[j])
                post.append(_conv_out(u, xs_, pe_, mod_, conv_dw[j], conv_dw_b[j], conv_ln_g[j],
                                      conv_ln_b[j], conv_w_out[j], conv_b_out[j], g_ffn[i], moe_router[i]))
        elif kind == 1:
            te, fo, tab = _s5_tables(s5_lam_re[j], s5_lam_im[j], s5_log_dt[j], s5_b_re[j], s5_b_im[j],
                                     s5_c_re[j], s5_c_im[j])
            uc = _s5_in(x_ctx, mod_ctx, g_mix[i])
            u_all = jnp.concatenate([uc, _s5_in(x_lat, mod_lat, g_mix[i]), uc], axis=2)
            y_lat, y_ctx = _s5_core(u_all, te, fo, tab, nctx // S5_CHUNK, n // S5_CHUNK)
            post = [_s5_out(y_lat, x_lat, mod_lat, g_mix[i], s5_d[j], s5_w_glu[j], s5_b_glu[j],
                            g_ffn[i], moe_router[i])]
            if ctx_out:
                post.append(_s5_out(y_ctx, x_ctx, mod_ctx, g_mix[i], s5_d[j], s5_w_glu[j],
                                    s5_b_glu[j], g_ffn[i], moe_router[i]))
        else:
            scan_args = (lru_conv_w[j], lru_conv_b[j], lru_w_a[j], lru_b_a[j], lru_w_i[j], lru_b_i[j], lru_lam[j])
            xp_ctx, gate_ctx = _lru_in(x_ctx, mod_ctx, g_mix[i], lru_w_x[j], lru_b_x[j], lru_w_y[j], lru_b_y[j])
            rf_c, rb_c, h_ctx = _lru_scan(xp_ctx, jnp.zeros((bsz, 2, LRU_HEADS, d // LRU_HEADS), F32), *scan_args)
            xp_lat, gate_lat = _lru_in(x_lat, mod_lat, g_mix[i], lru_w_x[j], lru_b_x[j], lru_w_y[j], lru_b_y[j])
            rf_l, rb_l, _ = _lru_scan(xp_lat, h_ctx, *scan_args)
            post = [_lru_out(rf_l, rb_l, gate_lat, x_lat, mod_lat, lru_w_out[j], lru_b_out[j],
                             g_ffn[i], moe_router[i])]
            if ctx_out:
                post.append(_lru_out(rf_c, rb_c, gate_ctx, x_ctx, mod_ctx, lru_w_out[j], lru_b_out[j],
                                     g_ffn[i], moe_router[i]))

        xn, m, aff = post[0]
        x_lat = _moe(xn, m, aff, mod_lat, experts, i, final_g if i == depth - 1 else None)
        if ctx_out:
            xn, m, aff = post[1]
            x_ctx = _moe(xn, m, aff, mod_ctx, experts, i)
    return x_lat
```

```python
import functools
import math

import jax
import jax.numpy as jnp
from jax import lax
from jax.experimental import pallas as pl
from jax.experimental.pallas import tpu as pltpu

F32 = jnp.float32
BF16 = jnp.bfloat16
I32 = jnp.int32
HIGHEST = lax.Precision.HIGHEST

EPS = 1e-6
N_EXPERTS = 16
EC_CAPACITY = 2
CONV_WIDTH = 31
LRU_CONV = 4
LRU_HEADS = 8
LRU_C = 8.0
S5_GROUP = 16
S5_STATE = 64
S5_CHUNK = 16
GRID_W = 64
POS_BASE = 10000.0

LANES = 128
SUBLANES = 8
ROW_BLOCK = 512
VMEM_LIMIT = 56 * 1024 * 1024


def _cparams(*sem):
    return pltpu.CompilerParams(dimension_semantics=sem, vmem_limit_bytes=VMEM_LIMIT)


def _full(shape):
    nd = len(shape)
    return pl.BlockSpec(shape, lambda *_: (0,) * nd)


def _rms_mod(x, g, scale, shift):
    y = x * lax.rsqrt(jnp.mean(x * x, axis=-1, keepdims=True) + EPS)
    return (y * g) * (1.0 + scale) + shift


def _silu(x):
    return x * jax.nn.sigmoid(x)


def _row_block(n):
    return min(ROW_BLOCK, n)


def _mod_kernel(c_ref, w_ref, b_ref, o_ref):
    s = _silu(c_ref[...])
    o_ref[0] = jnp.dot(s, w_ref[0], precision=HIGHEST, preferred_element_type=F32) + b_ref[0]


def _modulation(cond, w_mod, b_mod):
    depth, d, d6 = w_mod.shape
    rows = cond.shape[0]
    tn = 1536
    return pl.pallas_call(
        _mod_kernel,
        out_shape=jax.ShapeDtypeStruct((depth, rows, d6), F32),
        grid=(depth, d6 // tn),
        in_specs=[pl.BlockSpec((rows, d), lambda i, j: (0, 0)),
                  pl.BlockSpec((1, d, tn), lambda i, j: (i, 0, j)),
                  pl.BlockSpec((1, 1, tn), lambda i, j: (i, 0, j))],
        out_specs=pl.BlockSpec((1, rows, tn), lambda i, j: (i, 0, j)),
        compiler_params=_cparams("parallel", "parallel"),
        name="modulation",
    )(cond, w_mod, b_mod.reshape(depth, 1, d6))


def _store_token_major(ref, lead, val):
    rows, d = val.shape
    nl = d // LANES
    for c in range(nl):
        ref[lead + (pl.ds(c, rows, stride=nl), slice(None))] = val[:, c * LANES:(c + 1) * LANES]


def _load_token_major(ref, lead, rows, d):
    nl = d // LANES
    return jnp.concatenate([ref[lead + (pl.ds(c, rows, stride=nl), slice(None))] for c in range(nl)], axis=1)


def _is_token_major(shape):
    return shape[-1] == LANES


def _stream_dims(x, d):
    return (x.shape[0], x.shape[1] * LANES // d) if _is_token_major(x.shape) else x.shape[:2]


def _stream_spec(x, tm, d):
    if _is_token_major(x.shape):
        return pl.BlockSpec((1, tm * (d // LANES), LANES), lambda b, i: (b, i, 0))
    return pl.BlockSpec((1, tm, d), lambda b, i: (b, i, 0))


def _stream_rows(x_ref, tm, d):
    if _is_token_major(x_ref.shape):
        return _load_token_major(x_ref, (0,), tm, d)
    return x_ref[0]


def _post(xn, mod_ref, gffn_ref, wr_ref, xo_ref, m_ref, aff_ref):
    _store_token_major(xo_ref, (0,), xn)
    m = _rms_mod(xn, gffn_ref[...], mod_ref[0, 4:5, :], mod_ref[0, 3:4, :])
    _store_token_major(m_ref, (0,), m)
    logits = lax.dot_general(wr_ref[...], m, (((1,), (1,)), ((), ())),
                             precision=HIGHEST, preferred_element_type=F32)
    ex = jnp.exp(logits - jnp.max(logits, axis=0, keepdims=True))
    aff_ref[0] = ex / jnp.sum(ex, axis=0, keepdims=True)


def _post_specs(bsz, n, d, tm):
    nl = d // LANES
    out_shape = (jax.ShapeDtypeStruct((bsz, n * nl, LANES), F32),
                 jax.ShapeDtypeStruct((bsz, n * nl, LANES), F32),
                 jax.ShapeDtypeStruct((bsz, N_EXPERTS, n), F32))
    out_specs = (pl.BlockSpec((1, tm * nl, LANES), lambda b, i: (b, i, 0)),
                 pl.BlockSpec((1, tm * nl, LANES), lambda b, i: (b, i, 0)),
                 pl.BlockSpec((1, N_EXPERTS, tm), lambda b, i: (b, 0, i)))
    return out_shape, out_specs


def _conv_in_kernel(*refs, has_pe):
    if has_pe:
        x_ref, pe_ref, mod_ref, g_ref, w_ref, b_ref, u_ref = refs
        x = x_ref[0] + pe_ref[...]
    else:
        x_ref, mod_ref, g_ref, w_ref, b_ref, u_ref = refs
        x = _stream_rows(x_ref, u_ref.shape[1], u_ref.shape[2])
    h = _rms_mod(x, g_ref[...], mod_ref[0, 1:2, :], mod_ref[0, 0:1, :]).astype(BF16)
    z = jnp.dot(h, w_ref[...], preferred_element_type=F32) + b_ref[...]
    d = z.shape[1] // 2
    u_ref[0] = z[:, :d] * jax.nn.sigmoid(z[:, d:])


def _conv_in(x, pe, mod, g_mix, w_in, b_in):
    d = w_in.shape[0]
    bsz, n = _stream_dims(x, d)
    tm = _row_block(n)
    row = pl.BlockSpec((1, tm, d), lambda b, i: (b, i, 0))
    in_specs = [_stream_spec(x, tm, d)]
    args = [x]
    if pe is not None:
        in_specs.append(pl.BlockSpec((tm, d), lambda b, i: (i, 0)))
        args.append(pe)
    in_specs += [pl.BlockSpec((1, 6, d), lambda b, i: (b, 0, 0)), _full((1, d)),
                 _full((d, 2 * d)), _full((1, 2 * d))]
    args += [mod, g_mix.reshape(1, d), w_in.astype(BF16), b_in.reshape(1, 2 * d)]
    return pl.pallas_call(
        functools.partial(_conv_in_kernel, has_pe=pe is not None),
        out_shape=jax.ShapeDtypeStruct((bsz, n, d), F32),
        grid=(bsz, n // tm), in_specs=in_specs, out_specs=row,
        compiler_params=_cparams("parallel", "parallel"), name="conv_in",
    )(*args)


CONV_HALO = 16
CONV_ROWS = 128


def _conv_out_kernel(*refs, has_pe, tm):
    if has_pe:
        (u_ref, up_ref, un_ref, x_ref, pe_ref, mod_ref, dw_ref, dwb_ref, lng_ref, lnb_ref,
         w_ref, b_ref, gffn_ref, wr_ref, xo_ref, m_ref, aff_ref, ext_ref, cv_ref) = refs
        x = x_ref[0] + pe_ref[...]
    else:
        (u_ref, up_ref, un_ref, x_ref, mod_ref, dw_ref, dwb_ref, lng_ref, lnb_ref,
         w_ref, b_ref, gffn_ref, wr_ref, xo_ref, m_ref, aff_ref, ext_ref, cv_ref) = refs
        x = _stream_rows(x_ref, tm, u_ref.shape[2])
    i = pl.program_id(1)
    last = pl.num_programs(1) - 1
    ext_ref[0:CONV_HALO, :] = jnp.where(i > 0, up_ref[0], 0.0)
    ext_ref[CONV_HALO:CONV_HALO + tm, :] = u_ref[0]
    ext_ref[CONV_HALO + tm:2 * CONV_HALO + tm, :] = jnp.where(i < last, un_ref[0], 0.0)

    first_tap = CONV_HALO - CONV_WIDTH // 2

    rows = min(CONV_ROWS, tm)
    for lt in range(ext_ref.shape[1] // LANES):
        ls = slice(lt * LANES, (lt + 1) * LANES)
        for c in range(tm // rows):
            base = c * rows
            acc = jnp.zeros((rows, LANES), F32)
            for s in range(SUBLANES):
                part = None
                for o in range(first_tap, first_tap + CONV_WIDTH):
                    if o % SUBLANES != s:
                        continue
                    lo = base + o - s
                    term = dw_ref[o - first_tap:o - first_tap + 1, ls] * ext_ref[lo:lo + rows + SUBLANES, ls]
                    part = term if part is None else part + term
                acc = acc + part[s:s + rows]
            cv_ref[base:base + rows, ls] = acc
    cv = cv_ref[...] + dwb_ref[...]
    mu = jnp.mean(cv, axis=-1, keepdims=True)
    xc = cv - mu
    ln = xc * lax.rsqrt(jnp.mean(xc * xc, axis=-1, keepdims=True) + EPS) * lng_ref[...] + lnb_ref[...]
    y = jnp.dot(_silu(ln).astype(BF16), w_ref[...], preferred_element_type=F32) + b_ref[...]
    _post(x + mod_ref[0, 2:3, :] * y, mod_ref, gffn_ref, wr_ref, xo_ref, m_ref, aff_ref)


def _conv_out(u, x, pe, mod, dw, dw_b, ln_g, ln_b, w_out, b_out, g_ffn, w_router):
    bsz, n, d = u.shape
    tm = _row_block(n)
    hb = tm // CONV_HALO
    nh = n // CONV_HALO
    row = pl.BlockSpec((1, tm, d), lambda b, i: (b, i, 0))
    in_specs = [row,
                pl.BlockSpec((1, CONV_HALO, d), lambda b, i: (b, jnp.maximum(i * hb - 1, 0), 0)),
                pl.BlockSpec((1, CONV_HALO, d), lambda b, i: (b, jnp.minimum((i + 1) * hb, nh - 1), 0)),
                _stream_spec(x, tm, d)]
    args = [u, u, u, x]
    if pe is not None:
        in_specs.append(pl.BlockSpec((tm, d), lambda b, i: (i, 0)))
        args.append(pe)
    vec = _full((1, d))
    in_specs += [pl.BlockSpec((1, 6, d), lambda b, i: (b, 0, 0)), _full((CONV_WIDTH, d)), vec, vec, vec,
                 _full((d, d)), vec, vec, _full((N_EXPERTS, d))]
    args += [mod, dw, dw_b.reshape(1, d), ln_g.reshape(1, d), ln_b.reshape(1, d),
             w_out.astype(BF16), b_out.reshape(1, d), g_ffn.reshape(1, d), w_router.T]
    out_shape, out_specs = _post_specs(bsz, n, d, tm)
    return pl.pallas_call(
        functools.partial(_conv_out_kernel, has_pe=pe is not None, tm=tm),
        out_shape=out_shape, grid=(bsz, n // tm), in_specs=in_specs, out_specs=out_specs,
        scratch_shapes=[pltpu.VMEM((tm + 2 * CONV_HALO, d), F32), pltpu.VMEM((tm, d), F32)],
        compiler_params=_cparams("parallel", "parallel"), name="conv_out",
    )(*args)


GROUPS_PER_TILE = LANES // S5_GROUP
STEPS_PER_TILE = LANES // S5_GROUP


def _s5_in_kernel(x_ref, mod_ref, g_ref, ug_ref, slab_ref, *, tm):
    u = _rms_mod(_stream_rows(x_ref, tm, g_ref.shape[1]), g_ref[...], mod_ref[0, 1:2, :], mod_ref[0, 0:1, :])
    nl = u.shape[1] // LANES
    nc = tm // S5_CHUNK
    for lt in range(nl):
        slab_ref[lt] = u[:, lt * LANES:(lt + 1) * LANES]
    lane_step = lax.broadcasted_iota(I32, (nc, LANES), 1) // S5_GROUP
    for lt in range(nl):
        steps = [slab_ref[lt, pl.ds(t, nc, stride=S5_CHUNK), :] for t in range(S5_CHUNK)]
        for g8 in range(GROUPS_PER_TILE):
            for half in range(S5_CHUNK // STEPS_PER_TILE):
                acc = jnp.zeros((nc, LANES), F32)
                for tq in range(STEPS_PER_TILE):
                    shift = ((tq - g8) * S5_GROUP) % LANES
                    src = steps[half * STEPS_PER_TILE + tq]
                    acc = jnp.where(lane_step == tq, pltpu.roll(src, shift, 1) if shift else src, acc)
                ug_ref[0, lt * GROUPS_PER_TILE + g8, :, half * LANES:(half + 1) * LANES] = acc.astype(BF16)


def _s5_in(x, mod, g_mix):
    d = g_mix.shape[0]
    bsz, n = _stream_dims(x, d)
    tm = _row_block(n)
    groups = d // S5_GROUP
    lw = S5_CHUNK * S5_GROUP
    return pl.pallas_call(
        functools.partial(_s5_in_kernel, tm=tm),
        out_shape=jax.ShapeDtypeStruct((bsz, groups, n // S5_CHUNK, lw), BF16),
        grid=(bsz, n // tm),
        in_specs=[_stream_spec(x, tm, d),
                  pl.BlockSpec((1, 6, d), lambda b, i: (b, 0, 0)), _full((1, d))],
        out_specs=pl.BlockSpec((1, groups, tm // S5_CHUNK, lw), lambda b, i: (b, 0, i, 0)),
        scratch_shapes=[pltpu.VMEM((d // LANES, tm, LANES), F32)],
        compiler_params=_cparams("parallel", "parallel"), name="s5_in",
    )(x, mod, g_mix.reshape(1, d))


def _s5_tables(lam_re, lam_im, log_dt, b_re, b_im, c_re, c_im):
    L = S5_CHUNK
    dt = jnp.exp(log_dt)[:, :, None]
    mag = jnp.exp(lam_re * dt)
    ang = lam_im * dt
    lb_re = mag * jnp.cos(ang)
    lb_im = mag * jnp.sin(ang)
    nr = lb_re - 1.0
    ni = lb_im
    den = lam_re * lam_re + lam_im * lam_im
    coef_re = ((nr * lam_re + ni * lam_im) / den)[..., None]
    coef_im = ((ni * lam_re - nr * lam_im) / den)[..., None]
    bb_re = coef_re * b_re - coef_im * b_im
    bb_im = coef_re * b_im + coef_im * b_re

    def powers(steps):
        st = steps.astype(F32)
        m = jnp.exp((lam_re * dt)[..., None] * st)
        a = (lam_im * dt)[..., None] * st
        return m * jnp.cos(a), m * jnp.sin(a)

    pw_re, pw_im = powers(jnp.arange(L + 1))
    cp_re = c_re[:, :, None] * pw_re.transpose(0, 1, 3, 2)[:, :, :, None, :] \
        - c_im[:, :, None] * pw_im.transpose(0, 1, 3, 2)[:, :, :, None, :]
    cp_im = c_re[:, :, None] * pw_im.transpose(0, 1, 3, 2)[:, :, :, None, :] \
        + c_im[:, :, None] * pw_re.transpose(0, 1, 3, 2)[:, :, :, None, :]
    kern = (jnp.einsum("dgtkp,dgpj->dgtkj", cp_re[:, :, :L], bb_re, precision=HIGHEST)
            - jnp.einsum("dgtkp,dgpj->dgtkj", cp_im[:, :, :L], bb_im, precision=HIGHEST))
    lbb_re = pw_re.transpose(0, 1, 3, 2)[..., None] * bb_re[:, :, None] \
        - pw_im.transpose(0, 1, 3, 2)[..., None] * bb_im[:, :, None]
    lbb_im = pw_re.transpose(0, 1, 3, 2)[..., None] * bb_im[:, :, None] \
        + pw_im.transpose(0, 1, 3, 2)[..., None] * bb_re[:, :, None]

    s_idx = jnp.arange(L)[:, None]
    t_idx = jnp.arange(L)[None, :]
    G = lam_re.shape[1]
    K = S5_GROUP
    P = S5_STATE
    tes, fos = [], []
    for direction in range(2):
        if direction == 0:
            lag = t_idx - s_idx
            e_pow = (L - 1) - jnp.arange(L)
            f_pow = jnp.arange(L) + 1
        else:
            lag = s_idx - t_idx
            e_pow = jnp.arange(L)
            f_pow = L - jnp.arange(L)
        valid = lag >= 0
        kd = kern[direction][:, jnp.clip(lag, 0, L - 1)]
        kd = jnp.where(valid[None, :, :, None, None], kd, 0.0)
        tmat = kd.transpose(0, 1, 4, 2, 3).reshape(G, L * K, L * K)
        e_re = lbb_re[direction][:, e_pow]
        e_im = lbb_im[direction][:, e_pow]
        emat = jnp.concatenate([e_re.transpose(0, 1, 3, 2).reshape(G, L * K, P),
                                e_im.transpose(0, 1, 3, 2).reshape(G, L * K, P)], axis=-1)
        tes.append(jnp.concatenate([tmat, emat], axis=-1))
        f_re = cp_re[direction][:, f_pow]
        f_im = cp_im[direction][:, f_pow]
        fos.append(jnp.concatenate([f_re.transpose(0, 3, 1, 2).reshape(G, P, L * K),
                                    -f_im.transpose(0, 3, 1, 2).reshape(G, P, L * K)], axis=1))
    te = jnp.stack(tes).astype(BF16)
    fo = jnp.stack(fos).astype(BF16)
    a_re, a_im = powers(jnp.full((1,), L))
    a_re, a_im = a_re[..., 0], a_im[..., 0]
    tab = jnp.stack([jnp.concatenate([a_re, a_re], axis=-1),
                     jnp.concatenate([-a_im, a_im], axis=-1)], axis=1)
    return te, fo, tab


S5_GROUPS_PER_STEP = 8


S5_SCAN_UNROLL = 4


def _s5_core_kernel(u_ref, tef_ref, teb_ref, fof_ref, fob_ref, tab_ref, yl_ref, yc_ref,
                    yf_ref, yb_ref, hf_ref, hb_ref, sf_ref, sb_ref, *, nctx, nlat):
    gb = S5_GROUPS_PER_STEP
    rows = nctx + nlat
    lw = S5_CHUNK * S5_GROUP
    for g in range(gb):
        u = u_ref[0, g]
        zf = jnp.dot(u[0:rows], tef_ref[g], preferred_element_type=F32)
        yf_ref[g] = zf[:, :lw]
        hf_ref[pl.ds(g, rows, stride=gb), :] = zf[:, lw:]
        sf_ref[pl.ds(g, rows, stride=gb), :] = pltpu.roll(zf[:, lw:], S5_STATE, 1)
        zb = jnp.dot(u[nctx:nctx + rows], teb_ref[g], preferred_element_type=F32)
        yb_ref[g] = zb[:, :lw]
        hb_ref[pl.ds(g, rows, stride=gb), :] = zb[:, lw:]
        sb_ref[pl.ds(g, rows, stride=gb), :] = pltpu.roll(zb[:, lw:], S5_STATE, 1)

    cat_f, swp_f = tab_ref[0, 0], tab_ref[0, 1]
    cat_b, swp_b = tab_ref[1, 0], tab_ref[1, 1]

    def scan_rows(i, carry):
        hf, sf, hb, sb = carry
        rf = pl.multiple_of(i * gb, gb)
        rb = pl.multiple_of((rows - 1 - i) * gb, gb)
        ef, esf = hf_ref[pl.ds(rf, gb), :], sf_ref[pl.ds(rf, gb), :]
        eb, esb = hb_ref[pl.ds(rb, gb), :], sb_ref[pl.ds(rb, gb), :]
        hf_ref[pl.ds(rf, gb), :] = hf
        hb_ref[pl.ds(rb, gb), :] = hb
        return (cat_f * hf + swp_f * sf + ef, cat_f * sf - swp_f * hf + esf,
                cat_b * hb + swp_b * sb + eb, cat_b * sb - swp_b * hb + esb)

    zero = jnp.zeros((gb, 2 * S5_STATE), F32)
    lax.fori_loop(0, rows, scan_rows, (zero, zero, zero, zero), unroll=S5_SCAN_UNROLL)

    for g in range(gb):
        hin_f = hf_ref[pl.ds(g, rows, stride=gb), :].astype(BF16)
        hin_b = hb_ref[pl.ds(g, rows, stride=gb), :].astype(BF16)
        yf = yf_ref[g] + jnp.dot(hin_f, fof_ref[g], preferred_element_type=F32)
        yb = yb_ref[g] + jnp.dot(hin_b, fob_ref[g], preferred_element_type=F32)
        yl_ref[0, g] = yf[nctx:nctx + nlat] + yb[0:nlat]
        yc_ref[0, g] = yf[0:nctx] + yb[nlat:nlat + nctx]


def _s5_core(u_all, te, fo, tab, nctx, nlat):
    bsz, G, rtot, lw = u_all.shape
    gb = S5_GROUPS_PER_STEP
    rows = nctx + nlat
    w = 2 * S5_STATE
    gspec3 = lambda shape: pl.BlockSpec((gb,) + shape, lambda b, j: (j, 0, 0))
    return pl.pallas_call(
        functools.partial(_s5_core_kernel, nctx=nctx, nlat=nlat),
        out_shape=(jax.ShapeDtypeStruct((bsz, G, nlat, lw), F32),
                   jax.ShapeDtypeStruct((bsz, G, nctx, lw), F32)),
        grid=(bsz, G // gb),
        in_specs=[pl.BlockSpec((1, gb, rtot, lw), lambda b, j: (b, j, 0, 0)),
                  gspec3((lw, lw + w)), gspec3((lw, lw + w)), gspec3((w, lw)), gspec3((w, lw)),
                  pl.BlockSpec((2, 2, gb, w), lambda b, j: (0, 0, j, 0))],
        out_specs=(pl.BlockSpec((1, gb, nlat, lw), lambda b, j: (b, j, 0, 0)),
                   pl.BlockSpec((1, gb, nctx, lw), lambda b, j: (b, j, 0, 0))),
        scratch_shapes=[pltpu.VMEM((gb, rows, lw), F32), pltpu.VMEM((gb, rows, lw), F32)]
        + [pltpu.VMEM((rows * gb, w), F32)] * 4,
        compiler_params=_cparams("parallel", "parallel"), name="s5_core",
    )(u_all, te[0], te[1], fo[0], fo[1], tab)


def _s5_out_kernel(yg_ref, x_ref, mod_ref, gmix_ref, dsk_ref, w_ref, b_ref, gffn_ref, wr_ref,
                   xo_ref, m_ref, aff_ref, slab_ref, *, tm):
    nl = gmix_ref.shape[1] // LANES
    nc = tm // S5_CHUNK
    lane_group = lax.broadcasted_iota(I32, (nc, LANES), 1) // S5_GROUP
    for lt in range(nl):
        for t in range(S5_CHUNK):
            half, tq = divmod(t, STEPS_PER_TILE)
            acc = jnp.zeros((nc, LANES), F32)
            for g8 in range(GROUPS_PER_TILE):
                src = yg_ref[0, lt * GROUPS_PER_TILE + g8, :, half * LANES:(half + 1) * LANES]
                shift = ((g8 - tq) * S5_GROUP) % LANES
                acc = jnp.where(lane_group == g8, pltpu.roll(src, shift, 1) if shift else src, acc)
            slab_ref[lt, pl.ds(t, nc, stride=S5_CHUNK), :] = acc
    ssm = jnp.concatenate([slab_ref[lt] for lt in range(nl)], axis=1)
    x = _stream_rows(x_ref, tm, gmix_ref.shape[1])
    u = _rms_mod(x, gmix_ref[...], mod_ref[0, 1:2, :], mod_ref[0, 0:1, :])
    y = ssm + dsk_ref[...] * u
    z = jnp.dot(jax.nn.gelu(y).astype(BF16), w_ref[...], preferred_element_type=F32) + b_ref[...]
    d = z.shape[1] // 2
    out = z[:, :d] * jax.nn.sigmoid(z[:, d:])
    _post(x + mod_ref[0, 2:3, :] * out, mod_ref, gffn_ref, wr_ref, xo_ref, m_ref, aff_ref)


def _s5_out(yg, x, mod, g_mix, d_skip, w_glu, b_glu, g_ffn, w_router):
    d = g_mix.shape[0]
    bsz, n = _stream_dims(x, d)
    tm = _row_block(n)
    groups = d // S5_GROUP
    lw = S5_CHUNK * S5_GROUP
    row = pl.BlockSpec((1, tm, d), lambda b, i: (b, i, 0))
    vec = _full((1, d))
    out_shape, out_specs = _post_specs(bsz, n, d, tm)
    return pl.pallas_call(
        functools.partial(_s5_out_kernel, tm=tm), out_shape=out_shape, grid=(bsz, n // tm),
        in_specs=[pl.BlockSpec((1, groups, tm // S5_CHUNK, lw), lambda b, i: (b, 0, i, 0)),
                  _stream_spec(x, tm, d), pl.BlockSpec((1, 6, d), lambda b, i: (b, 0, 0)), vec, vec,
                  _full((d, 2 * d)), _full((1, 2 * d)), vec, _full((N_EXPERTS, d))],
        out_specs=out_specs,
        scratch_shapes=[pltpu.VMEM((d // LANES, tm, LANES), F32)],
        compiler_params=_cparams("parallel", "parallel"), name="s5_out",
    )(yg, x, mod, g_mix.reshape(1, d), d_skip.reshape(1, d), w_glu.astype(BF16), b_glu.reshape(1, 2 * d),
      g_ffn.reshape(1, d), w_router.T)


def _lru_in_kernel(x_ref, mod_ref, g_ref, wx_ref, bx_ref, wy_ref, by_ref, xp_ref, gate_ref):
    x = _stream_rows(x_ref, xp_ref.shape[1], xp_ref.shape[2])
    h = _rms_mod(x, g_ref[...], mod_ref[0, 1:2, :], mod_ref[0, 0:1, :]).astype(BF16)
    xp_ref[0] = jnp.dot(h, wx_ref[...], preferred_element_type=F32) + bx_ref[...]
    gate_ref[0] = jax.nn.gelu(jnp.dot(h, wy_ref[...], preferred_element_type=F32) + by_ref[...])


def _lru_in(x, mod, g_mix, w_x, b_x, w_y, b_y):
    d = g_mix.shape[0]
    bsz, n = _stream_dims(x, d)
    tm = _row_block(n)
    row = pl.BlockSpec((1, tm, d), lambda b, i: (b, i, 0))
    vec = _full((1, d))
    return pl.pallas_call(
        _lru_in_kernel,
        out_shape=(jax.ShapeDtypeStruct((bsz, n, d), F32), jax.ShapeDtypeStruct((bsz, n, d), F32)),
        grid=(bsz, n // tm),
        in_specs=[_stream_spec(x, tm, d), pl.BlockSpec((1, 6, d), lambda b, i: (b, 0, 0)), vec,
                  _full((d, d)), vec, _full((d, d)), vec],
        out_specs=(row, row), compiler_params=_cparams("parallel", "parallel"), name="lru_in",
    )(x, mod, g_mix.reshape(1, d), w_x.astype(BF16), b_x.reshape(1, d), w_y.astype(BF16), b_y.reshape(1, d))


LRU_HALO = 8
LRU_SCAN_UNROLL = 8


def _lru_scan_kernel(xf_ref, xfp_ref, xfn_ref, xb_ref, xbp_ref, xbn_ref, h0_ref, cw_ref, cb_ref,
                     wai_ref, bai_ref, lam_ref, rf_ref, rb_ref, hfin_ref,
                     ext_ref, a_ref, b_ref, r_ref, h_ref, *, tm):
    i = pl.program_id(1)
    last = pl.num_programs(1) - 1
    d = ext_ref.shape[1]
    hw = d // LRU_HEADS

    @pl.when(i == 0)
    def _():
        h_ref[...] = h0_ref[0]

    lam = lam_ref[...]
    nl = -lam
    softplus = jnp.maximum(nl, 0.0) + jnp.log1p(jnp.exp(-jnp.abs(nl)))
    c8 = -LRU_C * softplus

    def sigmoid(v):
        return 0.5 * jnp.tanh(0.5 * v) + 0.5

    def gates(dr, main_ref, prev_ref, next_ref, blk):
        ext_ref[0:LRU_HALO, :] = jnp.where(blk > 0, prev_ref[0], 0.0)
        ext_ref[LRU_HALO:LRU_HALO + tm, :] = main_ref[0]
        ext_ref[LRU_HALO + tm:2 * LRU_HALO + tm, :] = jnp.where(blk < last, next_ref[0], 0.0)
        xl = cb_ref[...]
        for k in range(LRU_CONV):
            off = LRU_HALO - LRU_CONV // 2 + k
            xl = xl + cw_ref[k:k + 1, :] * ext_ref[off:off + tm, :]
        xlb = xl.astype(BF16)
        for hd in range(LRU_HEADS):
            sl = slice(hd * hw, (hd + 1) * hw)
            z = jnp.dot(xlb[:, sl], wai_ref[dr, hd], preferred_element_type=F32)
            r = sigmoid(z[:, :hw] + bai_ref[dr, 0:1, sl])
            ig = sigmoid(z[:, hw:] + bai_ref[dr, 1:2, sl])
            log_a = c8[dr:dr + 1, sl] * r
            th = jnp.tanh(log_a)
            a_ref[dr, pl.ds(hd, tm, stride=LRU_HEADS), :] = jnp.exp(log_a)
            b_ref[dr, pl.ds(hd, tm, stride=LRU_HEADS), :] = jnp.sqrt(-2.0 * th / (1.0 - th)) * (ig * xl[:, sl])

    gates(0, xf_ref, xfp_ref, xfn_ref, i)
    gates(1, xb_ref, xbp_ref, xbn_ref, last - i)

    def step(t, carry):
        hf, hb = carry
        rf = pl.multiple_of(t * LRU_HEADS, LRU_HEADS)
        rb = pl.multiple_of((tm - 1 - t) * LRU_HEADS, LRU_HEADS)
        hf = a_ref[0, pl.ds(rf, LRU_HEADS), :] * hf + b_ref[0, pl.ds(rf, LRU_HEADS), :]
        hb = a_ref[1, pl.ds(rb, LRU_HEADS), :] * hb + b_ref[1, pl.ds(rb, LRU_HEADS), :]
        r_ref[0, pl.ds(rf, LRU_HEADS), :] = hf
        r_ref[1, pl.ds(rb, LRU_HEADS), :] = hb
        return hf, hb

    hf, hb = lax.fori_loop(0, tm, step, (h_ref[0], h_ref[1]), unroll=LRU_SCAN_UNROLL)
    h_ref[0] = hf
    h_ref[1] = hb
    hfin_ref[0] = h_ref[...]
    rf_ref[0] = _load_token_major(r_ref, (0,), tm, d)
    rb_ref[0] = _load_token_major(r_ref, (1,), tm, d)


def _lru_scan(xp, h0, conv_w, conv_b, w_a, b_a, w_i, b_i, lam):
    bsz, n, d = xp.shape
    tm = _row_block(n)
    nb = n // tm
    hb = tm // LRU_HALO
    nh = n // LRU_HALO
    fwd = lambda b, i: (b, i, 0)
    bwd = lambda b, i: (b, nb - 1 - i, 0)
    halo = lambda f: pl.BlockSpec((1, LRU_HALO, d), f)
    wai = jnp.concatenate([w_a, w_i], axis=-1).astype(BF16)
    bai = jnp.stack([b_a, b_i], axis=1)
    assert d // LRU_HEADS == LANES, "token-major scan assumes one 128-lane tile per head"
    state = pl.BlockSpec((1, 2, LRU_HEADS, LANES), lambda b, i: (b, 0, 0, 0))
    return pl.pallas_call(
        functools.partial(_lru_scan_kernel, tm=tm),
        out_shape=(jax.ShapeDtypeStruct((bsz, n, d), F32), jax.ShapeDtypeStruct((bsz, n, d), F32),
                   jax.ShapeDtypeStruct((bsz, 2, LRU_HEADS, LANES), F32)),
        grid=(bsz, nb),
        in_specs=[pl.BlockSpec((1, tm, d), fwd),
                  halo(lambda b, i: (b, jnp.maximum(i * hb - 1, 0), 0)),
                  halo(lambda b, i: (b, jnp.minimum((i + 1) * hb, nh - 1), 0)),
                  pl.BlockSpec((1, tm, d), bwd),
                  halo(lambda b, i: (b, jnp.maximum((nb - 1 - i) * hb - 1, 0), 0)),
                  halo(lambda b, i: (b, jnp.minimum((nb - i) * hb, nh - 1), 0)),
                  state,
                  _full((LRU_CONV, d)), _full((1, d)), _full(wai.shape), _full(bai.shape), _full((2, d))],
        out_specs=(pl.BlockSpec((1, tm, d), fwd), pl.BlockSpec((1, tm, d), bwd), state),
        scratch_shapes=[pltpu.VMEM((tm + 2 * LRU_HALO, d), F32)]
        + [pltpu.VMEM((2, tm * LRU_HEADS, LANES), F32)] * 3
        + [pltpu.VMEM((2, LRU_HEADS, LANES), F32)],
        compiler_params=_cparams("parallel", "arbitrary"), name="lru_scan",
    )(xp, xp, xp, xp, xp, xp, h0, conv_w, conv_b.reshape(1, d), wai, bai, lam)


def _lru_out_kernel(rf_ref, rb_ref, gate_ref, x_ref, mod_ref, w_ref, b_ref, gffn_ref, wr_ref,
                    xo_ref, m_ref, aff_ref):
    r = (rf_ref[0] + rb_ref[0]) * gate_ref[0]
    y = jnp.dot(r.astype(BF16), w_ref[...], preferred_element_type=F32) + b_ref[...]
    x = _stream_rows(x_ref, rf_ref.shape[1], rf_ref.shape[2])
    _post(x + mod_ref[0, 2:3, :] * y, mod_ref, gffn_ref, wr_ref, xo_ref, m_ref, aff_ref)


def _lru_out(rf, rb, gate, x, mod, w_out, b_out, g_ffn, w_router):
    bsz, n, d = rf.shape
    tm = _row_block(n)
    row = pl.BlockSpec((1, tm, d), lambda b, i: (b, i, 0))
    vec = _full((1, d))
    out_shape, out_specs = _post_specs(bsz, n, d, tm)
    return pl.pallas_call(
        _lru_out_kernel, out_shape=out_shape, grid=(bsz, n // tm),
        in_specs=[row, row, row, _stream_spec(x, tm, d), pl.BlockSpec((1, 6, d), lambda b, i: (b, 0, 0)),
                  _full((d, d)), vec, vec, _full((N_EXPERTS, d))],
        out_specs=out_specs, compiler_params=_cparams("parallel", "parallel"), name="lru_out",
    )(rf, rb, gate, x, mod, w_out.astype(BF16), b_out.reshape(1, d), g_ffn.reshape(1, d), w_router.T)


INF_BITS = 0x7F800000
SLOT_UNROLL = 8


def _tile_cumsum(mask, tri):
    return jnp.dot(mask.astype(BF16), tri, preferred_element_type=F32)


def _select_kernel(aff_ref, lpos_ref, off_ref, sel_ref, *, n, cap):
    nt = n // LANES
    bits = pltpu.bitcast(aff_ref[0], I32)
    capf = float(cap)

    def bisect(_, lohi):
        lo, hi = lohi
        mid = lo + ((hi - lo + 1) >> 1)
        cnt = jnp.sum(jnp.where(bits >= mid, 1.0, 0.0), axis=1, keepdims=True)
        ok = cnt >= capf
        return jnp.where(ok, mid, lo), jnp.where(ok, hi, mid - 1)

    e = bits.shape[0]
    thr, _ = lax.fori_loop(0, 31, bisect, (jnp.zeros((e, 1), I32), jnp.full((e, 1), INF_BITS, I32)))
    gt = jnp.where(bits > thr, 1.0, 0.0)
    eq = jnp.where(bits == thr, 1.0, 0.0)
    need = capf - jnp.sum(gt, axis=1, keepdims=True)

    r_i = lax.broadcasted_iota(I32, (LANES, LANES), 0)
    c_i = lax.broadcasted_iota(I32, (LANES, LANES), 1)
    tri = jnp.where(r_i <= c_i, 1.0, 0.0).astype(BF16)
    sup = jnp.where(r_i < c_i, 1.0, 0.0).astype(BF16)
    t_i = lax.broadcasted_iota(I32, (n, LANES), 0)
    k_i = lax.broadcasted_iota(I32, (n, LANES), 1)
    tile_of = jnp.where((t_i >> 7) == k_i, 1.0, 0.0).astype(BF16)

    def tile_offsets(mask):
        counts = jnp.dot(mask.astype(BF16), tile_of, preferred_element_type=F32)
        return jnp.dot(counts.astype(BF16), sup, preferred_element_type=F32)

    eq_off = tile_offsets(eq)
    for k in range(nt):
        sl = slice(k * LANES, (k + 1) * LANES)
        eqk = eq[:, sl]
        rank = eq_off[:, k:k + 1] + _tile_cumsum(eqk, tri) - eqk
        sel_ref[:, sl] = gt[:, sl] + eqk * jnp.where(rank < need, 1.0, 0.0)
    sel = sel_ref[...]
    off_ref[0] = tile_offsets(sel).astype(I32)
    for k in range(nt):
        sl = slice(k * LANES, (k + 1) * LANES)
        sk = sel[:, sl]
        lpos_ref[0, :, sl] = jnp.where(sk > 0.0, _tile_cumsum(sk, tri) - 1.0, -1.0).astype(I32)


def _select(aff_t, cap):
    bsz, e, n = aff_t.shape
    return pl.pallas_call(
        functools.partial(_select_kernel, n=n, cap=cap),
        out_shape=(jax.ShapeDtypeStruct((bsz, e, n), I32), jax.ShapeDtypeStruct((bsz, e, LANES), I32)),
        grid=(bsz,),
        in_specs=[pl.BlockSpec((1, e, n), lambda b: (b, 0, 0))],
        out_specs=(pl.BlockSpec((1, e, n), lambda b: (b, 0, 0)),
                   pl.BlockSpec((1, e, LANES), lambda b: (b, 0, 0))),
        scratch_shapes=[pltpu.VMEM((e, n), F32)],
        compiler_params=_cparams("parallel"), name="moe_select",
    )(aff_t)


def _slots_kernel(off_ref, lpos_ref, aff_ref, out_ref, scr_ref, *, nt, cap):
    b = pl.program_id(0)
    e = pl.program_id(1)
    base = (b * pl.num_programs(1) + e) * LANES
    j_i = lax.broadcasted_iota(I32, (LANES, LANES), 0)
    row = lax.broadcasted_iota(I32, (LANES, LANES), 0)
    lane = lax.broadcasted_iota(I32, (LANES, LANES), 1)

    def tile(k, carry):
        lp = lpos_ref[0, 0, k]
        av = aff_ref[0, 0, k]
        onehot = jnp.where(lp == j_i, 1.0, 0.0).astype(BF16)
        a1 = av.astype(BF16).astype(F32)
        r1 = av - a1
        a2 = r1.astype(BF16).astype(F32)
        a3 = r1 - a2
        q = jnp.where(row == 0, lane.astype(F32),
                      jnp.where(row == 1, a1, jnp.where(row == 2, a2, jnp.where(row == 3, a3, 0.0))))
        res = lax.dot_general(onehot, q.astype(BF16), (((1,), (1,)), ((), ())),
                              preferred_element_type=F32)
        res = res + jnp.where(lane == 0, jnp.asarray(k * LANES, F32), 0.0)
        scr_ref[pl.ds(off_ref[base + k], LANES), :] = res
        return carry

    lax.fori_loop(0, nt, tile, 0, unroll=min(SLOT_UNROLL, nt))
    out_ref[0, 0] = scr_ref[0:cap, :]


def _slots(lpos, tile_off, aff_t, cap):
    bsz, e, n = lpos.shape
    nt = n // LANES
    lp5 = lpos.reshape(bsz, e, nt, 1, LANES)
    af5 = aff_t.reshape(bsz, e, nt, 1, LANES)
    blk = pl.BlockSpec((1, 1, nt, 1, LANES), lambda b, j, off: (b, j, 0, 0, 0))
    return pl.pallas_call(
        functools.partial(_slots_kernel, nt=nt, cap=cap),
        out_shape=jax.ShapeDtypeStruct((bsz, e, cap, LANES), F32),
        grid_spec=pltpu.PrefetchScalarGridSpec(
            num_scalar_prefetch=1, grid=(bsz, e), in_specs=[blk, blk],
            out_specs=pl.BlockSpec((1, 1, cap, LANES), lambda b, j, off: (b, j, 0, 0)),
            scratch_shapes=[pltpu.VMEM((cap + LANES, LANES), F32)]),
        compiler_params=_cparams("parallel", "parallel"), name="moe_slots",
    )(tile_off.reshape(-1), lp5, af5)


ROW_UNROLL = 8


def _gather_kernel(idx_ref, m_ref, xs_ref, scr_ref, *, cap, nl):
    b = pl.program_id(0)
    e = pl.program_id(1)
    base = (b * pl.num_programs(1) + e) * cap

    def rows(ju, carry):
        j0 = ju * ROW_UNROLL
        for r in range(ROW_UNROLL):
            t = idx_ref[base + j0 + r]
            scr_ref[pl.ds(pl.multiple_of((j0 + r) * nl, nl), nl), :] = \
                m_ref[0, pl.ds(pl.multiple_of(t * nl, nl), nl), :]
        return carry

    lax.fori_loop(0, cap // ROW_UNROLL, rows, 0)
    xs_ref[0, 0] = _load_token_major(scr_ref, (), cap, nl * LANES).astype(BF16)


def _gather(idx_flat, m_tok, e, cap):
    bsz, rows_, _ = m_tok.shape
    nl = SUBLANES
    d = nl * LANES
    return pl.pallas_call(
        functools.partial(_gather_kernel, cap=cap, nl=nl),
        out_shape=jax.ShapeDtypeStruct((bsz, e, cap, d), BF16),
        grid_spec=pltpu.PrefetchScalarGridSpec(
            num_scalar_prefetch=1, grid=(bsz, e),
            in_specs=[pl.BlockSpec((1, rows_, LANES), lambda b, j, idx: (b, 0, 0),
                                   pipeline_mode=pl.Buffered(1))],
            out_specs=pl.BlockSpec((1, 1, cap, d), lambda b, j, idx: (b, j, 0, 0)),
            scratch_shapes=[pltpu.VMEM((cap * nl, LANES), F32)]),
        compiler_params=_cparams("parallel", "arbitrary"), name="moe_gather",
    )(idx_flat, m_tok)


FFN_CHUNK = 512


def _ffn_rows(xs_ref, gs_ref, y_ref, w1b_ref, w3b_ref, w2b_ref):
    xs = xs_ref[0, 0]
    de = w1b_ref.shape[1]
    fc = min(FFN_CHUNK, de)
    acc = None
    for c in range(de // fc):
        sl = slice(c * fc, (c + 1) * fc)
        h1 = jnp.dot(xs, w1b_ref[:, sl], preferred_element_type=F32)
        h3 = jnp.dot(xs, w3b_ref[:, sl], preferred_element_type=F32)
        h = (_silu(h1) * h3).astype(BF16)
        part = jnp.dot(h, w2b_ref[sl, :], preferred_element_type=F32)
        acc = part if acc is None else acc + part
    gs = gs_ref[0, 0]
    gate = gs[:, 1:2] + gs[:, 2:3] + gs[:, 3:4]
    _store_token_major(y_ref, (0, 0), acc * gate)


def _ffn_kernel(*refs, n_streams):
    ins = refs[:2 * n_streams]
    w1_ref, w3_ref, w2_ref = refs[2 * n_streams:2 * n_streams + 3]
    outs = refs[2 * n_streams + 3:3 * n_streams + 3]
    w1b_ref, w3b_ref, w2b_ref = refs[3 * n_streams + 3:]

    @pl.when(pl.program_id(1) == 0)
    def _():
        w1b_ref[...] = w1_ref[0, 0].astype(BF16)
        w3b_ref[...] = w3_ref[0, 0].astype(BF16)
        w2b_ref[...] = w2_ref[0, 0].astype(BF16)

    for k in range(n_streams):
        _ffn_rows(ins[2 * k], ins[2 * k + 1], outs[k], w1b_ref, w3b_ref, w2b_ref)


def _ffn(streams, experts, layer):
    w1, w3, w2 = experts
    bsz, e, _, d = streams[0][0].shape
    de = w1.shape[3]
    nl = d // LANES
    in_specs, args, out_shape, out_specs = [], [], [], []
    for xs, slots in streams:
        cap = xs.shape[2]
        in_specs += [pl.BlockSpec((1, 1, cap, d), lambda j, b: (b, j, 0, 0)),
                     pl.BlockSpec((1, 1, cap, LANES), lambda j, b: (b, j, 0, 0))]
        args += [xs, slots]
        out_shape.append(jax.ShapeDtypeStruct((bsz, e, cap * nl, LANES), F32))
        out_specs.append(pl.BlockSpec((1, 1, cap * nl, LANES), lambda j, b: (b, j, 0, 0)))
    in_specs += [pl.BlockSpec((1, 1, d, de), lambda j, b: (layer, j, 0, 0)),
                 pl.BlockSpec((1, 1, d, de), lambda j, b: (layer, j, 0, 0)),
                 pl.BlockSpec((1, 1, de, d), lambda j, b: (layer, j, 0, 0))]
    return pl.pallas_call(
        functools.partial(_ffn_kernel, n_streams=len(streams)), out_shape=tuple(out_shape),
        grid=(e, bsz), in_specs=in_specs, out_specs=tuple(out_specs),
        scratch_shapes=[pltpu.VMEM((d, de), BF16), pltpu.VMEM((d, de), BF16), pltpu.VMEM((de, d), BF16)],
        compiler_params=_cparams("parallel", "arbitrary"), name="moe_ffn",
    )(*args, w1, w3, w2)


COMBINE_UNROLL = 4


def _combine_kernel(idx_ref, y_ref, g2_ref, xin_hbm, acc_ref, sem, *, cap, nl):
    b = pl.program_id(0)
    e = pl.program_id(1)
    base = (b * pl.num_programs(1) + e) * cap

    @pl.when(e == 0)
    def _():
        load = pltpu.make_async_copy(xin_hbm.at[b], acc_ref.at[0], sem)
        load.start()
        load.wait()

    g2 = g2_ref[0]

    def rows(ju, carry):
        j0 = ju * COMBINE_UNROLL
        ts = [pl.multiple_of(idx_ref[base + j0 + r] * nl, nl) for r in range(COMBINE_UNROLL)]
        vals = [acc_ref[0, pl.ds(ts[r], nl), :]
                + g2 * y_ref[0, 0, pl.ds(pl.multiple_of((j0 + r) * nl, nl), nl), :]
                for r in range(COMBINE_UNROLL)]
        for r in range(COMBINE_UNROLL):
            acc_ref[0, pl.ds(ts[r], nl), :] = vals[r]
        return carry

    lax.fori_loop(0, cap // COMBINE_UNROLL, rows, 0)


def _combine(idx_flat, y_tok, x_tok, gate2, cap):
    bsz, e, rows_, _ = y_tok.shape
    nl = rows_ // cap
    return pl.pallas_call(
        functools.partial(_combine_kernel, cap=cap, nl=nl),
        out_shape=jax.ShapeDtypeStruct(x_tok.shape, F32),
        grid_spec=pltpu.PrefetchScalarGridSpec(
            num_scalar_prefetch=1, grid=(bsz, e),
            in_specs=[pl.BlockSpec((1, 1, rows_, LANES), lambda b, j, idx: (b, j, 0, 0)),
                      pl.BlockSpec((1, nl, LANES), lambda b, j, idx: (b, 0, 0)),
                      pl.BlockSpec(memory_space=pl.ANY)],
            out_specs=pl.BlockSpec((1,) + x_tok.shape[1:], lambda b, j, idx: (b, 0, 0),
                                   pipeline_mode=pl.Buffered(1)),
            scratch_shapes=[pltpu.SemaphoreType.DMA(())]),
        compiler_params=_cparams("parallel", "arbitrary"), name="moe_combine",
    )(idx_flat, y_tok, gate2.reshape(bsz, nl, LANES), x_tok)


def _final_norm_kernel(x_ref, g_ref, o_ref):
    x = _stream_rows(x_ref, o_ref.shape[1], o_ref.shape[2])
    o_ref[0] = x * lax.rsqrt(jnp.mean(x * x, axis=-1, keepdims=True) + EPS) * g_ref[...]


def _final_norm(x, final_g):
    d = final_g.shape[0]
    bsz, n = _stream_dims(x, d)
    tm = _row_block(n)
    return pl.pallas_call(
        _final_norm_kernel, out_shape=jax.ShapeDtypeStruct((bsz, n, d), F32), grid=(bsz, n // tm),
        in_specs=[_stream_spec(x, tm, d), _full((1, d))],
        out_specs=pl.BlockSpec((1, tm, d), lambda b, i: (b, i, 0)),
        compiler_params=_cparams("parallel", "parallel"), name="final_norm",
    )(x, final_g.reshape(1, d))


def _moe(streams, experts, layer):
    routed = []
    for x_tok, m_tok, aff_t, mod in streams:
        e, n = aff_t.shape[1], aff_t.shape[2]
        cap = EC_CAPACITY * n // e
        lpos, tile_off = _select(aff_t, cap)
        slots = _slots(lpos, tile_off, aff_t, cap)
        idx_flat = slots[..., 0].astype(I32).reshape(-1)
        routed.append((idx_flat, _gather(idx_flat, m_tok, e, cap), slots, cap))
    ys = _ffn([(xs, slots) for _, xs, slots, _ in routed], experts, layer)
    return [_combine(idx_flat, y_tok, x_tok, mod[:, 5], cap)
            for (idx_flat, _, _, cap), y_tok, (x_tok, _, _, mod) in zip(routed, ys, streams)]


def _pos_embed(n, d):
    rows = n // GRID_W
    quarter = d // 4
    omega = 1.0 / (POS_BASE ** (jnp.arange(quarter, dtype=F32) / quarter))
    r = jnp.arange(rows, dtype=F32)[:, None] * omega
    cc = jnp.arange(GRID_W, dtype=F32)[:, None] * omega
    row_emb = jnp.concatenate([jnp.sin(r), jnp.cos(r)], axis=-1)
    col_emb = jnp.concatenate([jnp.sin(cc), jnp.cos(cc)], axis=-1)
    emb = jnp.concatenate([
        jnp.broadcast_to(row_emb[:, None, :], (rows, GRID_W, d // 2)),
        jnp.broadcast_to(col_emb[None, :, :], (rows, GRID_W, d // 2))], axis=-1)
    return emb.reshape(rows * GRID_W, d)


def kernel(x, c, ctx, c_ctx, w_mod, b_mod, g_mix, g_ffn, conv_w_in, conv_b_in, conv_dw, conv_dw_b, conv_ln_g, conv_ln_b, conv_w_out, conv_b_out, s5_lam_re, s5_lam_im, s5_log_dt, s5_b_re, s5_b_im, s5_c_re, s5_c_im, s5_d, s5_w_glu, s5_b_glu, lru_w_y, lru_b_y, lru_w_x, lru_b_x, lru_conv_w, lru_conv_b, lru_w_a, lru_b_a, lru_w_i, lru_b_i, lru_lam, lru_w_out, lru_b_out, moe_router, moe_w1, moe_w3, moe_w2, final_g):
    bsz, n, d = x.shape
    nctx = ctx.shape[1]
    depth = w_mod.shape[0]
    n_mixers = 3
    reader_layers = [i for i in range(depth) if i % n_mixers != 0]
    last_reader = max(reader_layers) if reader_layers else -1

    rows = -(-(bsz + 1) // SUBLANES) * SUBLANES
    cond = jnp.zeros((rows, d), F32).at[:bsz].set(c).at[bsz].set(c_ctx)
    mods = _modulation(cond, w_mod, b_mod).reshape(depth, rows, 6, d)
    pe = _pos_embed(n, d)
    assert d == SUBLANES * LANES, "token-major MoE rows assume one (8, 128) tile per token"
    experts = (moe_w1, moe_w3, moe_w2)

    x_lat, x_ctx = x, ctx
    for i in range(depth):
        kind, j = i % n_mixers, i // n_mixers
        ctx_in = i <= last_reader
        ctx_out = i < last_reader
        mod_lat = mods[i, :bsz]
        mod_ctx = jnp.broadcast_to(mods[i, bsz][None], (bsz, 6, d))
        lat_pe = pe if i == 0 else None
        streams = [(x_lat, mod_lat, lat_pe)]
        if ctx_out:
            streams.append((x_ctx, mod_ctx, None))

        if kind == 0:
            post = []
            for xs_, mod_, pe_ in streams:
                u = _conv_in(xs_, pe_, mod_, g_mix[i], conv_w_in[j], conv_b_in[j])
                post.append(_conv_out(u, xs_, pe_, mod_, conv_dw[j], conv_dw_b[j], conv_ln_g[j],
                                      conv_ln_b[j], conv_w_out[j], conv_b_out[j], g_ffn[i], moe_router[i]))
        elif kind == 1:
            te, fo, tab = _s5_tables(s5_lam_re[j], s5_lam_im[j], s5_log_dt[j], s5_b_re[j], s5_b_im[j],
                                     s5_c_re[j], s5_c_im[j])
            uc = _s5_in(x_ctx, mod_ctx, g_mix[i])
            u_all = jnp.concatenate([uc, _s5_in(x_lat, mod_lat, g_mix[i]), uc], axis=2)
            y_lat, y_ctx = _s5_core(u_all, te, fo, tab, nctx // S5_CHUNK, n // S5_CHUNK)
            post = [_s5_out(y_lat, x_lat, mod_lat, g_mix[i], s5_d[j], s5_w_glu[j], s5_b_glu[j],
                            g_ffn[i], moe_router[i])]
            if ctx_out:
                post.append(_s5_out(y_ctx, x_ctx, mod_ctx, g_mix[i], s5_d[j], s5_w_glu[j],
                                    s5_b_glu[j], g_ffn[i], moe_router[i]))
        else:
            scan_args = (lru_conv_w[j], lru_conv_b[j], lru_w_a[j], lru_b_a[j], lru_w_i[j], lru_b_i[j], lru_lam[j])
            xp_ctx, gate_ctx = _lru_in(x_ctx, mod_ctx, g_mix[i], lru_w_x[j], lru_b_x[j], lru_w_y[j], lru_b_y[j])
            rf_c, rb_c, h_ctx = _lru_scan(xp_ctx, jnp.zeros((bsz, 2, LRU_HEADS, d // LRU_HEADS), F32), *scan_args)
            xp_lat, gate_lat = _lru_in(x_lat, mod_lat, g_mix[i], lru_w_x[j], lru_b_x[j], lru_w_y[j], lru_b_y[j])
            rf_l, rb_l, _ = _lru_scan(xp_lat, h_ctx, *scan_args)
            post = [_lru_out(rf_l, rb_l, gate_lat, x_lat, mod_lat, lru_w_out[j], lru_b_out[j],
                             g_ffn[i], moe_router[i])]
            if ctx_out:
                post.append(_lru_out(rf_c, rb_c, gate_ctx, x_ctx, mod_ctx, lru_w_out[j], lru_b_out[j],
                                     g_ffn[i], moe_router[i]))

        mods_ = [mod_lat, mod_ctx]
        outs = _moe([(xn, m, aff, mods_[k]) for k, (xn, m, aff) in enumerate(post)], experts, i)
        x_lat = outs[0]
        if ctx_out:
            x_ctx = outs[1]
    return _final_norm(x_lat, final_g)
```

```python
import functools
import math

import jax
import jax.numpy as jnp
from jax import lax
from jax.experimental import pallas as pl
from jax.experimental.pallas import tpu as pltpu

F32 = jnp.float32
BF16 = jnp.bfloat16
I32 = jnp.int32
HIGHEST = lax.Precision.HIGHEST

EPS = 1e-6
N_EXPERTS = 16
EC_CAPACITY = 2
CONV_WIDTH = 31
LRU_CONV = 4
LRU_HEADS = 8
LRU_C = 8.0
S5_GROUP = 16
S5_STATE = 64
S5_CHUNK = 16
GRID_W = 64
POS_BASE = 10000.0

LANES = 128
SUBLANES = 8
ROW_BLOCK = 512
VMEM_LIMIT = 56 * 1024 * 1024


def _cparams(*sem):
    return pltpu.CompilerParams(dimension_semantics=sem, vmem_limit_bytes=VMEM_LIMIT)


def _full(shape):
    nd = len(shape)
    return pl.BlockSpec(shape, lambda *_: (0,) * nd)


def _rms_mod(x, g, scale, shift):
    y = x * lax.rsqrt(jnp.mean(x * x, axis=-1, keepdims=True) + EPS)
    return (y * g) * (1.0 + scale) + shift


def _silu(x):
    return x * jax.nn.sigmoid(x)


def _row_block(n):
    return min(ROW_BLOCK, n)


def _mod_kernel(c_ref, w_ref, b_ref, o_ref):
    s = _silu(c_ref[...])
    o_ref[0] = jnp.dot(s, w_ref[0], precision=HIGHEST, preferred_element_type=F32) + b_ref[0]


def _modulation(cond, w_mod, b_mod):
    depth, d, d6 = w_mod.shape
    rows = cond.shape[0]
    tn = 1536
    return pl.pallas_call(
        _mod_kernel,
        out_shape=jax.ShapeDtypeStruct((depth, rows, d6), F32),
        grid=(depth, d6 // tn),
        in_specs=[pl.BlockSpec((rows, d), lambda i, j: (0, 0)),
                  pl.BlockSpec((1, d, tn), lambda i, j: (i, 0, j)),
                  pl.BlockSpec((1, 1, tn), lambda i, j: (i, 0, j))],
        out_specs=pl.BlockSpec((1, rows, tn), lambda i, j: (i, 0, j)),
        compiler_params=_cparams("parallel", "parallel"),
        name="modulation",
    )(cond, w_mod, b_mod.reshape(depth, 1, d6))


def _store_token_major(ref, lead, val, row0=0):
    rows, d = val.shape
    nl = d // LANES
    for c in range(nl):
        ref[lead + (pl.ds(row0 * nl + c, rows, stride=nl), slice(None))] = val[:, c * LANES:(c + 1) * LANES]


def _load_token_major(ref, lead, rows, d):
    nl = d // LANES
    return jnp.concatenate([ref[lead + (pl.ds(c, rows, stride=nl), slice(None))] for c in range(nl)], axis=1)


def _is_token_major(shape):
    return shape[-1] == LANES


def _stream_dims(x, d):
    return (x.shape[0], x.shape[1] * LANES // d) if _is_token_major(x.shape) else x.shape[:2]


def _stream_spec(x, tm, d):
    if _is_token_major(x.shape):
        return pl.BlockSpec((1, tm * (d // LANES), LANES), lambda b, i: (b, i, 0))
    return pl.BlockSpec((1, tm, d), lambda b, i: (b, i, 0))


def _stream_rows(x_ref, tm, d):
    if _is_token_major(x_ref.shape):
        return _load_token_major(x_ref, (0,), tm, d)
    return x_ref[0]


def _post(xn, mod_ref, gffn_ref, wr_ref, xo_ref, m_ref, aff_ref):
    _store_token_major(xo_ref, (0,), xn)
    m = _rms_mod(xn, gffn_ref[...], mod_ref[0, 4:5, :], mod_ref[0, 3:4, :])
    _store_token_major(m_ref, (0,), m)
    logits = lax.dot_general(wr_ref[...], m, (((1,), (1,)), ((), ())),
                             precision=HIGHEST, preferred_element_type=F32)
    ex = jnp.exp(logits - jnp.max(logits, axis=0, keepdims=True))
    aff_ref[0] = ex / jnp.sum(ex, axis=0, keepdims=True)


def _post_specs(bsz, n, d, tm):
    nl = d // LANES
    out_shape = (jax.ShapeDtypeStruct((bsz, n * nl, LANES), F32),
                 jax.ShapeDtypeStruct((bsz, n * nl, LANES), F32),
                 jax.ShapeDtypeStruct((bsz, N_EXPERTS, n), F32))
    out_specs = (pl.BlockSpec((1, tm * nl, LANES), lambda b, i: (b, i, 0)),
                 pl.BlockSpec((1, tm * nl, LANES), lambda b, i: (b, i, 0)),
                 pl.BlockSpec((1, N_EXPERTS, tm), lambda b, i: (b, 0, i)))
    return out_shape, out_specs


CONV_HALO = 16
CONV_ROWS = 128


def _conv_mixer_kernel(*refs, has_pe, tm):
    if has_pe:
        (x_ref, xp_ref, xn_ref, pe_ref, pep_ref, pen_ref, mod_ref, g_ref, win_ref, bin_ref, dw_ref, dwb_ref,
         lng_ref, lnb_ref, w_ref, b_ref, gffn_ref, wr_ref, xo_ref, m_ref, aff_ref, ext_ref, cv_ref) = refs
    else:
        (x_ref, xp_ref, xn_ref, mod_ref, g_ref, win_ref, bin_ref, dw_ref, dwb_ref,
         lng_ref, lnb_ref, w_ref, b_ref, gffn_ref, wr_ref, xo_ref, m_ref, aff_ref, ext_ref, cv_ref) = refs
    d = win_ref.shape[0]
    i = pl.program_id(1)
    last = pl.num_programs(1) - 1
    x = _stream_rows(x_ref, tm, d)
    x_prev = _stream_rows(xp_ref, CONV_HALO, d)
    x_next = _stream_rows(xn_ref, CONV_HALO, d)
    if has_pe:
        x = x + pe_ref[...]
        x_prev = x_prev + pep_ref[...]
        x_next = x_next + pen_ref[...]

    xe = jnp.concatenate([x_prev, x, x_next], axis=0)
    h = _rms_mod(xe, g_ref[...], mod_ref[0, 1:2, :], mod_ref[0, 0:1, :]).astype(BF16)
    z = jnp.dot(h, win_ref[...], preferred_element_type=F32) + bin_ref[...]
    u = z[:, :d] * jax.nn.sigmoid(z[:, d:])
    rid = lax.broadcasted_iota(I32, (tm + 2 * CONV_HALO, 1), 0)
    lo = jnp.where(i > 0, 0, CONV_HALO)
    hi = jnp.where(i < last, tm + 2 * CONV_HALO, tm + CONV_HALO)
    ext_ref[...] = jnp.where((rid >= lo) & (rid < hi), u, 0.0)

    first_tap = CONV_HALO - CONV_WIDTH // 2

    rows = min(CONV_ROWS, tm)
    for lt in range(d // LANES):
        ls = slice(lt * LANES, (lt + 1) * LANES)
        for c in range(tm // rows):
            base = c * rows
            acc = jnp.zeros((rows, LANES), F32)
            for s in range(SUBLANES):
                part = None
                for o in range(first_tap, first_tap + CONV_WIDTH):
                    if o % SUBLANES != s:
                        continue
                    lo_row = base + o - s
                    term = dw_ref[o - first_tap:o - first_tap + 1, ls] * ext_ref[lo_row:lo_row + rows + SUBLANES, ls]
                    part = term if part is None else part + term
                acc = acc + part[s:s + rows]
            cv_ref[base:base + rows, ls] = acc
    cv = cv_ref[...] + dwb_ref[...]
    mu = jnp.mean(cv, axis=-1, keepdims=True)
    xc = cv - mu
    ln = xc * lax.rsqrt(jnp.mean(xc * xc, axis=-1, keepdims=True) + EPS) * lng_ref[...] + lnb_ref[...]
    y = jnp.dot(_silu(ln).astype(BF16), w_ref[...], preferred_element_type=F32) + b_ref[...]
    _post(x + mod_ref[0, 2:3, :] * y, mod_ref, gffn_ref, wr_ref, xo_ref, m_ref, aff_ref)


def _conv_mixer(x, pe, mod, g_mix, w_in, b_in, dw, dw_b, ln_g, ln_b, w_out, b_out, g_ffn, w_router):
    d = w_in.shape[0]
    bsz, n = _stream_dims(x, d)
    tm = _row_block(n)
    hb = tm // CONV_HALO
    nh = n // CONV_HALO
    prev_blk = lambda i: jnp.maximum(i * hb - 1, 0)
    next_blk = lambda i: jnp.minimum((i + 1) * hb, nh - 1)
    if _is_token_major(x.shape):
        halo = lambda f: pl.BlockSpec((1, CONV_HALO * (d // LANES), LANES), lambda b, i: (b, f(i), 0))
    else:
        halo = lambda f: pl.BlockSpec((1, CONV_HALO, d), lambda b, i: (b, f(i), 0))
    in_specs = [_stream_spec(x, tm, d), halo(prev_blk), halo(next_blk)]
    args = [x, x, x]
    if pe is not None:
        in_specs += [pl.BlockSpec((tm, d), lambda b, i: (i, 0)),
                     pl.BlockSpec((CONV_HALO, d), lambda b, i: (prev_blk(i), 0)),
                     pl.BlockSpec((CONV_HALO, d), lambda b, i: (next_blk(i), 0))]
        args += [pe, pe, pe]
    vec = _full((1, d))
    in_specs += [pl.BlockSpec((1, 6, d), lambda b, i: (b, 0, 0)), vec, _full((d, 2 * d)), _full((1, 2 * d)),
                 _full((CONV_WIDTH, d)), vec, vec, vec, _full((d, d)), vec, vec, _full((N_EXPERTS, d))]
    args += [mod, g_mix.reshape(1, d), w_in.astype(BF16), b_in.reshape(1, 2 * d),
             dw, dw_b.reshape(1, d), ln_g.reshape(1, d), ln_b.reshape(1, d),
             w_out.astype(BF16), b_out.reshape(1, d), g_ffn.reshape(1, d), w_router.T]
    out_shape, out_specs = _post_specs(bsz, n, d, tm)
    return pl.pallas_call(
        functools.partial(_conv_mixer_kernel, has_pe=pe is not None, tm=tm),
        out_shape=out_shape, grid=(bsz, n // tm), in_specs=in_specs, out_specs=out_specs,
        scratch_shapes=[pltpu.VMEM((tm + 2 * CONV_HALO, d), F32), pltpu.VMEM((tm, d), F32)],
        compiler_params=_cparams("parallel", "parallel"), name="conv_mixer",
    )(*args)


GROUPS_PER_TILE = LANES // S5_GROUP
STEPS_PER_TILE = LANES // S5_GROUP


def _s5_in_kernel(x_ref, mod_ref, g_ref, ug_ref, slab_ref, *, tm):
    u = _rms_mod(_stream_rows(x_ref, tm, g_ref.shape[1]), g_ref[...], mod_ref[0, 1:2, :], mod_ref[0, 0:1, :])
    nl = u.shape[1] // LANES
    nc = tm // S5_CHUNK
    for lt in range(nl):
        slab_ref[lt] = u[:, lt * LANES:(lt + 1) * LANES]
    lane_step = lax.broadcasted_iota(I32, (nc, LANES), 1) // S5_GROUP
    for lt in range(nl):
        steps = [slab_ref[lt, pl.ds(t, nc, stride=S5_CHUNK), :] for t in range(S5_CHUNK)]
        for g8 in range(GROUPS_PER_TILE):
            for half in range(S5_CHUNK // STEPS_PER_TILE):
                acc = jnp.zeros((nc, LANES), F32)
                for tq in range(STEPS_PER_TILE):
                    shift = ((tq - g8) * S5_GROUP) % LANES
                    src = steps[half * STEPS_PER_TILE + tq]
                    acc = jnp.where(lane_step == tq, pltpu.roll(src, shift, 1) if shift else src, acc)
                ug_ref[0, lt * GROUPS_PER_TILE + g8, :, half * LANES:(half + 1) * LANES] = acc.astype(BF16)


def _s5_in(x, mod, g_mix):
    d = g_mix.shape[0]
    bsz, n = _stream_dims(x, d)
    tm = _row_block(n)
    groups = d // S5_GROUP
    lw = S5_CHUNK * S5_GROUP
    return pl.pallas_call(
        functools.partial(_s5_in_kernel, tm=tm),
        out_shape=jax.ShapeDtypeStruct((bsz, groups, n // S5_CHUNK, lw), BF16),
        grid=(bsz, n // tm),
        in_specs=[_stream_spec(x, tm, d),
                  pl.BlockSpec((1, 6, d), lambda b, i: (b, 0, 0)), _full((1, d))],
        out_specs=pl.BlockSpec((1, groups, tm // S5_CHUNK, lw), lambda b, i: (b, 0, i, 0)),
        scratch_shapes=[pltpu.VMEM((d // LANES, tm, LANES), F32)],
        compiler_params=_cparams("parallel", "parallel"), name="s5_in",
    )(x, mod, g_mix.reshape(1, d))


def _s5_tables(lam_re, lam_im, log_dt, b_re, b_im, c_re, c_im):
    L = S5_CHUNK
    dt = jnp.exp(log_dt)[:, :, None]
    mag = jnp.exp(lam_re * dt)
    ang = lam_im * dt
    lb_re = mag * jnp.cos(ang)
    lb_im = mag * jnp.sin(ang)
    nr = lb_re - 1.0
    ni = lb_im
    den = lam_re * lam_re + lam_im * lam_im
    coef_re = ((nr * lam_re + ni * lam_im) / den)[..., None]
    coef_im = ((ni * lam_re - nr * lam_im) / den)[..., None]
    bb_re = coef_re * b_re - coef_im * b_im
    bb_im = coef_re * b_im + coef_im * b_re

    def powers(steps):
        st = steps.astype(F32)
        m = jnp.exp((lam_re * dt)[..., None] * st)
        a = (lam_im * dt)[..., None] * st
        return m * jnp.cos(a), m * jnp.sin(a)

    pw_re, pw_im = powers(jnp.arange(L + 1))
    cp_re = c_re[:, :, None] * pw_re.transpose(0, 1, 3, 2)[:, :, :, None, :] \
        - c_im[:, :, None] * pw_im.transpose(0, 1, 3, 2)[:, :, :, None, :]
    cp_im = c_re[:, :, None] * pw_im.transpose(0, 1, 3, 2)[:, :, :, None, :] \
        + c_im[:, :, None] * pw_re.transpose(0, 1, 3, 2)[:, :, :, None, :]
    kern = (jnp.einsum("dgtkp,dgpj->dgtkj", cp_re[:, :, :L], bb_re, precision=HIGHEST)
            - jnp.einsum("dgtkp,dgpj->dgtkj", cp_im[:, :, :L], bb_im, precision=HIGHEST))
    lbb_re = pw_re.transpose(0, 1, 3, 2)[..., None] * bb_re[:, :, None] \
        - pw_im.transpose(0, 1, 3, 2)[..., None] * bb_im[:, :, None]
    lbb_im = pw_re.transpose(0, 1, 3, 2)[..., None] * bb_im[:, :, None] \
        + pw_im.transpose(0, 1, 3, 2)[..., None] * bb_re[:, :, None]

    G = lam_re.shape[1]
    K = S5_GROUP
    P = S5_STATE
    tes, fos = [], []
    for direction in range(2):
        kpad = jnp.pad(kern[direction], ((0, 0), (L - 1, 0), (0, 0), (0, 0)))
        if direction == 0:
            kd = jnp.stack([kpad[:, L - 1 - s_:2 * L - 1 - s_] for s_ in range(L)], axis=1)
            e_pow = (L - 1) - jnp.arange(L)
            f_pow = jnp.arange(L) + 1
        else:
            kd = jnp.stack([kpad[:, s_:s_ + L][:, ::-1] for s_ in range(L)], axis=1)
            e_pow = jnp.arange(L)
            f_pow = L - jnp.arange(L)
        tmat = kd.transpose(0, 1, 4, 2, 3).reshape(G, L * K, L * K)
        e_re = lbb_re[direction][:, e_pow]
        e_im = lbb_im[direction][:, e_pow]
        emat = jnp.concatenate([e_re.transpose(0, 1, 3, 2).reshape(G, L * K, P),
                                e_im.transpose(0, 1, 3, 2).reshape(G, L * K, P)], axis=-1)
        tes.append(jnp.concatenate([tmat, emat], axis=-1))
        f_re = cp_re[direction][:, f_pow]
        f_im = cp_im[direction][:, f_pow]
        fos.append(jnp.concatenate([f_re.transpose(0, 3, 1, 2).reshape(G, P, L * K),
                                    -f_im.transpose(0, 3, 1, 2).reshape(G, P, L * K)], axis=1))
    te = jnp.stack(tes).astype(BF16)
    fo = jnp.stack(fos).astype(BF16)
    a_re, a_im = powers(jnp.full((1,), L))
    a_re, a_im = a_re[..., 0], a_im[..., 0]
    tab = jnp.stack([jnp.concatenate([a_re, a_re], axis=-1),
                     jnp.concatenate([-a_im, a_im], axis=-1)], axis=1)
    return te, fo, tab


S5_GROUPS_PER_STEP = 8


S5_SCAN_UNROLL = 4


def _s5_core_kernel(u_ref, tef_ref, teb_ref, fof_ref, fob_ref, tab_ref, yl_ref, yc_ref,
                    yf_ref, yb_ref, hf_ref, hb_ref, sf_ref, sb_ref, *, nctx, nlat):
    gb = S5_GROUPS_PER_STEP
    rows = nctx + nlat
    lw = S5_CHUNK * S5_GROUP
    for g in range(gb):
        u = u_ref[0, g]
        zf = jnp.dot(u[0:rows], tef_ref[g], preferred_element_type=F32)
        yf_ref[g] = zf[:, :lw]
        hf_ref[pl.ds(g, rows, stride=gb), :] = zf[:, lw:]
        sf_ref[pl.ds(g, rows, stride=gb), :] = pltpu.roll(zf[:, lw:], S5_STATE, 1)
        zb = jnp.dot(u[nctx:nctx + rows], teb_ref[g], preferred_element_type=F32)
        yb_ref[g] = zb[:, :lw]
        hb_ref[pl.ds(g, rows, stride=gb), :] = zb[:, lw:]
        sb_ref[pl.ds(g, rows, stride=gb), :] = pltpu.roll(zb[:, lw:], S5_STATE, 1)

    cat_f, swp_f = tab_ref[0, 0], tab_ref[0, 1]
    cat_b, swp_b = tab_ref[1, 0], tab_ref[1, 1]

    def scan_rows(i, carry):
        hf, sf, hb, sb = carry
        rf = pl.multiple_of(i * gb, gb)
        rb = pl.multiple_of((rows - 1 - i) * gb, gb)
        ef, esf = hf_ref[pl.ds(rf, gb), :], sf_ref[pl.ds(rf, gb), :]
        eb, esb = hb_ref[pl.ds(rb, gb), :], sb_ref[pl.ds(rb, gb), :]
        hf_ref[pl.ds(rf, gb), :] = hf
        hb_ref[pl.ds(rb, gb), :] = hb
        return (cat_f * hf + swp_f * sf + ef, cat_f * sf - swp_f * hf + esf,
                cat_b * hb + swp_b * sb + eb, cat_b * sb - swp_b * hb + esb)

    zero = jnp.zeros((gb, 2 * S5_STATE), F32)
    lax.fori_loop(0, rows, scan_rows, (zero, zero, zero, zero), unroll=S5_SCAN_UNROLL)

    for g in range(gb):
        hin_f = hf_ref[pl.ds(g, rows, stride=gb), :].astype(BF16)
        hin_b = hb_ref[pl.ds(g, rows, stride=gb), :].astype(BF16)
        yf = yf_ref[g] + jnp.dot(hin_f, fof_ref[g], preferred_element_type=F32)
        yb = yb_ref[g] + jnp.dot(hin_b, fob_ref[g], preferred_element_type=F32)
        yl_ref[0, g] = yf[nctx:nctx + nlat] + yb[0:nlat]
        yc_ref[0, g] = yf[0:nctx] + yb[nlat:nlat + nctx]


def _s5_core(u_all, te, fo, tab, nctx, nlat):
    bsz, G, rtot, lw = u_all.shape
    gb = S5_GROUPS_PER_STEP
    rows = nctx + nlat
    w = 2 * S5_STATE
    gspec3 = lambda shape: pl.BlockSpec((gb,) + shape, lambda b, j: (j, 0, 0))
    return pl.pallas_call(
        functools.partial(_s5_core_kernel, nctx=nctx, nlat=nlat),
        out_shape=(jax.ShapeDtypeStruct((bsz, G, nlat, lw), F32),
                   jax.ShapeDtypeStruct((bsz, G, nctx, lw), F32)),
        grid=(bsz, G // gb),
        in_specs=[pl.BlockSpec((1, gb, rtot, lw), lambda b, j: (b, j, 0, 0)),
                  gspec3((lw, lw + w)), gspec3((lw, lw + w)), gspec3((w, lw)), gspec3((w, lw)),
                  pl.BlockSpec((2, 2, gb, w), lambda b, j: (0, 0, j, 0))],
        out_specs=(pl.BlockSpec((1, gb, nlat, lw), lambda b, j: (b, j, 0, 0)),
                   pl.BlockSpec((1, gb, nctx, lw), lambda b, j: (b, j, 0, 0))),
        scratch_shapes=[pltpu.VMEM((gb, rows, lw), F32), pltpu.VMEM((gb, rows, lw), F32)]
        + [pltpu.VMEM((rows * gb, w), F32)] * 4,
        compiler_params=_cparams("parallel", "parallel"), name="s5_core",
    )(u_all, te[0], te[1], fo[0], fo[1], tab)


def _s5_out_kernel(yg_ref, x_ref, mod_ref, gmix_ref, dsk_ref, w_ref, b_ref, gffn_ref, wr_ref,
                   xo_ref, m_ref, aff_ref, slab_ref, *, tm):
    nl = gmix_ref.shape[1] // LANES
    nc = tm // S5_CHUNK
    lane_group = lax.broadcasted_iota(I32, (nc, LANES), 1) // S5_GROUP
    for lt in range(nl):
        for t in range(S5_CHUNK):
            half, tq = divmod(t, STEPS_PER_TILE)
            acc = jnp.zeros((nc, LANES), F32)
            for g8 in range(GROUPS_PER_TILE):
                src = yg_ref[0, lt * GROUPS_PER_TILE + g8, :, half * LANES:(half + 1) * LANES]
                shift = ((g8 - tq) * S5_GROUP) % LANES
                acc = jnp.where(lane_group == g8, pltpu.roll(src, shift, 1) if shift else src, acc)
            slab_ref[lt, pl.ds(t, nc, stride=S5_CHUNK), :] = acc
    ssm = jnp.concatenate([slab_ref[lt] for lt in range(nl)], axis=1)
    x = _stream_rows(x_ref, tm, gmix_ref.shape[1])
    u = _rms_mod(x, gmix_ref[...], mod_ref[0, 1:2, :], mod_ref[0, 0:1, :])
    y = ssm + dsk_ref[...] * u
    z = jnp.dot(jax.nn.gelu(y).astype(BF16), w_ref[...], preferred_element_type=F32) + b_ref[...]
    d = z.shape[1] // 2
    out = z[:, :d] * jax.nn.sigmoid(z[:, d:])
    _post(x + mod_ref[0, 2:3, :] * out, mod_ref, gffn_ref, wr_ref, xo_ref, m_ref, aff_ref)


def _s5_out(yg, x, mod, g_mix, d_skip, w_glu, b_glu, g_ffn, w_router):
    d = g_mix.shape[0]
    bsz, n = _stream_dims(x, d)
    tm = _row_block(n)
    groups = d // S5_GROUP
    lw = S5_CHUNK * S5_GROUP
    row = pl.BlockSpec((1, tm, d), lambda b, i: (b, i, 0))
    vec = _full((1, d))
    out_shape, out_specs = _post_specs(bsz, n, d, tm)
    return pl.pallas_call(
        functools.partial(_s5_out_kernel, tm=tm), out_shape=out_shape, grid=(bsz, n // tm),
        in_specs=[pl.BlockSpec((1, groups, tm // S5_CHUNK, lw), lambda b, i: (b, 0, i, 0)),
                  _stream_spec(x, tm, d), pl.BlockSpec((1, 6, d), lambda b, i: (b, 0, 0)), vec, vec,
                  _full((d, 2 * d)), _full((1, 2 * d)), vec, _full((N_EXPERTS, d))],
        out_specs=out_specs,
        scratch_shapes=[pltpu.VMEM((d // LANES, tm, LANES), F32)],
        compiler_params=_cparams("parallel", "parallel"), name="s5_out",
    )(yg, x, mod, g_mix.reshape(1, d), d_skip.reshape(1, d), w_glu.astype(BF16), b_glu.reshape(1, 2 * d),
      g_ffn.reshape(1, d), w_router.T)


def _lru_in_kernel(x_ref, mod_ref, g_ref, wx_ref, bx_ref, wy_ref, by_ref, xp_ref, gate_ref):
    x = _stream_rows(x_ref, xp_ref.shape[1], xp_ref.shape[2])
    h = _rms_mod(x, g_ref[...], mod_ref[0, 1:2, :], mod_ref[0, 0:1, :]).astype(BF16)
    xp_ref[0] = jnp.dot(h, wx_ref[...], preferred_element_type=F32) + bx_ref[...]
    gate_ref[0] = jax.nn.gelu(jnp.dot(h, wy_ref[...], preferred_element_type=F32) + by_ref[...])


def _lru_in(x, mod, g_mix, w_x, b_x, w_y, b_y):
    d = g_mix.shape[0]
    bsz, n = _stream_dims(x, d)
    tm = _row_block(n)
    row = pl.BlockSpec((1, tm, d), lambda b, i: (b, i, 0))
    vec = _full((1, d))
    return pl.pallas_call(
        _lru_in_kernel,
        out_shape=(jax.ShapeDtypeStruct((bsz, n, d), F32), jax.ShapeDtypeStruct((bsz, n, d), F32)),
        grid=(bsz, n // tm),
        in_specs=[_stream_spec(x, tm, d), pl.BlockSpec((1, 6, d), lambda b, i: (b, 0, 0)), vec,
                  _full((d, d)), vec, _full((d, d)), vec],
        out_specs=(row, row), compiler_params=_cparams("parallel", "parallel"), name="lru_in",
    )(x, mod, g_mix.reshape(1, d), w_x.astype(BF16), b_x.reshape(1, d), w_y.astype(BF16), b_y.reshape(1, d))


LRU_HALO = 8
LRU_SCAN_UNROLL = 8


def _lru_scan_kernel(xf_ref, xfp_ref, xfn_ref, xb_ref, xbp_ref, xbn_ref, h0_ref, cw_ref, cb_ref,
                     wai_ref, bai_ref, lam_ref, rf_ref, rb_ref, hfin_ref,
                     ext_ref, xl_ref, a_ref, b_ref, r_ref, h_ref, *, tm):
    i = pl.program_id(1)
    last = pl.num_programs(1) - 1
    hw = LANES
    d = LRU_HEADS * hw

    @pl.when(i == 0)
    def _():
        h_ref[...] = h0_ref[0]

    lam = lam_ref[...]
    nl = -lam
    softplus = jnp.maximum(nl, 0.0) + jnp.log1p(jnp.exp(-jnp.abs(nl)))
    c8 = -LRU_C * softplus

    def sigmoid(v):
        return 0.5 * jnp.tanh(0.5 * v) + 0.5

    def gates(dr, main_ref, prev_ref, next_ref, blk):
        _store_token_major(ext_ref, (), jnp.where(blk > 0, prev_ref[0], 0.0))
        _store_token_major(ext_ref, (), main_ref[0], row0=LRU_HALO)
        _store_token_major(ext_ref, (), jnp.where(blk < last, next_ref[0], 0.0), row0=LRU_HALO + tm)
        xl3 = jnp.broadcast_to(cb_ref[...], (tm, LRU_HEADS, hw))
        for k in range(LRU_CONV):
            off = (LRU_HALO - LRU_CONV // 2 + k) * LRU_HEADS
            xl3 = xl3 + cw_ref[k:k + 1] * ext_ref[off:off + tm * LRU_HEADS, :].reshape(tm, LRU_HEADS, hw)
        xl_ref[...] = xl3.reshape(tm * LRU_HEADS, hw)
        for hd in range(LRU_HEADS):
            sl = slice(hd * hw, (hd + 1) * hw)
            xl = xl_ref[pl.ds(hd, tm, stride=LRU_HEADS), :]
            z = jnp.dot(xl.astype(BF16), wai_ref[dr, hd], preferred_element_type=F32)
            r = sigmoid(z[:, :hw] + bai_ref[dr, 0:1, sl])
            ig = sigmoid(z[:, hw:] + bai_ref[dr, 1:2, sl])
            log_a = c8[dr:dr + 1, sl] * r
            th = jnp.tanh(log_a)
            a_ref[dr, pl.ds(hd, tm, stride=LRU_HEADS), :] = jnp.exp(log_a)
            b_ref[dr, pl.ds(hd, tm, stride=LRU_HEADS), :] = jnp.sqrt(-2.0 * th / (1.0 - th)) * (ig * xl)

    gates(0, xf_ref, xfp_ref, xfn_ref, i)
    gates(1, xb_ref, xbp_ref, xbn_ref, last - i)

    def step(t, carry):
        hf, hb = carry
        rf = pl.multiple_of(t * LRU_HEADS, LRU_HEADS)
        rb = pl.multiple_of((tm - 1 - t) * LRU_HEADS, LRU_HEADS)
        hf = a_ref[0, pl.ds(rf, LRU_HEADS), :] * hf + b_ref[0, pl.ds(rf, LRU_HEADS), :]
        hb = a_ref[1, pl.ds(rb, LRU_HEADS), :] * hb + b_ref[1, pl.ds(rb, LRU_HEADS), :]
        r_ref[0, pl.ds(rf, LRU_HEADS), :] = hf
        r_ref[1, pl.ds(rb, LRU_HEADS), :] = hb
        return hf, hb

    hf, hb = lax.fori_loop(0, tm, step, (h_ref[0], h_ref[1]), unroll=LRU_SCAN_UNROLL)
    h_ref[0] = hf
    h_ref[1] = hb
    hfin_ref[0] = h_ref[...]
    rf_ref[0] = _load_token_major(r_ref, (0,), tm, d)
    rb_ref[0] = _load_token_major(r_ref, (1,), tm, d)


def _lru_scan(xp, h0, conv_w, conv_b, w_a, b_a, w_i, b_i, lam):
    bsz, n, d = xp.shape
    tm = _row_block(n)
    nb = n // tm
    hb = tm // LRU_HALO
    nh = n // LRU_HALO
    fwd = lambda b, i: (b, i, 0)
    bwd = lambda b, i: (b, nb - 1 - i, 0)
    halo = lambda f: pl.BlockSpec((1, LRU_HALO, d), f)
    wai = jnp.concatenate([w_a, w_i], axis=-1).astype(BF16)
    bai = jnp.stack([b_a, b_i], axis=1)
    assert d // LRU_HEADS == LANES, "token-major scan assumes one 128-lane tile per head"
    state = pl.BlockSpec((1, 2, LRU_HEADS, LANES), lambda b, i: (b, 0, 0, 0))
    return pl.pallas_call(
        functools.partial(_lru_scan_kernel, tm=tm),
        out_shape=(jax.ShapeDtypeStruct((bsz, n, d), F32), jax.ShapeDtypeStruct((bsz, n, d), F32),
                   jax.ShapeDtypeStruct((bsz, 2, LRU_HEADS, LANES), F32)),
        grid=(bsz, nb),
        in_specs=[pl.BlockSpec((1, tm, d), fwd),
                  halo(lambda b, i: (b, jnp.maximum(i * hb - 1, 0), 0)),
                  halo(lambda b, i: (b, jnp.minimum((i + 1) * hb, nh - 1), 0)),
                  pl.BlockSpec((1, tm, d), bwd),
                  halo(lambda b, i: (b, jnp.maximum((nb - 1 - i) * hb - 1, 0), 0)),
                  halo(lambda b, i: (b, jnp.minimum((nb - i) * hb, nh - 1), 0)),
                  state,
                  _full((LRU_CONV, LRU_HEADS, LANES)), _full((1, LRU_HEADS, LANES)),
                  _full(wai.shape), _full(bai.shape), _full((2, d))],
        out_specs=(pl.BlockSpec((1, tm, d), fwd), pl.BlockSpec((1, tm, d), bwd), state),
        scratch_shapes=[pltpu.VMEM(((tm + 2 * LRU_HALO) * LRU_HEADS, LANES), F32),
                        pltpu.VMEM((tm * LRU_HEADS, LANES), F32)]
        + [pltpu.VMEM((2, tm * LRU_HEADS, LANES), F32)] * 3
        + [pltpu.VMEM((2, LRU_HEADS, LANES), F32)],
        compiler_params=_cparams("parallel", "arbitrary"), name="lru_scan",
    )(xp, xp, xp, xp, xp, xp, h0, conv_w.reshape(LRU_CONV, LRU_HEADS, LANES), conv_b.reshape(1, LRU_HEADS, LANES),
      wai, bai, lam)


def _lru_out_kernel(rf_ref, rb_ref, gate_ref, x_ref, mod_ref, w_ref, b_ref, gffn_ref, wr_ref,
                    xo_ref, m_ref, aff_ref):
    r = (rf_ref[0] + rb_ref[0]) * gate_ref[0]
    y = jnp.dot(r.astype(BF16), w_ref[...], preferred_element_type=F32) + b_ref[...]
    x = _stream_rows(x_ref, rf_ref.shape[1], rf_ref.shape[2])
    _post(x + mod_ref[0, 2:3, :] * y, mod_ref, gffn_ref, wr_ref, xo_ref, m_ref, aff_ref)


def _lru_out(rf, rb, gate, x, mod, w_out, b_out, g_ffn, w_router):
    bsz, n, d = rf.shape
    tm = _row_block(n)
    row = pl.BlockSpec((1, tm, d), lambda b, i: (b, i, 0))
    vec = _full((1, d))
    out_shape, out_specs = _post_specs(bsz, n, d, tm)
    return pl.pallas_call(
        _lru_out_kernel, out_shape=out_shape, grid=(bsz, n // tm),
        in_specs=[row, row, row, _stream_spec(x, tm, d), pl.BlockSpec((1, 6, d), lambda b, i: (b, 0, 0)),
                  _full((d, d)), vec, vec, _full((N_EXPERTS, d))],
        out_specs=out_specs, compiler_params=_cparams("parallel", "parallel"), name="lru_out",
    )(rf, rb, gate, x, mod, w_out.astype(BF16), b_out.reshape(1, d), g_ffn.reshape(1, d), w_router.T)


INF_BITS = 0x7F800000
SLOT_UNROLL = 8


def _tile_cumsum(mask, tri):
    return jnp.dot(mask.astype(BF16), tri, preferred_element_type=F32)


def _select_kernel(aff_ref, lpos_ref, off_ref, sel_ref, *, n, cap):
    nt = n // LANES
    bits = pltpu.bitcast(aff_ref[0], I32)
    capf = float(cap)

    def bisect(_, lohi):
        lo, hi = lohi
        mid = lo + ((hi - lo + 1) >> 1)
        cnt = jnp.sum(jnp.where(bits >= mid, 1.0, 0.0), axis=1, keepdims=True)
        ok = cnt >= capf
        return jnp.where(ok, mid, lo), jnp.where(ok, hi, mid - 1)

    e = bits.shape[0]
    thr, _ = lax.fori_loop(0, 31, bisect, (jnp.zeros((e, 1), I32), jnp.full((e, 1), INF_BITS, I32)))
    gt = jnp.where(bits > thr, 1.0, 0.0)
    eq = jnp.where(bits == thr, 1.0, 0.0)
    need = capf - jnp.sum(gt, axis=1, keepdims=True)

    r_i = lax.broadcasted_iota(I32, (LANES, LANES), 0)
    c_i = lax.broadcasted_iota(I32, (LANES, LANES), 1)
    tri = jnp.where(r_i <= c_i, 1.0, 0.0).astype(BF16)
    sup = jnp.where(r_i < c_i, 1.0, 0.0).astype(BF16)
    t_i = lax.broadcasted_iota(I32, (n, LANES), 0)
    k_i = lax.broadcasted_iota(I32, (n, LANES), 1)
    tile_of = jnp.where((t_i >> 7) == k_i, 1.0, 0.0).astype(BF16)

    def tile_offsets(mask):
        counts = jnp.dot(mask.astype(BF16), tile_of, preferred_element_type=F32)
        return jnp.dot(counts.astype(BF16), sup, preferred_element_type=F32)

    eq_off = tile_offsets(eq)
    for k in range(nt):
        sl = slice(k * LANES, (k + 1) * LANES)
        eqk = eq[:, sl]
        rank = eq_off[:, k:k + 1] + _tile_cumsum(eqk, tri) - eqk
        sel_ref[:, sl] = gt[:, sl] + eqk * jnp.where(rank < need, 1.0, 0.0)
    sel = sel_ref[...]
    off_ref[0] = tile_offsets(sel).astype(I32)
    for k in range(nt):
        sl = slice(k * LANES, (k + 1) * LANES)
        sk = sel[:, sl]
        lpos_ref[0, :, sl] = jnp.where(sk > 0.0, _tile_cumsum(sk, tri) - 1.0, -1.0).astype(I32)


def _select(aff_t, cap):
    bsz, e, n = aff_t.shape
    return pl.pallas_call(
        functools.partial(_select_kernel, n=n, cap=cap),
        out_shape=(jax.ShapeDtypeStruct((bsz, e, n), I32), jax.ShapeDtypeStruct((bsz, e, LANES), I32)),
        grid=(bsz,),
        in_specs=[pl.BlockSpec((1, e, n), lambda b: (b, 0, 0))],
        out_specs=(pl.BlockSpec((1, e, n), lambda b: (b, 0, 0)),
                   pl.BlockSpec((1, e, LANES), lambda b: (b, 0, 0))),
        scratch_shapes=[pltpu.VMEM((e, n), F32)],
        compiler_params=_cparams("parallel"), name="moe_select",
    )(aff_t)


def _slots_kernel(off_ref, lpos_ref, aff_ref, out_ref, scr_ref, *, nt, cap):
    b = pl.program_id(0)
    e = pl.program_id(1)
    base = (b * pl.num_programs(1) + e) * LANES
    j_i = lax.broadcasted_iota(I32, (LANES, LANES), 0)
    row = lax.broadcasted_iota(I32, (LANES, LANES), 0)
    lane = lax.broadcasted_iota(I32, (LANES, LANES), 1)

    def tile(k, carry):
        lp = lpos_ref[0, 0, k]
        av = aff_ref[0, 0, k]
        onehot = jnp.where(lp == j_i, 1.0, 0.0).astype(BF16)
        a1 = av.astype(BF16).astype(F32)
        r1 = av - a1
        a2 = r1.astype(BF16).astype(F32)
        a3 = r1 - a2
        q = jnp.where(row == 0, lane.astype(F32),
                      jnp.where(row == 1, a1, jnp.where(row == 2, a2, jnp.where(row == 3, a3, 0.0))))
        res = lax.dot_general(onehot, q.astype(BF16), (((1,), (1,)), ((), ())),
                              preferred_element_type=F32)
        res = res + jnp.where(lane == 0, jnp.asarray(k * LANES, F32), 0.0)
        scr_ref[pl.ds(off_ref[base + k], LANES), :] = res
        return carry

    lax.fori_loop(0, nt, tile, 0, unroll=min(SLOT_UNROLL, nt))
    out_ref[0, 0] = scr_ref[0:cap, :]


def _slots(lpos, tile_off, aff_t, cap):
    bsz, e, n = lpos.shape
    nt = n // LANES
    lp5 = lpos.reshape(bsz, e, nt, 1, LANES)
    af5 = aff_t.reshape(bsz, e, nt, 1, LANES)
    blk = pl.BlockSpec((1, 1, nt, 1, LANES), lambda b, j, off: (b, j, 0, 0, 0))
    return pl.pallas_call(
        functools.partial(_slots_kernel, nt=nt, cap=cap),
        out_shape=jax.ShapeDtypeStruct((bsz, e, cap, LANES), F32),
        grid_spec=pltpu.PrefetchScalarGridSpec(
            num_scalar_prefetch=1, grid=(bsz, e), in_specs=[blk, blk],
            out_specs=pl.BlockSpec((1, 1, cap, LANES), lambda b, j, off: (b, j, 0, 0)),
            scratch_shapes=[pltpu.VMEM((cap + LANES, LANES), F32)]),
        compiler_params=_cparams("parallel", "parallel"), name="moe_slots",
    )(tile_off.reshape(-1), lp5, af5)


ROW_UNROLL = 8


def _gather_kernel(idx_ref, m_ref, xs_ref, scr_ref, *, cap, nl):
    b = pl.program_id(0)
    e = pl.program_id(1)
    base = (b * pl.num_programs(1) + e) * cap

    def rows(ju, carry):
        j0 = ju * ROW_UNROLL
        for r in range(ROW_UNROLL):
            t = idx_ref[base + j0 + r]
            scr_ref[pl.ds(pl.multiple_of((j0 + r) * nl, nl), nl), :] = \
                m_ref[0, pl.ds(pl.multiple_of(t * nl, nl), nl), :]
        return carry

    lax.fori_loop(0, cap // ROW_UNROLL, rows, 0)
    xs_ref[0, 0] = _load_token_major(scr_ref, (), cap, nl * LANES).astype(BF16)


def _gather(idx_flat, m_tok, e, cap):
    bsz, rows_, _ = m_tok.shape
    nl = SUBLANES
    d = nl * LANES
    return pl.pallas_call(
        functools.partial(_gather_kernel, cap=cap, nl=nl),
        out_shape=jax.ShapeDtypeStruct((bsz, e, cap, d), BF16),
        grid_spec=pltpu.PrefetchScalarGridSpec(
            num_scalar_prefetch=1, grid=(bsz, e),
            in_specs=[pl.BlockSpec((1, rows_, LANES), lambda b, j, idx: (b, 0, 0),
                                   pipeline_mode=pl.Buffered(1))],
            out_specs=pl.BlockSpec((1, 1, cap, d), lambda b, j, idx: (b, j, 0, 0)),
            scratch_shapes=[pltpu.VMEM((cap * nl, LANES), F32)]),
        compiler_params=_cparams("parallel", "arbitrary"), name="moe_gather",
    )(idx_flat, m_tok)


FFN_CHUNK = 512


def _ffn_rows(xs, gs, w1b_ref, w3b_ref, w2b_ref):
    de = w1b_ref.shape[1]
    fc = min(FFN_CHUNK, de)
    acc = None
    for c in range(de // fc):
        sl = slice(c * fc, (c + 1) * fc)
        h1 = jnp.dot(xs, w1b_ref[:, sl], preferred_element_type=F32)
        h3 = jnp.dot(xs, w3b_ref[:, sl], preferred_element_type=F32)
        h = (_silu(h1) * h3).astype(BF16)
        part = jnp.dot(h, w2b_ref[sl, :], preferred_element_type=F32)
        acc = part if acc is None else acc + part
    gate = gs[:, 1:2] + gs[:, 2:3] + gs[:, 3:4]
    return acc * gate


def _ffn_kernel(*refs, n_streams, batched):
    ins = refs[:2 * n_streams]
    w1_ref, w3_ref, w2_ref = refs[2 * n_streams:2 * n_streams + 3]
    outs = refs[2 * n_streams + 3:3 * n_streams + 3]
    w1b_ref, w3b_ref, w2b_ref = refs[3 * n_streams + 3:]

    @pl.when(pl.program_id(1) == 0)
    def _():
        w1b_ref[...] = w1_ref[0, 0].astype(BF16)
        w3b_ref[...] = w3_ref[0, 0].astype(BF16)
        w2b_ref[...] = w2_ref[0, 0].astype(BF16)

    for k in range(n_streams):
        xs_ref, gs_ref, y_ref = ins[2 * k], ins[2 * k + 1], outs[k]
        if batched[k]:
            @pl.when(pl.program_id(1) == 0)
            def _(xs_ref=xs_ref, gs_ref=gs_ref, y_ref=y_ref):
                nb, _, cap, d = xs_ref.shape
                y = _ffn_rows(xs_ref[:, 0].reshape(nb * cap, d), gs_ref[:, 0].reshape(nb * cap, LANES),
                              w1b_ref, w3b_ref, w2b_ref)
                for s in range(nb):
                    _store_token_major(y_ref, (s, 0), y[s * cap:(s + 1) * cap])
        else:
            _store_token_major(y_ref, (0, 0), _ffn_rows(xs_ref[0, 0], gs_ref[0, 0], w1b_ref, w3b_ref, w2b_ref))


FFN_BATCH_ROWS = 128


def _ffn(streams, experts, layer):
    w1, w3, w2 = experts
    bsz, e, _, d = streams[0][0].shape
    de = w1.shape[3]
    nl = d // LANES
    in_specs, args, out_shape, out_specs, batched = [], [], [], [], []
    for xs, slots in streams:
        cap = xs.shape[2]
        whole = cap < FFN_BATCH_ROWS
        batched.append(whole)
        nb = bsz if whole else 1
        sample = (lambda j, b: (0, j, 0, 0)) if whole else (lambda j, b: (b, j, 0, 0))
        in_specs += [pl.BlockSpec((nb, 1, cap, d), sample), pl.BlockSpec((nb, 1, cap, LANES), sample)]
        args += [xs, slots]
        out_shape.append(jax.ShapeDtypeStruct((bsz, e, cap * nl, LANES), F32))
        out_specs.append(pl.BlockSpec((nb, 1, cap * nl, LANES), sample))
    in_specs += [pl.BlockSpec((1, 1, d, de), lambda j, b: (layer, j, 0, 0)),
                 pl.BlockSpec((1, 1, d, de), lambda j, b: (layer, j, 0, 0)),
                 pl.BlockSpec((1, 1, de, d), lambda j, b: (layer, j, 0, 0))]
    return pl.pallas_call(
        functools.partial(_ffn_kernel, n_streams=len(streams), batched=tuple(batched)),
        out_shape=tuple(out_shape),
        grid=(e, bsz), in_specs=in_specs, out_specs=tuple(out_specs),
        scratch_shapes=[pltpu.VMEM((d, de), BF16), pltpu.VMEM((d, de), BF16), pltpu.VMEM((de, d), BF16)],
        compiler_params=_cparams("parallel", "arbitrary"), name="moe_ffn",
    )(*args, w1, w3, w2)


COMBINE_UNROLL = 4


def _combine_kernel(idx_ref, y_ref, g2_ref, xin_hbm, acc_ref, sem, *, cap, nl):
    b = pl.program_id(0)
    e = pl.program_id(1)
    base = (b * pl.num_programs(1) + e) * cap

    @pl.when(e == 0)
    def _():
        load = pltpu.make_async_copy(xin_hbm.at[b], acc_ref.at[0], sem)
        load.start()
        load.wait()

    g2 = g2_ref[0]

    def rows(ju, carry):
        j0 = ju * COMBINE_UNROLL
        ts = [pl.multiple_of(idx_ref[base + j0 + r] * nl, nl) for r in range(COMBINE_UNROLL)]
        vals = [acc_ref[0, pl.ds(ts[r], nl), :]
                + g2 * y_ref[0, 0, pl.ds(pl.multiple_of((j0 + r) * nl, nl), nl), :]
                for r in range(COMBINE_UNROLL)]
        for r in range(COMBINE_UNROLL):
            acc_ref[0, pl.ds(ts[r], nl), :] = vals[r]
        return carry

    lax.fori_loop(0, cap // COMBINE_UNROLL, rows, 0)


def _combine(idx_flat, y_tok, x_tok, gate2, cap):
    bsz, e, rows_, _ = y_tok.shape
    nl = rows_ // cap
    return pl.pallas_call(
        functools.partial(_combine_kernel, cap=cap, nl=nl),
        out_shape=jax.ShapeDtypeStruct(x_tok.shape, F32),
        grid_spec=pltpu.PrefetchScalarGridSpec(
            num_scalar_prefetch=1, grid=(bsz, e),
            in_specs=[pl.BlockSpec((1, 1, rows_, LANES), lambda b, j, idx: (b, j, 0, 0)),
                      pl.BlockSpec((1, nl, LANES), lambda b, j, idx: (b, 0, 0)),
                      pl.BlockSpec(memory_space=pl.ANY)],
            out_specs=pl.BlockSpec((1,) + x_tok.shape[1:], lambda b, j, idx: (b, 0, 0),
                                   pipeline_mode=pl.Buffered(1)),
            scratch_shapes=[pltpu.SemaphoreType.DMA(())]),
        compiler_params=_cparams("parallel", "arbitrary"), name="moe_combine",
    )(idx_flat, y_tok, gate2.reshape(bsz, nl, LANES), x_tok)


def _final_norm_kernel(x_ref, g_ref, o_ref):
    x = _stream_rows(x_ref, o_ref.shape[1], o_ref.shape[2])
    o_ref[0] = x * lax.rsqrt(jnp.mean(x * x, axis=-1, keepdims=True) + EPS) * g_ref[...]


def _final_norm(x, final_g):
    d = final_g.shape[0]
    bsz, n = _stream_dims(x, d)
    tm = _row_block(n)
    return pl.pallas_call(
        _final_norm_kernel, out_shape=jax.ShapeDtypeStruct((bsz, n, d), F32), grid=(bsz, n // tm),
        in_specs=[_stream_spec(x, tm, d), _full((1, d))],
        out_specs=pl.BlockSpec((1, tm, d), lambda b, i: (b, i, 0)),
        compiler_params=_cparams("parallel", "parallel"), name="final_norm",
    )(x, final_g.reshape(1, d))


def _moe(streams, experts, layer):
    routed = []
    for x_tok, m_tok, aff_t, mod in streams:
        e, n = aff_t.shape[1], aff_t.shape[2]
        cap = EC_CAPACITY * n // e
        lpos, tile_off = _select(aff_t, cap)
        slots = _slots(lpos, tile_off, aff_t, cap)
        idx_flat = slots[..., 0].astype(I32).reshape(-1)
        routed.append((idx_flat, _gather(idx_flat, m_tok, e, cap), slots, cap))
    ys = _ffn([(xs, slots) for _, xs, slots, _ in routed], experts, layer)
    return [_combine(idx_flat, y_tok, x_tok, mod[:, 5], cap)
            for (idx_flat, _, _, cap), y_tok, (x_tok, _, _, mod) in zip(routed, ys, streams)]


def _pos_embed(n, d):
    rows = n // GRID_W
    quarter = d // 4
    omega = 1.0 / (POS_BASE ** (jnp.arange(quarter, dtype=F32) / quarter))
    r = jnp.arange(rows, dtype=F32)[:, None] * omega
    cc = jnp.arange(GRID_W, dtype=F32)[:, None] * omega
    row_emb = jnp.concatenate([jnp.sin(r), jnp.cos(r)], axis=-1)
    col_emb = jnp.concatenate([jnp.sin(cc), jnp.cos(cc)], axis=-1)
    emb = jnp.concatenate([
        jnp.broadcast_to(row_emb[:, None, :], (rows, GRID_W, d // 2)),
        jnp.broadcast_to(col_emb[None, :, :], (rows, GRID_W, d // 2))], axis=-1)
    return emb.reshape(rows * GRID_W, d)


def kernel(x, c, ctx, c_ctx, w_mod, b_mod, g_mix, g_ffn, conv_w_in, conv_b_in, conv_dw, conv_dw_b, conv_ln_g, conv_ln_b, conv_w_out, conv_b_out, s5_lam_re, s5_lam_im, s5_log_dt, s5_b_re, s5_b_im, s5_c_re, s5_c_im, s5_d, s5_w_glu, s5_b_glu, lru_w_y, lru_b_y, lru_w_x, lru_b_x, lru_conv_w, lru_conv_b, lru_w_a, lru_b_a, lru_w_i, lru_b_i, lru_lam, lru_w_out, lru_b_out, moe_router, moe_w1, moe_w3, moe_w2, final_g):
    bsz, n, d = x.shape
    nctx = ctx.shape[1]
    depth = w_mod.shape[0]
    n_mixers = 3
    reader_layers = [i for i in range(depth) if i % n_mixers != 0]
    last_reader = max(reader_layers) if reader_layers else -1

    rows = -(-(bsz + 1) // SUBLANES) * SUBLANES
    cond = jnp.zeros((rows, d), F32).at[:bsz].set(c).at[bsz].set(c_ctx)
    mods = _modulation(cond, w_mod, b_mod).reshape(depth, rows, 6, d)
    pe = _pos_embed(n, d)
    assert d == SUBLANES * LANES, "token-major MoE rows assume one (8, 128) tile per token"
    experts = (moe_w1, moe_w3, moe_w2)

    x_lat, x_ctx = x, ctx
    for i in range(depth):
        kind, j = i % n_mixers, i // n_mixers
        ctx_in = i <= last_reader
        ctx_out = i < last_reader
        mod_lat = mods[i, :bsz]
        mod_ctx = jnp.broadcast_to(mods[i, bsz][None], (bsz, 6, d))
        lat_pe = pe if i == 0 else None
        streams = [(x_lat, mod_lat, lat_pe)]
        if ctx_out:
            streams.append((x_ctx, mod_ctx, None))

        if kind == 0:
            post = []
            for xs_, mod_, pe_ in streams:
                post.append(_conv_mixer(xs_, pe_, mod_, g_mix[i], conv_w_in[j], conv_b_in[j], conv_dw[j],
                                        conv_dw_b[j], conv_ln_g[j], conv_ln_b[j], conv_w_out[j], conv_b_out[j],
                                        g_ffn[i], moe_router[i]))
        elif kind == 1:
            te, fo, tab = _s5_tables(s5_lam_re[j], s5_lam_im[j], s5_log_dt[j], s5_b_re[j], s5_b_im[j],
                                     s5_c_re[j], s5_c_im[j])
            uc = _s5_in(x_ctx, mod_ctx, g_mix[i])
            u_all = jnp.concatenate([uc, _s5_in(x_lat, mod_lat, g_mix[i]), uc], axis=2)
            y_lat, y_ctx = _s5_core(u_all, te, fo, tab, nctx // S5_CHUNK, n // S5_CHUNK)
            post = [_s5_out(y_lat, x_lat, mod_lat, g_mix[i], s5_d[j], s5_w_glu[j], s5_b_glu[j],
                            g_ffn[i], moe_router[i])]
            if ctx_out:
                post.append(_s5_out(y_ctx, x_ctx, mod_ctx, g_mix[i], s5_d[j], s5_w_glu[j],
                                    s5_b_glu[j], g_ffn[i], moe_router[i]))
        else:
            scan_args = (lru_conv_w[j], lru_conv_b[j], lru_w_a[j], lru_b_a[j], lru_w_i[j], lru_b_i[j], lru_lam[j])
            xp_ctx, gate_ctx = _lru_in(x_ctx, mod_ctx, g_mix[i], lru_w_x[j], lru_b_x[j], lru_w_y[j], lru_b_y[j])
            rf_c, rb_c, h_ctx = _lru_scan(xp_ctx, jnp.zeros((bsz, 2, LRU_HEADS, d // LRU_HEADS), F32), *scan_args)
            xp_lat, gate_lat = _lru_in(x_lat, mod_lat, g_mix[i], lru_w_x[j], lru_b_x[j], lru_w_y[j], lru_b_y[j])
            rf_l, rb_l, _ = _lru_scan(xp_lat, h_ctx, *scan_args)
            post = [_lru_out(rf_l, rb_l, gate_lat, x_lat, mod_lat, lru_w_out[j], lru_b_out[j],
                             g_ffn[i], moe_router[i])]
            if ctx_out:
                post.append(_lru_out(rf_c, rb_c, gate_ctx, x_ctx, mod_ctx, lru_w_out[j], lru_b_out[j],
                                     g_ffn[i], moe_router[i]))

        mods_ = [mod_lat, mod_ctx]
        outs = _moe([(xn, m, aff, mods_[k]) for k, (xn, m, aff) in enumerate(post)], experts, i)
        x_lat = outs[0]
        if ctx_out:
            x_ctx = outs[1]
    return _final_norm(x_lat, final_g)
```

```python
import functools
import math

import jax
import jax.numpy as jnp
from jax import lax
from jax.experimental import pallas as pl
from jax.experimental.pallas import tpu as pltpu

F32 = jnp.float32
BF16 = jnp.bfloat16
I32 = jnp.int32
HIGHEST = lax.Precision.HIGHEST

EPS = 1e-6
N_EXPERTS = 16
EC_CAPACITY = 2
CONV_WIDTH = 31
LRU_CONV = 4
LRU_HEADS = 8
LRU_C = 8.0
S5_GROUP = 16
S5_STATE = 64
S5_CHUNK = 16
GRID_W = 64
POS_BASE = 10000.0

LANES = 128
SUBLANES = 8
ROW_BLOCK = 512
VMEM_LIMIT = 56 * 1024 * 1024


def _cparams(*sem):
    return pltpu.CompilerParams(dimension_semantics=sem, vmem_limit_bytes=VMEM_LIMIT)


def _full(shape):
    nd = len(shape)
    return pl.BlockSpec(shape, lambda *_: (0,) * nd)


def _rms_mod(x, g, scale, shift):
    y = x * lax.rsqrt(jnp.mean(x * x, axis=-1, keepdims=True) + EPS)
    return (y * g) * (1.0 + scale) + shift


def _silu(x):
    return x * jax.nn.sigmoid(x)


def _row_block(n):
    return min(ROW_BLOCK, n)


def _mod_kernel(c_ref, w_ref, b_ref, o_ref):
    s = _silu(c_ref[...])
    o_ref[0] = jnp.dot(s, w_ref[0], precision=HIGHEST, preferred_element_type=F32) + b_ref[0]


def _modulation(cond, w_mod, b_mod):
    depth, d, d6 = w_mod.shape
    rows = cond.shape[0]
    tn = 1536
    return pl.pallas_call(
        _mod_kernel,
        out_shape=jax.ShapeDtypeStruct((depth, rows, d6), F32),
        grid=(depth, d6 // tn),
        in_specs=[pl.BlockSpec((rows, d), lambda i, j: (0, 0)),
                  pl.BlockSpec((1, d, tn), lambda i, j: (i, 0, j)),
                  pl.BlockSpec((1, 1, tn), lambda i, j: (i, 0, j))],
        out_specs=pl.BlockSpec((1, rows, tn), lambda i, j: (i, 0, j)),
        compiler_params=_cparams("parallel", "parallel"),
        name="modulation",
    )(cond, w_mod, b_mod.reshape(depth, 1, d6))


def _store_token_major(ref, lead, val, row0=0):
    rows, d = val.shape
    nl = d // LANES
    for c in range(nl):
        ref[lead + (pl.ds(row0 * nl + c, rows, stride=nl), slice(None))] = val[:, c * LANES:(c + 1) * LANES]


def _load_token_major(ref, lead, rows, d):
    nl = d // LANES
    return jnp.concatenate([ref[lead + (pl.ds(c, rows, stride=nl), slice(None))] for c in range(nl)], axis=1)


def _is_token_major(shape):
    return shape[-1] == LANES


def _stream_dims(x, d):
    return (x.shape[0], x.shape[1] * LANES // d) if _is_token_major(x.shape) else x.shape[:2]


def _stream_spec(x, tm, d):
    if _is_token_major(x.shape):
        return pl.BlockSpec((1, tm * (d // LANES), LANES), lambda b, i: (b, i, 0))
    return pl.BlockSpec((1, tm, d), lambda b, i: (b, i, 0))


def _stream_rows(x_ref, tm, d):
    if _is_token_major(x_ref.shape):
        return _load_token_major(x_ref, (0,), tm, d)
    return x_ref[0]


def _post(xn, mod_ref, gffn_ref, wr_ref, xo_ref, m_ref, aff_ref):
    _store_token_major(xo_ref, (0,), xn)
    m = _rms_mod(xn, gffn_ref[...], mod_ref[0, 4:5, :], mod_ref[0, 3:4, :])
    _store_token_major(m_ref, (0,), m)
    logits = lax.dot_general(wr_ref[...], m, (((1,), (1,)), ((), ())),
                             precision=HIGHEST, preferred_element_type=F32)
    ex = jnp.exp(logits - jnp.max(logits, axis=0, keepdims=True))
    aff_ref[0] = ex / jnp.sum(ex, axis=0, keepdims=True)


def _post_specs(bsz, n, d, tm):
    nl = d // LANES
    out_shape = (jax.ShapeDtypeStruct((bsz, n * nl, LANES), F32),
                 jax.ShapeDtypeStruct((bsz, n * nl, LANES), F32),
                 jax.ShapeDtypeStruct((bsz, N_EXPERTS, n), F32))
    out_specs = (pl.BlockSpec((1, tm * nl, LANES), lambda b, i: (b, i, 0)),
                 pl.BlockSpec((1, tm * nl, LANES), lambda b, i: (b, i, 0)),
                 pl.BlockSpec((1, N_EXPERTS, tm), lambda b, i: (b, 0, i)))
    return out_shape, out_specs


CONV_HALO = 16
CONV_ROWS = 128


def _conv_mixer_kernel(*refs, has_pe, tm):
    if has_pe:
        (x_ref, xp_ref, xn_ref, pe_ref, pep_ref, pen_ref, mod_ref, g_ref, win_ref, bin_ref, dw_ref, dwb_ref,
         lng_ref, lnb_ref, w_ref, b_ref, gffn_ref, wr_ref, xo_ref, m_ref, aff_ref, ext_ref, cv_ref) = refs
    else:
        (x_ref, xp_ref, xn_ref, mod_ref, g_ref, win_ref, bin_ref, dw_ref, dwb_ref,
         lng_ref, lnb_ref, w_ref, b_ref, gffn_ref, wr_ref, xo_ref, m_ref, aff_ref, ext_ref, cv_ref) = refs
    d = win_ref.shape[0]
    i = pl.program_id(1)
    last = pl.num_programs(1) - 1
    x = _stream_rows(x_ref, tm, d)
    x_prev = _stream_rows(xp_ref, CONV_HALO, d)
    x_next = _stream_rows(xn_ref, CONV_HALO, d)
    if has_pe:
        x = x + pe_ref[...]
        x_prev = x_prev + pep_ref[...]
        x_next = x_next + pen_ref[...]

    xe = jnp.concatenate([x_prev, x, x_next], axis=0)
    h = _rms_mod(xe, g_ref[...], mod_ref[0, 1:2, :], mod_ref[0, 0:1, :]).astype(BF16)
    z = jnp.dot(h, win_ref[...], preferred_element_type=F32) + bin_ref[...]
    u = z[:, :d] * jax.nn.sigmoid(z[:, d:])
    rid = lax.broadcasted_iota(I32, (tm + 2 * CONV_HALO, 1), 0)
    lo = jnp.where(i > 0, 0, CONV_HALO)
    hi = jnp.where(i < last, tm + 2 * CONV_HALO, tm + CONV_HALO)
    ext_ref[...] = jnp.where((rid >= lo) & (rid < hi), u, 0.0)

    first_tap = CONV_HALO - CONV_WIDTH // 2

    rows = min(CONV_ROWS, tm)
    for lt in range(d // LANES):
        ls = slice(lt * LANES, (lt + 1) * LANES)
        for c in range(tm // rows):
            base = c * rows
            acc = jnp.zeros((rows, LANES), F32)
            for s in range(SUBLANES):
                part = None
                for o in range(first_tap, first_tap + CONV_WIDTH):
                    if o % SUBLANES != s:
                        continue
                    lo_row = base + o - s
                    term = dw_ref[o - first_tap:o - first_tap + 1, ls] * ext_ref[lo_row:lo_row + rows + SUBLANES, ls]
                    part = term if part is None else part + term
                acc = acc + part[s:s + rows]
            cv_ref[base:base + rows, ls] = acc
    cv = cv_ref[...] + dwb_ref[...]
    mu = jnp.mean(cv, axis=-1, keepdims=True)
    xc = cv - mu
    ln = xc * lax.rsqrt(jnp.mean(xc * xc, axis=-1, keepdims=True) + EPS) * lng_ref[...] + lnb_ref[...]
    y = jnp.dot(_silu(ln).astype(BF16), w_ref[...], preferred_element_type=F32) + b_ref[...]
    _post(x + mod_ref[0, 2:3, :] * y, mod_ref, gffn_ref, wr_ref, xo_ref, m_ref, aff_ref)


def _conv_mixer(x, pe, mod, g_mix, w_in, b_in, dw, dw_b, ln_g, ln_b, w_out, b_out, g_ffn, w_router):
    d = w_in.shape[0]
    bsz, n = _stream_dims(x, d)
    tm = _row_block(n)
    hb = tm // CONV_HALO
    nh = n // CONV_HALO
    prev_blk = lambda i: jnp.maximum(i * hb - 1, 0)
    next_blk = lambda i: jnp.minimum((i + 1) * hb, nh - 1)
    if _is_token_major(x.shape):
        halo = lambda f: pl.BlockSpec((1, CONV_HALO * (d // LANES), LANES), lambda b, i: (b, f(i), 0))
    else:
        halo = lambda f: pl.BlockSpec((1, CONV_HALO, d), lambda b, i: (b, f(i), 0))
    in_specs = [_stream_spec(x, tm, d), halo(prev_blk), halo(next_blk)]
    args = [x, x, x]
    if pe is not None:
        in_specs += [pl.BlockSpec((tm, d), lambda b, i: (i, 0)),
                     pl.BlockSpec((CONV_HALO, d), lambda b, i: (prev_blk(i), 0)),
                     pl.BlockSpec((CONV_HALO, d), lambda b, i: (next_blk(i), 0))]
        args += [pe, pe, pe]
    vec = _full((1, d))
    in_specs += [pl.BlockSpec((1, 6, d), lambda b, i: (b, 0, 0)), vec, _full((d, 2 * d)), _full((1, 2 * d)),
                 _full((CONV_WIDTH, d)), vec, vec, vec, _full((d, d)), vec, vec, _full((N_EXPERTS, d))]
    args += [mod, g_mix.reshape(1, d), w_in.astype(BF16), b_in.reshape(1, 2 * d),
             dw, dw_b.reshape(1, d), ln_g.reshape(1, d), ln_b.reshape(1, d),
             w_out.astype(BF16), b_out.reshape(1, d), g_ffn.reshape(1, d), w_router.T]
    out_shape, out_specs = _post_specs(bsz, n, d, tm)
    return pl.pallas_call(
        functools.partial(_conv_mixer_kernel, has_pe=pe is not None, tm=tm),
        out_shape=out_shape, grid=(bsz, n // tm), in_specs=in_specs, out_specs=out_specs,
        scratch_shapes=[pltpu.VMEM((tm + 2 * CONV_HALO, d), F32), pltpu.VMEM((tm, d), F32)],
        compiler_params=_cparams("parallel", "parallel"), name="conv_mixer",
    )(*args)


GROUPS_PER_TILE = LANES // S5_GROUP
STEPS_PER_TILE = LANES // S5_GROUP


def _s5_in_kernel(x_ref, mod_ref, g_ref, ug_ref, slab_ref, *, tm):
    u = _rms_mod(_stream_rows(x_ref, tm, g_ref.shape[1]), g_ref[...], mod_ref[0, 1:2, :], mod_ref[0, 0:1, :])
    nl = u.shape[1] // LANES
    nc = tm // S5_CHUNK
    for lt in range(nl):
        slab_ref[lt] = u[:, lt * LANES:(lt + 1) * LANES]
    lane_step = lax.broadcasted_iota(I32, (nc, LANES), 1) // S5_GROUP
    for lt in range(nl):
        steps = [slab_ref[lt, pl.ds(t, nc, stride=S5_CHUNK), :] for t in range(S5_CHUNK)]
        for g8 in range(GROUPS_PER_TILE):
            for half in range(S5_CHUNK // STEPS_PER_TILE):
                acc = jnp.zeros((nc, LANES), F32)
                for tq in range(STEPS_PER_TILE):
                    shift = ((tq - g8) * S5_GROUP) % LANES
                    src = steps[half * STEPS_PER_TILE + tq]
                    acc = jnp.where(lane_step == tq, pltpu.roll(src, shift, 1) if shift else src, acc)
                ug_ref[0, lt * GROUPS_PER_TILE + g8, :, half * LANES:(half + 1) * LANES] = acc.astype(BF16)


def _s5_in(x, mod, g_mix):
    d = g_mix.shape[0]
    bsz, n = _stream_dims(x, d)
    tm = _row_block(n)
    groups = d // S5_GROUP
    lw = S5_CHUNK * S5_GROUP
    return pl.pallas_call(
        functools.partial(_s5_in_kernel, tm=tm),
        out_shape=jax.ShapeDtypeStruct((bsz, groups, n // S5_CHUNK, lw), BF16),
        grid=(bsz, n // tm),
        in_specs=[_stream_spec(x, tm, d),
                  pl.BlockSpec((1, 6, d), lambda b, i: (b, 0, 0)), _full((1, d))],
        out_specs=pl.BlockSpec((1, groups, tm // S5_CHUNK, lw), lambda b, i: (b, 0, i, 0)),
        scratch_shapes=[pltpu.VMEM((d // LANES, tm, LANES), F32)],
        compiler_params=_cparams("parallel", "parallel"), name="s5_in",
    )(x, mod, g_mix.reshape(1, d))


def _s5_tables(lam_re, lam_im, log_dt, b_re, b_im, c_re, c_im):
    L = S5_CHUNK
    dt = jnp.exp(log_dt)[:, :, None]
    mag = jnp.exp(lam_re * dt)
    ang = lam_im * dt
    lb_re = mag * jnp.cos(ang)
    lb_im = mag * jnp.sin(ang)
    nr = lb_re - 1.0
    ni = lb_im
    den = lam_re * lam_re + lam_im * lam_im
    coef_re = ((nr * lam_re + ni * lam_im) / den)[..., None]
    coef_im = ((ni * lam_re - nr * lam_im) / den)[..., None]
    bb_re = coef_re * b_re - coef_im * b_im
    bb_im = coef_re * b_im + coef_im * b_re

    def powers(steps):
        st = steps.astype(F32)
        m = jnp.exp((lam_re * dt)[..., None] * st)
        a = (lam_im * dt)[..., None] * st
        return m * jnp.cos(a), m * jnp.sin(a)

    pw_re, pw_im = powers(jnp.arange(L + 1))
    cp_re = c_re[:, :, None] * pw_re.transpose(0, 1, 3, 2)[:, :, :, None, :] \
        - c_im[:, :, None] * pw_im.transpose(0, 1, 3, 2)[:, :, :, None, :]
    cp_im = c_re[:, :, None] * pw_im.transpose(0, 1, 3, 2)[:, :, :, None, :] \
        + c_im[:, :, None] * pw_re.transpose(0, 1, 3, 2)[:, :, :, None, :]
    kern = (jnp.einsum("dgtkp,dgpj->dgtkj", cp_re[:, :, :L], bb_re, precision=HIGHEST)
            - jnp.einsum("dgtkp,dgpj->dgtkj", cp_im[:, :, :L], bb_im, precision=HIGHEST))
    lbb_re = pw_re.transpose(0, 1, 3, 2)[..., None] * bb_re[:, :, None] \
        - pw_im.transpose(0, 1, 3, 2)[..., None] * bb_im[:, :, None]
    lbb_im = pw_re.transpose(0, 1, 3, 2)[..., None] * bb_im[:, :, None] \
        + pw_im.transpose(0, 1, 3, 2)[..., None] * bb_re[:, :, None]

    s_idx = jnp.arange(L)[:, None]
    t_idx = jnp.arange(L)[None, :]
    G = lam_re.shape[1]
    K = S5_GROUP
    P = S5_STATE
    tes, fos = [], []
    for direction in range(2):
        if direction == 0:
            lag = t_idx - s_idx
            e_pow = (L - 1) - jnp.arange(L)
            f_pow = jnp.arange(L) + 1
        else:
            lag = s_idx - t_idx
            e_pow = jnp.arange(L)
            f_pow = L - jnp.arange(L)
        valid = lag >= 0
        kd = kern[direction][:, jnp.clip(lag, 0, L - 1)]
        kd = jnp.where(valid[None, :, :, None, None], kd, 0.0)
        tmat = kd.transpose(0, 1, 4, 2, 3).reshape(G, L * K, L * K)
        e_re = lbb_re[direction][:, e_pow]
        e_im = lbb_im[direction][:, e_pow]
        emat = jnp.concatenate([e_re.transpose(0, 1, 3, 2).reshape(G, L * K, P),
                                e_im.transpose(0, 1, 3, 2).reshape(G, L * K, P)], axis=-1)
        tes.append(jnp.concatenate([tmat, emat], axis=-1))
        f_re = cp_re[direction][:, f_pow]
        f_im = cp_im[direction][:, f_pow]
        fos.append(jnp.concatenate([f_re.transpose(0, 3, 1, 2).reshape(G, P, L * K),
                                    -f_im.transpose(0, 3, 1, 2).reshape(G, P, L * K)], axis=1))
    te = jnp.stack(tes).astype(BF16)
    fo = jnp.stack(fos).astype(BF16)
    a_re, a_im = powers(jnp.full((1,), L))
    a_re, a_im = a_re[..., 0], a_im[..., 0]
    tab = jnp.stack([jnp.concatenate([a_re, a_re], axis=-1),
                     jnp.concatenate([-a_im, a_im], axis=-1)], axis=1)
    return te, fo, tab


S5_GROUPS_PER_STEP = 8


S5_SCAN_UNROLL = 4


def _s5_core_kernel(u_ref, tef_ref, teb_ref, fof_ref, fob_ref, tab_ref, yl_ref, yc_ref,
                    yf_ref, yb_ref, hf_ref, hb_ref, sf_ref, sb_ref, *, nctx, nlat):
    gb = S5_GROUPS_PER_STEP
    rows = nctx + nlat
    lw = S5_CHUNK * S5_GROUP
    for g in range(gb):
        u = u_ref[0, g]
        zf = jnp.dot(u[0:rows], tef_ref[g], preferred_element_type=F32)
        yf_ref[g] = zf[:, :lw]
        hf_ref[pl.ds(g, rows, stride=gb), :] = zf[:, lw:]
        sf_ref[pl.ds(g, rows, stride=gb), :] = pltpu.roll(zf[:, lw:], S5_STATE, 1)
        zb = jnp.dot(u[nctx:nctx + rows], teb_ref[g], preferred_element_type=F32)
        yb_ref[g] = zb[:, :lw]
        hb_ref[pl.ds(g, rows, stride=gb), :] = zb[:, lw:]
        sb_ref[pl.ds(g, rows, stride=gb), :] = pltpu.roll(zb[:, lw:], S5_STATE, 1)

    cat_f, swp_f = tab_ref[0, 0], tab_ref[0, 1]
    cat_b, swp_b = tab_ref[1, 0], tab_ref[1, 1]

    def scan_rows(i, carry):
        hf, sf, hb, sb = carry
        rf = pl.multiple_of(i * gb, gb)
        rb = pl.multiple_of((rows - 1 - i) * gb, gb)
        ef, esf = hf_ref[pl.ds(rf, gb), :], sf_ref[pl.ds(rf, gb), :]
        eb, esb = hb_ref[pl.ds(rb, gb), :], sb_ref[pl.ds(rb, gb), :]
        hf_ref[pl.ds(rf, gb), :] = hf
        hb_ref[pl.ds(rb, gb), :] = hb
        return (cat_f * hf + swp_f * sf + ef, cat_f * sf - swp_f * hf + esf,
                cat_b * hb + swp_b * sb + eb, cat_b * sb - swp_b * hb + esb)

    zero = jnp.zeros((gb, 2 * S5_STATE), F32)
    lax.fori_loop(0, rows, scan_rows, (zero, zero, zero, zero), unroll=S5_SCAN_UNROLL)

    for g in range(gb):
        hin_f = hf_ref[pl.ds(g, rows, stride=gb), :].astype(BF16)
        hin_b = hb_ref[pl.ds(g, rows, stride=gb), :].astype(BF16)
        yf = yf_ref[g] + jnp.dot(hin_f, fof_ref[g], preferred_element_type=F32)
        yb = yb_ref[g] + jnp.dot(hin_b, fob_ref[g], preferred_element_type=F32)
        yl_ref[0, g] = yf[nctx:nctx + nlat] + yb[0:nlat]
        yc_ref[0, g] = yf[0:nctx] + yb[nlat:nlat + nctx]


def _s5_core(u_all, te, fo, tab, nctx, nlat):
    bsz, G, rtot, lw = u_all.shape
    gb = S5_GROUPS_PER_STEP
    rows = nctx + nlat
    w = 2 * S5_STATE
    gspec3 = lambda shape: pl.BlockSpec((gb,) + shape, lambda b, j: (j, 0, 0))
    return pl.pallas_call(
        functools.partial(_s5_core_kernel, nctx=nctx, nlat=nlat),
        out_shape=(jax.ShapeDtypeStruct((bsz, G, nlat, lw), F32),
                   jax.ShapeDtypeStruct((bsz, G, nctx, lw), F32)),
        grid=(bsz, G // gb),
        in_specs=[pl.BlockSpec((1, gb, rtot, lw), lambda b, j: (b, j, 0, 0)),
                  gspec3((lw, lw + w)), gspec3((lw, lw + w)), gspec3((w, lw)), gspec3((w, lw)),
                  pl.BlockSpec((2, 2, gb, w), lambda b, j: (0, 0, j, 0))],
        out_specs=(pl.BlockSpec((1, gb, nlat, lw), lambda b, j: (b, j, 0, 0)),
                   pl.BlockSpec((1, gb, nctx, lw), lambda b, j: (b, j, 0, 0))),
        scratch_shapes=[pltpu.VMEM((gb, rows, lw), F32), pltpu.VMEM((gb, rows, lw), F32)]
        + [pltpu.VMEM((rows * gb, w), F32)] * 4,
        compiler_params=_cparams("parallel", "parallel"), name="s5_core",
    )(u_all, te[0], te[1], fo[0], fo[1], tab)


def _s5_out_kernel(yg_ref, x_ref, mod_ref, gmix_ref, dsk_ref, w_ref, b_ref, gffn_ref, wr_ref,
                   xo_ref, m_ref, aff_ref, slab_ref, *, tm):
    nl = gmix_ref.shape[1] // LANES
    nc = tm // S5_CHUNK
    lane_group = lax.broadcasted_iota(I32, (nc, LANES), 1) // S5_GROUP
    for lt in range(nl):
        for t in range(S5_CHUNK):
            half, tq = divmod(t, STEPS_PER_TILE)
            acc = jnp.zeros((nc, LANES), F32)
            for g8 in range(GROUPS_PER_TILE):
                src = yg_ref[0, lt * GROUPS_PER_TILE + g8, :, half * LANES:(half + 1) * LANES]
                shift = ((g8 - tq) * S5_GROUP) % LANES
                acc = jnp.where(lane_group == g8, pltpu.roll(src, shift, 1) if shift else src, acc)
            slab_ref[lt, pl.ds(t, nc, stride=S5_CHUNK), :] = acc
    ssm = jnp.concatenate([slab_ref[lt] for lt in range(nl)], axis=1)
    x = _stream_rows(x_ref, tm, gmix_ref.shape[1])
    u = _rms_mod(x, gmix_ref[...], mod_ref[0, 1:2, :], mod_ref[0, 0:1, :])
    y = ssm + dsk_ref[...] * u
    z = jnp.dot(jax.nn.gelu(y).astype(BF16), w_ref[...], preferred_element_type=F32) + b_ref[...]
    d = z.shape[1] // 2
    out = z[:, :d] * jax.nn.sigmoid(z[:, d:])
    _post(x + mod_ref[0, 2:3, :] * out, mod_ref, gffn_ref, wr_ref, xo_ref, m_ref, aff_ref)


def _s5_out(yg, x, mod, g_mix, d_skip, w_glu, b_glu, g_ffn, w_router):
    d = g_mix.shape[0]
    bsz, n = _stream_dims(x, d)
    tm = _row_block(n)
    groups = d // S5_GROUP
    lw = S5_CHUNK * S5_GROUP
    row = pl.BlockSpec((1, tm, d), lambda b, i: (b, i, 0))
    vec = _full((1, d))
    out_shape, out_specs = _post_specs(bsz, n, d, tm)
    return pl.pallas_call(
        functools.partial(_s5_out_kernel, tm=tm), out_shape=out_shape, grid=(bsz, n // tm),
        in_specs=[pl.BlockSpec((1, groups, tm // S5_CHUNK, lw), lambda b, i: (b, 0, i, 0)),
                  _stream_spec(x, tm, d), pl.BlockSpec((1, 6, d), lambda b, i: (b, 0, 0)), vec, vec,
                  _full((d, 2 * d)), _full((1, 2 * d)), vec, _full((N_EXPERTS, d))],
        out_specs=out_specs,
        scratch_shapes=[pltpu.VMEM((d // LANES, tm, LANES), F32)],
        compiler_params=_cparams("parallel", "parallel"), name="s5_out",
    )(yg, x, mod, g_mix.reshape(1, d), d_skip.reshape(1, d), w_glu.astype(BF16), b_glu.reshape(1, 2 * d),
      g_ffn.reshape(1, d), w_router.T)


def _lru_in_kernel(x_ref, mod_ref, g_ref, wx_ref, bx_ref, wy_ref, by_ref, xp_ref, gate_ref):
    x = _stream_rows(x_ref, xp_ref.shape[1], xp_ref.shape[2])
    h = _rms_mod(x, g_ref[...], mod_ref[0, 1:2, :], mod_ref[0, 0:1, :]).astype(BF16)
    xp_ref[0] = jnp.dot(h, wx_ref[...], preferred_element_type=F32) + bx_ref[...]
    gate_ref[0] = jax.nn.gelu(jnp.dot(h, wy_ref[...], preferred_element_type=F32) + by_ref[...])


def _lru_in(x, mod, g_mix, w_x, b_x, w_y, b_y):
    d = g_mix.shape[0]
    bsz, n = _stream_dims(x, d)
    tm = _row_block(n)
    row = pl.BlockSpec((1, tm, d), lambda b, i: (b, i, 0))
    vec = _full((1, d))
    return pl.pallas_call(
        _lru_in_kernel,
        out_shape=(jax.ShapeDtypeStruct((bsz, n, d), F32), jax.ShapeDtypeStruct((bsz, n, d), F32)),
        grid=(bsz, n // tm),
        in_specs=[_stream_spec(x, tm, d), pl.BlockSpec((1, 6, d), lambda b, i: (b, 0, 0)), vec,
                  _full((d, d)), vec, _full((d, d)), vec],
        out_specs=(row, row), compiler_params=_cparams("parallel", "parallel"), name="lru_in",
    )(x, mod, g_mix.reshape(1, d), w_x.astype(BF16), b_x.reshape(1, d), w_y.astype(BF16), b_y.reshape(1, d))


LRU_HALO = 8
LRU_SCAN_UNROLL = 8


def _lru_scan_kernel(xf_ref, xfp_ref, xfn_ref, xb_ref, xbp_ref, xbn_ref, h0_ref, cw_ref, cb_ref,
                     wai_ref, bai_ref, lam_ref, rf_ref, rb_ref, hfin_ref,
                     ext_ref, xl_ref, a_ref, b_ref, r_ref, h_ref, *, tm):
    i = pl.program_id(1)
    last = pl.num_programs(1) - 1
    hw = LANES
    d = LRU_HEADS * hw

    @pl.when(i == 0)
    def _():
        h_ref[...] = h0_ref[0]

    lam = lam_ref[...]
    nl = -lam
    softplus = jnp.maximum(nl, 0.0) + jnp.log1p(jnp.exp(-jnp.abs(nl)))
    c8 = -LRU_C * softplus

    def sigmoid(v):
        return 0.5 * jnp.tanh(0.5 * v) + 0.5

    def gates(dr, main_ref, prev_ref, next_ref, blk):
        _store_token_major(ext_ref, (), jnp.where(blk > 0, prev_ref[0], 0.0))
        _store_token_major(ext_ref, (), main_ref[0], row0=LRU_HALO)
        _store_token_major(ext_ref, (), jnp.where(blk < last, next_ref[0], 0.0), row0=LRU_HALO + tm)
        xl3 = jnp.broadcast_to(cb_ref[...], (tm, LRU_HEADS, hw))
        for k in range(LRU_CONV):
            off = (LRU_HALO - LRU_CONV // 2 + k) * LRU_HEADS
            xl3 = xl3 + cw_ref[k:k + 1] * ext_ref[off:off + tm * LRU_HEADS, :].reshape(tm, LRU_HEADS, hw)
        xl_ref[...] = xl3.reshape(tm * LRU_HEADS, hw)
        for hd in range(LRU_HEADS):
            sl = slice(hd * hw, (hd + 1) * hw)
            xl = xl_ref[pl.ds(hd, tm, stride=LRU_HEADS), :]
            z = jnp.dot(xl.astype(BF16), wai_ref[dr, hd], preferred_element_type=F32)
            r = sigmoid(z[:, :hw] + bai_ref[dr, 0:1, sl])
            ig = sigmoid(z[:, hw:] + bai_ref[dr, 1:2, sl])
            log_a = c8[dr:dr + 1, sl] * r
            th = jnp.tanh(log_a)
            a_ref[dr, pl.ds(hd, tm, stride=LRU_HEADS), :] = jnp.exp(log_a)
            b_ref[dr, pl.ds(hd, tm, stride=LRU_HEADS), :] = jnp.sqrt(-2.0 * th / (1.0 - th)) * (ig * xl)

    gates(0, xf_ref, xfp_ref, xfn_ref, i)
    gates(1, xb_ref, xbp_ref, xbn_ref, last - i)

    def step(t, carry):
        hf, hb = carry
        rf = pl.multiple_of(t * LRU_HEADS, LRU_HEADS)
        rb = pl.multiple_of((tm - 1 - t) * LRU_HEADS, LRU_HEADS)
        hf = a_ref[0, pl.ds(rf, LRU_HEADS), :] * hf + b_ref[0, pl.ds(rf, LRU_HEADS), :]
        hb = a_ref[1, pl.ds(rb, LRU_HEADS), :] * hb + b_ref[1, pl.ds(rb, LRU_HEADS), :]
        r_ref[0, pl.ds(rf, LRU_HEADS), :] = hf
        r_ref[1, pl.ds(rb, LRU_HEADS), :] = hb
        return hf, hb

    hf, hb = lax.fori_loop(0, tm, step, (h_ref[0], h_ref[1]), unroll=LRU_SCAN_UNROLL)
    h_ref[0] = hf
    h_ref[1] = hb
    hfin_ref[0] = h_ref[...]
    rf_ref[0] = _load_token_major(r_ref, (0,), tm, d)
    rb_ref[0] = _load_token_major(r_ref, (1,), tm, d)


def _lru_scan(xp, h0, conv_w, conv_b, w_a, b_a, w_i, b_i, lam):
    bsz, n, d = xp.shape
    tm = _row_block(n)
    nb = n // tm
    hb = tm // LRU_HALO
    nh = n // LRU_HALO
    fwd = lambda b, i: (b, i, 0)
    bwd = lambda b, i: (b, nb - 1 - i, 0)
    halo = lambda f: pl.BlockSpec((1, LRU_HALO, d), f)
    wai = jnp.concatenate([w_a, w_i], axis=-1).astype(BF16)
    bai = jnp.stack([b_a, b_i], axis=1)
    assert d // LRU_HEADS == LANES, "token-major scan assumes one 128-lane tile per head"
    state = pl.BlockSpec((1, 2, LRU_HEADS, LANES), lambda b, i: (b, 0, 0, 0))
    return pl.pallas_call(
        functools.partial(_lru_scan_kernel, tm=tm),
        out_shape=(jax.ShapeDtypeStruct((bsz, n, d), F32), jax.ShapeDtypeStruct((bsz, n, d), F32),
                   jax.ShapeDtypeStruct((bsz, 2, LRU_HEADS, LANES), F32)),
        grid=(bsz, nb),
        in_specs=[pl.BlockSpec((1, tm, d), fwd),
                  halo(lambda b, i: (b, jnp.maximum(i * hb - 1, 0), 0)),
                  halo(lambda b, i: (b, jnp.minimum((i + 1) * hb, nh - 1), 0)),
                  pl.BlockSpec((1, tm, d), bwd),
                  halo(lambda b, i: (b, jnp.maximum((nb - 1 - i) * hb - 1, 0), 0)),
                  halo(lambda b, i: (b, jnp.minimum((nb - i) * hb, nh - 1), 0)),
                  state,
                  _full((LRU_CONV, LRU_HEADS, LANES)), _full((1, LRU_HEADS, LANES)),
                  _full(wai.shape), _full(bai.shape), _full((2, d))],
        out_specs=(pl.BlockSpec((1, tm, d), fwd), pl.BlockSpec((1, tm, d), bwd), state),
        scratch_shapes=[pltpu.VMEM(((tm + 2 * LRU_HALO) * LRU_HEADS, LANES), F32),
                        pltpu.VMEM((tm * LRU_HEADS, LANES), F32)]
        + [pltpu.VMEM((2, tm * LRU_HEADS, LANES), F32)] * 3
        + [pltpu.VMEM((2, LRU_HEADS, LANES), F32)],
        compiler_params=_cparams("parallel", "arbitrary"), name="lru_scan",
    )(xp, xp, xp, xp, xp, xp, h0, conv_w.reshape(LRU_CONV, LRU_HEADS, LANES), conv_b.reshape(1, LRU_HEADS, LANES),
      wai, bai, lam)


def _lru_out_kernel(rf_ref, rb_ref, gate_ref, x_ref, mod_ref, w_ref, b_ref, gffn_ref, wr_ref,
                    xo_ref, m_ref, aff_ref):
    r = (rf_ref[0] + rb_ref[0]) * gate_ref[0]
    y = jnp.dot(r.astype(BF16), w_ref[...], preferred_element_type=F32) + b_ref[...]
    x = _stream_rows(x_ref, rf_ref.shape[1], rf_ref.shape[2])
    _post(x + mod_ref[0, 2:3, :] * y, mod_ref, gffn_ref, wr_ref, xo_ref, m_ref, aff_ref)


def _lru_out(rf, rb, gate, x, mod, w_out, b_out, g_ffn, w_router):
    bsz, n, d = rf.shape
    tm = _row_block(n)
    row = pl.BlockSpec((1, tm, d), lambda b, i: (b, i, 0))
    vec = _full((1, d))
    out_shape, out_specs = _post_specs(bsz, n, d, tm)
    return pl.pallas_call(
        _lru_out_kernel, out_shape=out_shape, grid=(bsz, n // tm),
        in_specs=[row, row, row, _stream_spec(x, tm, d), pl.BlockSpec((1, 6, d), lambda b, i: (b, 0, 0)),
                  _full((d, d)), vec, vec, _full((N_EXPERTS, d))],
        out_specs=out_specs, compiler_params=_cparams("parallel", "parallel"), name="lru_out",
    )(rf, rb, gate, x, mod, w_out.astype(BF16), b_out.reshape(1, d), g_ffn.reshape(1, d), w_router.T)


INF_BITS = 0x7F800000
SLOT_UNROLL = 32


def _tile_cumsum(mask, tri):
    return jnp.dot(mask.astype(BF16), tri, preferred_element_type=F32)


def _select_kernel(aff_ref, lpos_ref, off_ref, sel_ref, *, n, cap):
    nt = n // LANES
    bits = pltpu.bitcast(aff_ref[0], I32)
    capf = float(cap)

    def bisect(_, lohi):
        lo, hi = lohi
        mid = lo + ((hi - lo + 1) >> 1)
        cnt = jnp.sum(jnp.where(bits >= mid, 1.0, 0.0), axis=1, keepdims=True)
        ok = cnt >= capf
        return jnp.where(ok, mid, lo), jnp.where(ok, hi, mid - 1)

    e = bits.shape[0]
    thr, _ = lax.fori_loop(0, 31, bisect, (jnp.zeros((e, 1), I32), jnp.full((e, 1), INF_BITS, I32)))
    gt = jnp.where(bits > thr, 1.0, 0.0)
    eq = jnp.where(bits == thr, 1.0, 0.0)
    need = capf - jnp.sum(gt, axis=1, keepdims=True)

    r_i = lax.broadcasted_iota(I32, (LANES, LANES), 0)
    c_i = lax.broadcasted_iota(I32, (LANES, LANES), 1)
    tri = jnp.where(r_i <= c_i, 1.0, 0.0).astype(BF16)
    sup = jnp.where(r_i < c_i, 1.0, 0.0).astype(BF16)
    t_i = lax.broadcasted_iota(I32, (n, LANES), 0)
    k_i = lax.broadcasted_iota(I32, (n, LANES), 1)
    tile_of = jnp.where((t_i >> 7) == k_i, 1.0, 0.0).astype(BF16)

    def tile_offsets(mask):
        counts = jnp.dot(mask.astype(BF16), tile_of, preferred_element_type=F32)
        return jnp.dot(counts.astype(BF16), sup, preferred_element_type=F32)

    eq_off = tile_offsets(eq)
    for k in range(nt):
        sl = slice(k * LANES, (k + 1) * LANES)
        eqk = eq[:, sl]
        rank = eq_off[:, k:k + 1] + _tile_cumsum(eqk, tri) - eqk
        sel_ref[:, sl] = gt[:, sl] + eqk * jnp.where(rank < need, 1.0, 0.0)
    sel = sel_ref[...]
    off_ref[0] = tile_offsets(sel).astype(I32)
    for k in range(nt):
        sl = slice(k * LANES, (k + 1) * LANES)
        sk = sel[:, sl]
        lpos_ref[0, :, sl] = jnp.where(sk > 0.0, _tile_cumsum(sk, tri) - 1.0, -1.0).astype(I32)


def _select(aff_t, cap):
    bsz, e, n = aff_t.shape
    return pl.pallas_call(
        functools.partial(_select_kernel, n=n, cap=cap),
        out_shape=(jax.ShapeDtypeStruct((bsz, e, n), I32), jax.ShapeDtypeStruct((bsz, e, LANES), I32)),
        grid=(bsz,),
        in_specs=[pl.BlockSpec((1, e, n), lambda b: (b, 0, 0))],
        out_specs=(pl.BlockSpec((1, e, n), lambda b: (b, 0, 0)),
                   pl.BlockSpec((1, e, LANES), lambda b: (b, 0, 0))),
        scratch_shapes=[pltpu.VMEM((e, n), F32)],
        compiler_params=_cparams("parallel"), name="moe_select",
    )(aff_t)


def _slots_kernel(off_ref, lpos_ref, aff_ref, out_ref, scr_ref, *, nt, cap):
    b = pl.program_id(0)
    e = pl.program_id(1)
    base = (b * pl.num_programs(1) + e) * LANES
    j_i = lax.broadcasted_iota(I32, (LANES, LANES), 0)
    row = lax.broadcasted_iota(I32, (LANES, LANES), 0)
    lane = lax.broadcasted_iota(I32, (LANES, LANES), 1)

    def tile(k, carry):
        lp = lpos_ref[0, 0, k]
        av = aff_ref[0, 0, k]
        onehot = jnp.where(lp == j_i, 1.0, 0.0).astype(BF16)
        a1 = av.astype(BF16).astype(F32)
        r1 = av - a1
        a2 = r1.astype(BF16).astype(F32)
        a3 = r1 - a2
        q = jnp.where(row == 0, lane.astype(F32),
                      jnp.where(row == 1, a1, jnp.where(row == 2, a2, jnp.where(row == 3, a3, 0.0))))
        res = lax.dot_general(onehot, q.astype(BF16), (((1,), (1,)), ((), ())),
                              preferred_element_type=F32)
        res = res + jnp.where(lane == 0, jnp.asarray(k * LANES, F32), 0.0)
        scr_ref[pl.ds(off_ref[base + k], LANES), :] = res
        return carry

    lax.fori_loop(0, nt, tile, 0, unroll=min(SLOT_UNROLL, nt))
    out_ref[0, 0] = scr_ref[0:cap, :]


def _slots(lpos, tile_off, aff_t, cap):
    bsz, e, n = lpos.shape
    nt = n // LANES
    lp5 = lpos.reshape(bsz, e, nt, 1, LANES)
    af5 = aff_t.reshape(bsz, e, nt, 1, LANES)
    blk = pl.BlockSpec((1, 1, nt, 1, LANES), lambda b, j, off: (b, j, 0, 0, 0))
    return pl.pallas_call(
        functools.partial(_slots_kernel, nt=nt, cap=cap),
        out_shape=jax.ShapeDtypeStruct((bsz, e, cap, LANES), F32),
        grid_spec=pltpu.PrefetchScalarGridSpec(
            num_scalar_prefetch=1, grid=(bsz, e), in_specs=[blk, blk],
            out_specs=pl.BlockSpec((1, 1, cap, LANES), lambda b, j, off: (b, j, 0, 0)),
            scratch_shapes=[pltpu.VMEM((cap + LANES, LANES), F32)]),
        compiler_params=_cparams("parallel", "parallel"), name="moe_slots",
    )(tile_off.reshape(-1), lp5, af5)


ROW_UNROLL = 8


def _gather_kernel(idx_ref, m_ref, xs_ref, scr_ref, *, cap, nl):
    b = pl.program_id(0)
    e = pl.program_id(1)
    base = (b * pl.num_programs(1) + e) * cap

    def rows(ju, carry):
        j0 = ju * ROW_UNROLL
        for r in range(ROW_UNROLL):
            t = idx_ref[base + j0 + r]
            scr_ref[pl.ds(pl.multiple_of((j0 + r) * nl, nl), nl), :] = \
                m_ref[0, pl.ds(pl.multiple_of(t * nl, nl), nl), :]
        return carry

    lax.fori_loop(0, cap // ROW_UNROLL, rows, 0)
    xs_ref[0, 0] = _load_token_major(scr_ref, (), cap, nl * LANES).astype(BF16)


def _gather(idx_flat, m_tok, e, cap):
    bsz, rows_, _ = m_tok.shape
    nl = SUBLANES
    d = nl * LANES
    return pl.pallas_call(
        functools.partial(_gather_kernel, cap=cap, nl=nl),
        out_shape=jax.ShapeDtypeStruct((bsz, e, cap, d), BF16),
        grid_spec=pltpu.PrefetchScalarGridSpec(
            num_scalar_prefetch=1, grid=(bsz, e),
            in_specs=[pl.BlockSpec((1, rows_, LANES), lambda b, j, idx: (b, 0, 0),
                                   pipeline_mode=pl.Buffered(1))],
            out_specs=pl.BlockSpec((1, 1, cap, d), lambda b, j, idx: (b, j, 0, 0)),
            scratch_shapes=[pltpu.VMEM((cap * nl, LANES), F32)]),
        compiler_params=_cparams("parallel", "arbitrary"), name="moe_gather",
    )(idx_flat, m_tok)


FFN_CHUNK = 512


def _ffn_rows(xs, gs, w1b_ref, w3b_ref, w2b_ref):
    de = w1b_ref.shape[1]
    fc = min(FFN_CHUNK, de)
    acc = None
    for c in range(de // fc):
        sl = slice(c * fc, (c + 1) * fc)
        h1 = jnp.dot(xs, w1b_ref[:, sl], preferred_element_type=F32)
        h3 = jnp.dot(xs, w3b_ref[:, sl], preferred_element_type=F32)
        h = (_silu(h1) * h3).astype(BF16)
        part = jnp.dot(h, w2b_ref[sl, :], preferred_element_type=F32)
        acc = part if acc is None else acc + part
    gate = gs[:, 1:2] + gs[:, 2:3] + gs[:, 3:4]
    return acc * gate


def _ffn_kernel(*refs, n_streams, batched):
    ins = refs[:2 * n_streams]
    w1_ref, w3_ref, w2_ref = refs[2 * n_streams:2 * n_streams + 3]
    outs = refs[2 * n_streams + 3:3 * n_streams + 3]
    w1b_ref, w3b_ref, w2b_ref = refs[3 * n_streams + 3:]

    @pl.when(pl.program_id(1) == 0)
    def _():
        w1b_ref[...] = w1_ref[0, 0].astype(BF16)
        w3b_ref[...] = w3_ref[0, 0].astype(BF16)
        w2b_ref[...] = w2_ref[0, 0].astype(BF16)

    for k in range(n_streams):
        xs_ref, gs_ref, y_ref = ins[2 * k], ins[2 * k + 1], outs[k]
        if batched[k]:
            @pl.when(pl.program_id(1) == 0)
            def _(xs_ref=xs_ref, gs_ref=gs_ref, y_ref=y_ref):
                nb, _, cap, d = xs_ref.shape
                y = _ffn_rows(xs_ref[:, 0].reshape(nb * cap, d), gs_ref[:, 0].reshape(nb * cap, LANES),
                              w1b_ref, w3b_ref, w2b_ref)
                for s in range(nb):
                    _store_token_major(y_ref, (s, 0), y[s * cap:(s + 1) * cap])
        else:
            _store_token_major(y_ref, (0, 0), _ffn_rows(xs_ref[0, 0], gs_ref[0, 0], w1b_ref, w3b_ref, w2b_ref))


FFN_BATCH_ROWS = 128


def _ffn(streams, experts, layer):
    w1, w3, w2 = experts
    bsz, e, _, d = streams[0][0].shape
    de = w1.shape[3]
    nl = d // LANES
    in_specs, args, out_shape, out_specs, batched = [], [], [], [], []
    for xs, slots in streams:
        cap = xs.shape[2]
        whole = cap < FFN_BATCH_ROWS
        batched.append(whole)
        nb = bsz if whole else 1
        sample = (lambda j, b: (0, j, 0, 0)) if whole else (lambda j, b: (b, j, 0, 0))
        in_specs += [pl.BlockSpec((nb, 1, cap, d), sample), pl.BlockSpec((nb, 1, cap, LANES), sample)]
        args += [xs, slots]
        out_shape.append(jax.ShapeDtypeStruct((bsz, e, cap * nl, LANES), F32))
        out_specs.append(pl.BlockSpec((nb, 1, cap * nl, LANES), sample))
    in_specs += [pl.BlockSpec((1, 1, d, de), lambda j, b: (layer, j, 0, 0)),
                 pl.BlockSpec((1, 1, d, de), lambda j, b: (layer, j, 0, 0)),
                 pl.BlockSpec((1, 1, de, d), lambda j, b: (layer, j, 0, 0))]
    return pl.pallas_call(
        functools.partial(_ffn_kernel, n_streams=len(streams), batched=tuple(batched)),
        out_shape=tuple(out_shape),
        grid=(e, bsz), in_specs=in_specs, out_specs=tuple(out_specs),
        scratch_shapes=[pltpu.VMEM((d, de), BF16), pltpu.VMEM((d, de), BF16), pltpu.VMEM((de, d), BF16)],
        compiler_params=_cparams("parallel", "arbitrary"), name="moe_ffn",
    )(*args, w1, w3, w2)


COMBINE_UNROLL = 4


def _combine_kernel(idx_ref, y_ref, g2_ref, xin_hbm, acc_ref, sem, *, cap, nl):
    b = pl.program_id(0)
    e = pl.program_id(1)
    base = (b * pl.num_programs(1) + e) * cap

    @pl.when(e == 0)
    def _():
        load = pltpu.make_async_copy(xin_hbm.at[b], acc_ref.at[0], sem)
        load.start()
        load.wait()

    g2 = g2_ref[0]

    def rows(ju, carry):
        j0 = ju * COMBINE_UNROLL
        ts = [pl.multiple_of(idx_ref[base + j0 + r] * nl, nl) for r in range(COMBINE_UNROLL)]
        vals = [acc_ref[0, pl.ds(ts[r], nl), :]
                + g2 * y_ref[0, 0, pl.ds(pl.multiple_of((j0 + r) * nl, nl), nl), :]
                for r in range(COMBINE_UNROLL)]
        for r in range(COMBINE_UNROLL):
            acc_ref[0, pl.ds(ts[r], nl), :] = vals[r]
        return carry

    lax.fori_loop(0, cap // COMBINE_UNROLL, rows, 0)


def _combine(idx_flat, y_tok, x_tok, gate2, cap):
    bsz, e, rows_, _ = y_tok.shape
    nl = rows_ // cap
    return pl.pallas_call(
        functools.partial(_combine_kernel, cap=cap, nl=nl),
        out_shape=jax.ShapeDtypeStruct(x_tok.shape, F32),
        grid_spec=pltpu.PrefetchScalarGridSpec(
            num_scalar_prefetch=1, grid=(bsz, e),
            in_specs=[pl.BlockSpec((1, 1, rows_, LANES), lambda b, j, idx: (b, j, 0, 0)),
                      pl.BlockSpec((1, nl, LANES), lambda b, j, idx: (b, 0, 0)),
                      pl.BlockSpec(memory_space=pl.ANY)],
            out_specs=pl.BlockSpec((1,) + x_tok.shape[1:], lambda b, j, idx: (b, 0, 0),
                                   pipeline_mode=pl.Buffered(1)),
            scratch_shapes=[pltpu.SemaphoreType.DMA(())]),
        compiler_params=_cparams("parallel", "arbitrary"), name="moe_combine",
    )(idx_flat, y_tok, gate2.reshape(bsz, nl, LANES), x_tok)


def _final_norm_kernel(x_ref, g_ref, o_ref):
    x = _stream_rows(x_ref, o_ref.shape[1], o_ref.shape[2])
    o_ref[0] = x * lax.rsqrt(jnp.mean(x * x, axis=-1, keepdims=True) + EPS) * g_ref[...]


def _final_norm(x, final_g):
    d = final_g.shape[0]
    bsz, n = _stream_dims(x, d)
    tm = _row_block(n)
    return pl.pallas_call(
        _final_norm_kernel, out_shape=jax.ShapeDtypeStruct((bsz, n, d), F32), grid=(bsz, n // tm),
        in_specs=[_stream_spec(x, tm, d), _full((1, d))],
        out_specs=pl.BlockSpec((1, tm, d), lambda b, i: (b, i, 0)),
        compiler_params=_cparams("parallel", "parallel"), name="final_norm",
    )(x, final_g.reshape(1, d))


def _moe(streams, experts, layer):
    routed = []
    for x_tok, m_tok, aff_t, mod in streams:
        e, n = aff_t.shape[1], aff_t.shape[2]
        cap = EC_CAPACITY * n // e
        lpos, tile_off = _select(aff_t, cap)
        slots = _slots(lpos, tile_off, aff_t, cap)
        idx_flat = slots[..., 0].astype(I32).reshape(-1)
        routed.append((idx_flat, _gather(idx_flat, m_tok, e, cap), slots, cap))
    ys = _ffn([(xs, slots) for _, xs, slots, _ in routed], experts, layer)
    return [_combine(idx_flat, y_tok, x_tok, mod[:, 5], cap)
            for (idx_flat, _, _, cap), y_tok, (x_tok, _, _, mod) in zip(routed, ys, streams)]


def _pos_embed(n, d):
    rows = n // GRID_W
    quarter = d // 4
    omega = 1.0 / (POS_BASE ** (jnp.arange(quarter, dtype=F32) / quarter))
    r = jnp.arange(rows, dtype=F32)[:, None] * omega
    cc = jnp.arange(GRID_W, dtype=F32)[:, None] * omega
    row_emb = jnp.concatenate([jnp.sin(r), jnp.cos(r)], axis=-1)
    col_emb = jnp.concatenate([jnp.sin(cc), jnp.cos(cc)], axis=-1)
    emb = jnp.concatenate([
        jnp.broadcast_to(row_emb[:, None, :], (rows, GRID_W, d // 2)),
        jnp.broadcast_to(col_emb[None, :, :], (rows, GRID_W, d // 2))], axis=-1)
    return emb.reshape(rows * GRID_W, d)


def kernel(x, c, ctx, c_ctx, w_mod, b_mod, g_mix, g_ffn, conv_w_in, conv_b_in, conv_dw, conv_dw_b, conv_ln_g, conv_ln_b, conv_w_out, conv_b_out, s5_lam_re, s5_lam_im, s5_log_dt, s5_b_re, s5_b_im, s5_c_re, s5_c_im, s5_d, s5_w_glu, s5_b_glu, lru_w_y, lru_b_y, lru_w_x, lru_b_x, lru_conv_w, lru_conv_b, lru_w_a, lru_b_a, lru_w_i, lru_b_i, lru_lam, lru_w_out, lru_b_out, moe_router, moe_w1, moe_w3, moe_w2, final_g):
    bsz, n, d = x.shape
    nctx = ctx.shape[1]
    depth = w_mod.shape[0]
    n_mixers = 3
    reader_layers = [i for i in range(depth) if i % n_mixers != 0]
    last_reader = max(reader_layers) if reader_layers else -1

    rows = -(-(bsz + 1) // SUBLANES) * SUBLANES
    cond = jnp.zeros((rows, d), F32).at[:bsz].set(c).at[bsz].set(c_ctx)
    mods = _modulation(cond, w_mod, b_mod).reshape(depth, rows, 6, d)
    pe = _pos_embed(n, d)
    assert d == SUBLANES * LANES, "token-major MoE rows assume one (8, 128) tile per token"
    experts = (moe_w1, moe_w3, moe_w2)

    x_lat, x_ctx = x, ctx
    for i in range(depth):
        kind, j = i % n_mixers, i // n_mixers
        ctx_in = i <= last_reader
        ctx_out = i < last_reader
        mod_lat = mods[i, :bsz]
        mod_ctx = jnp.broadcast_to(mods[i, bsz][None], (bsz, 6, d))
        lat_pe = pe if i == 0 else None
        streams = [(x_lat, mod_lat, lat_pe)]
        if ctx_out:
            streams.append((x_ctx, mod_ctx, None))

        if kind == 0:
            post = []
            for xs_, mod_, pe_ in streams:
                post.append(_conv_mixer(xs_, pe_, mod_, g_mix[i], conv_w_in[j], conv_b_in[j], conv_dw[j],
                                        conv_dw_b[j], conv_ln_g[j], conv_ln_b[j], conv_w_out[j], conv_b_out[j],
                                        g_ffn[i], moe_router[i]))
        elif kind == 1:
            te, fo, tab = _s5_tables(s5_lam_re[j], s5_lam_im[j], s5_log_dt[j], s5_b_re[j], s5_b_im[j],
                                     s5_c_re[j], s5_c_im[j])
            uc = _s5_in(x_ctx, mod_ctx, g_mix[i])
            u_all = jnp.concatenate([uc, _s5_in(x_lat, mod_lat, g_mix[i]), uc], axis=2)
            y_lat, y_ctx = _s5_core(u_all, te, fo, tab, nctx // S5_CHUNK, n // S5_CHUNK)
            post = [_s5_out(y_lat, x_lat, mod_lat, g_mix[i], s5_d[j], s5_w_glu[j], s5_b_glu[j],
                            g_ffn[i], moe_router[i])]
            if ctx_out:
                post.append(_s5_out(y_ctx, x_ctx, mod_ctx, g_mix[i], s5_d[j], s5_w_glu[j],
                                    s5_b_glu[j], g_ffn[i], moe_router[i]))
        else:
            scan_args = (lru_conv_w[j], lru_conv_b[j], lru_w_a[j], lru_b_a[j], lru_w_i[j], lru_b_i[j], lru_lam[j])
            xp_ctx, gate_ctx = _lru_in(x_ctx, mod_ctx, g_mix[i], lru_w_x[j], lru_b_x[j], lru_w_y[j], lru_b_y[j])
            rf_c, rb_c, h_ctx = _lru_scan(xp_ctx, jnp.zeros((bsz, 2, LRU_HEADS, d // LRU_HEADS), F32), *scan_args)
            xp_lat, gate_lat = _lru_in(x_lat, mod_lat, g_mix[i], lru_w_x[j], lru_b_x[j], lru_w_y[j], lru_b_y[j])
            rf_l, rb_l, _ = _lru_scan(xp_lat, h_ctx, *scan_args)
            post = [_lru_out(rf_l, rb_l, gate_lat, x_lat, mod_lat, lru_w_out[j], lru_b_out[j],
                             g_ffn[i], moe_router[i])]
            if ctx_out:
                post.append(_lru_out(rf_c, rb_c, gate_ctx, x_ctx, mod_ctx, lru_w_out[j], lru_b_out[j],
                                     g_ffn[i], moe_router[i]))

        mods_ = [mod_lat, mod_ctx]
        outs = _moe([(xn, m, aff, mods_[k]) for k, (xn, m, aff) in enumerate(post)], experts, i)
        x_lat = outs[0]
        if ctx_out:
            x_ctx = outs[1]
    return _final_norm(x_lat, final_g)
```

```python
import functools
import math

import jax
import jax.numpy as jnp
from jax import lax
from jax.experimental import pallas as pl
from jax.experimental.pallas import tpu as pltpu

F32 = jnp.float32
BF16 = jnp.bfloat16
I32 = jnp.int32
HIGHEST = lax.Precision.HIGHEST

EPS = 1e-6
N_EXPERTS = 16
EC_CAPACITY = 2
CONV_WIDTH = 31
LRU_CONV = 4
LRU_HEADS = 8
LRU_C = 8.0
S5_GROUP = 16
S5_STATE = 64
S5_CHUNK = 16
GRID_W = 64
POS_BASE = 10000.0

LANES = 128
SUBLANES = 8
ROW_BLOCK = 512
VMEM_LIMIT = 56 * 1024 * 1024


def _cparams(*sem):
    return pltpu.CompilerParams(dimension_semantics=sem, vmem_limit_bytes=VMEM_LIMIT)


def _full(shape):
    nd = len(shape)
    return pl.BlockSpec(shape, lambda *_: (0,) * nd)


def _rms_mod(x, g, scale, shift):
    y = x * lax.rsqrt(jnp.mean(x * x, axis=-1, keepdims=True) + EPS)
    return (y * g) * (1.0 + scale) + shift


def _silu(x):
    return x * jax.nn.sigmoid(x)


def _row_block(n):
    return min(ROW_BLOCK, n)


def _mod_kernel(c_ref, w_ref, b_ref, o_ref):
    s = _silu(c_ref[...])
    o_ref[0] = jnp.dot(s, w_ref[0], precision=HIGHEST, preferred_element_type=F32) + b_ref[0]


def _modulation(cond, w_mod, b_mod):
    depth, d, d6 = w_mod.shape
    rows = cond.shape[0]
    tn = 1536
    return pl.pallas_call(
        _mod_kernel,
        out_shape=jax.ShapeDtypeStruct((depth, rows, d6), F32),
        grid=(depth, d6 // tn),
        in_specs=[pl.BlockSpec((rows, d), lambda i, j: (0, 0)),
                  pl.BlockSpec((1, d, tn), lambda i, j: (i, 0, j)),
                  pl.BlockSpec((1, 1, tn), lambda i, j: (i, 0, j))],
        out_specs=pl.BlockSpec((1, rows, tn), lambda i, j: (i, 0, j)),
        compiler_params=_cparams("parallel", "parallel"),
        name="modulation",
    )(cond, w_mod, b_mod.reshape(depth, 1, d6))


def _store_token_major(ref, lead, val, row0=0):
    rows, d = val.shape
    nl = d // LANES
    for c in range(nl):
        ref[lead + (pl.ds(row0 * nl + c, rows, stride=nl), slice(None))] = val[:, c * LANES:(c + 1) * LANES]


def _load_token_major(ref, lead, rows, d):
    nl = d // LANES
    return jnp.concatenate([ref[lead + (pl.ds(c, rows, stride=nl), slice(None))] for c in range(nl)], axis=1)


def _is_token_major(shape):
    return shape[-1] == LANES


def _stream_dims(x, d):
    return (x.shape[0], x.shape[1] * LANES // d) if _is_token_major(x.shape) else x.shape[:2]


def _stream_spec(x, tm, d):
    if _is_token_major(x.shape):
        return pl.BlockSpec((1, tm * (d // LANES), LANES), lambda b, i: (b, i, 0))
    return pl.BlockSpec((1, tm, d), lambda b, i: (b, i, 0))


def _stream_rows(x_ref, tm, d):
    if _is_token_major(x_ref.shape):
        return _load_token_major(x_ref, (0,), tm, d)
    return x_ref[0]


def _post(xn, mod_ref, gffn_ref, wr_ref, xo_ref, m_ref, aff_ref):
    _store_token_major(xo_ref, (0,), xn)
    m = _rms_mod(xn, gffn_ref[...], mod_ref[0, 4:5, :], mod_ref[0, 3:4, :])
    _store_token_major(m_ref, (0,), m)
    logits = lax.dot_general(wr_ref[...], m, (((1,), (1,)), ((), ())),
                             precision=HIGHEST, preferred_element_type=F32)
    ex = jnp.exp(logits - jnp.max(logits, axis=0, keepdims=True))
    aff_ref[0] = ex / jnp.sum(ex, axis=0, keepdims=True)


def _post_specs(bsz, n, d, tm):
    nl = d // LANES
    out_shape = (jax.ShapeDtypeStruct((bsz, n * nl, LANES), F32),
                 jax.ShapeDtypeStruct((bsz, n * nl, LANES), F32),
                 jax.ShapeDtypeStruct((bsz, N_EXPERTS, n), F32))
    out_specs = (pl.BlockSpec((1, tm * nl, LANES), lambda b, i: (b, i, 0)),
                 pl.BlockSpec((1, tm * nl, LANES), lambda b, i: (b, i, 0)),
                 pl.BlockSpec((1, N_EXPERTS, tm), lambda b, i: (b, 0, i)))
    return out_shape, out_specs


CONV_HALO = 16
CONV_ROWS = 128


def _conv_mixer_kernel(*refs, has_pe, tm):
    if has_pe:
        (x_ref, xp_ref, xn_ref, pe_ref, pep_ref, pen_ref, mod_ref, g_ref, win_ref, bin_ref, dw_ref, dwb_ref,
         lng_ref, lnb_ref, w_ref, b_ref, gffn_ref, wr_ref, xo_ref, m_ref, aff_ref, ext_ref, cv_ref) = refs
    else:
        (x_ref, xp_ref, xn_ref, mod_ref, g_ref, win_ref, bin_ref, dw_ref, dwb_ref,
         lng_ref, lnb_ref, w_ref, b_ref, gffn_ref, wr_ref, xo_ref, m_ref, aff_ref, ext_ref, cv_ref) = refs
    d = win_ref.shape[0]
    i = pl.program_id(1)
    last = pl.num_programs(1) - 1
    x = _stream_rows(x_ref, tm, d)
    x_prev = _stream_rows(xp_ref, CONV_HALO, d)
    x_next = _stream_rows(xn_ref, CONV_HALO, d)
    if has_pe:
        x = x + pe_ref[...]
        x_prev = x_prev + pep_ref[...]
        x_next = x_next + pen_ref[...]

    xe = jnp.concatenate([x_prev, x, x_next], axis=0)
    h = _rms_mod(xe, g_ref[...], mod_ref[0, 1:2, :], mod_ref[0, 0:1, :]).astype(BF16)
    z = jnp.dot(h, win_ref[...], preferred_element_type=F32) + bin_ref[...]
    u = z[:, :d] * jax.nn.sigmoid(z[:, d:])
    rid = lax.broadcasted_iota(I32, (tm + 2 * CONV_HALO, 1), 0)
    lo = jnp.where(i > 0, 0, CONV_HALO)
    hi = jnp.where(i < last, tm + 2 * CONV_HALO, tm + CONV_HALO)
    ext_ref[...] = jnp.where((rid >= lo) & (rid < hi), u, 0.0)

    first_tap = CONV_HALO - CONV_WIDTH // 2

    rows = min(CONV_ROWS, tm)
    for lt in range(d // LANES):
        ls = slice(lt * LANES, (lt + 1) * LANES)
        for c in range(tm // rows):
            base = c * rows
            acc = jnp.zeros((rows, LANES), F32)
            for s in range(SUBLANES):
                part = None
                for o in range(first_tap, first_tap + CONV_WIDTH):
                    if o % SUBLANES != s:
                        continue
                    lo_row = base + o - s
                    term = dw_ref[o - first_tap:o - first_tap + 1, ls] * ext_ref[lo_row:lo_row + rows + SUBLANES, ls]
                    part = term if part is None else part + term
                acc = acc + part[s:s + rows]
            cv_ref[base:base + rows, ls] = acc
    cv = cv_ref[...] + dwb_ref[...]
    mu = jnp.mean(cv, axis=-1, keepdims=True)
    xc = cv - mu
    ln = xc * lax.rsqrt(jnp.mean(xc * xc, axis=-1, keepdims=True) + EPS) * lng_ref[...] + lnb_ref[...]
    y = jnp.dot(_silu(ln).astype(BF16), w_ref[...], preferred_element_type=F32) + b_ref[...]
    _post(x + mod_ref[0, 2:3, :] * y, mod_ref, gffn_ref, wr_ref, xo_ref, m_ref, aff_ref)


def _conv_mixer(x, pe, mod, g_mix, w_in, b_in, dw, dw_b, ln_g, ln_b, w_out, b_out, g_ffn, w_router):
    d = w_in.shape[0]
    bsz, n = _stream_dims(x, d)
    tm = _row_block(n)
    hb = tm // CONV_HALO
    nh = n // CONV_HALO
    prev_blk = lambda i: jnp.maximum(i * hb - 1, 0)
    next_blk = lambda i: jnp.minimum((i + 1) * hb, nh - 1)
    if _is_token_major(x.shape):
        halo = lambda f: pl.BlockSpec((1, CONV_HALO * (d // LANES), LANES), lambda b, i: (b, f(i), 0))
    else:
        halo = lambda f: pl.BlockSpec((1, CONV_HALO, d), lambda b, i: (b, f(i), 0))
    in_specs = [_stream_spec(x, tm, d), halo(prev_blk), halo(next_blk)]
    args = [x, x, x]
    if pe is not None:
        in_specs += [pl.BlockSpec((tm, d), lambda b, i: (i, 0)),
                     pl.BlockSpec((CONV_HALO, d), lambda b, i: (prev_blk(i), 0)),
                     pl.BlockSpec((CONV_HALO, d), lambda b, i: (next_blk(i), 0))]
        args += [pe, pe, pe]
    vec = _full((1, d))
    in_specs += [pl.BlockSpec((1, 6, d), lambda b, i: (b, 0, 0)), vec, _full((d, 2 * d)), _full((1, 2 * d)),
                 _full((CONV_WIDTH, d)), vec, vec, vec, _full((d, d)), vec, vec, _full((N_EXPERTS, d))]
    args += [mod, g_mix.reshape(1, d), w_in.astype(BF16), b_in.reshape(1, 2 * d),
             dw, dw_b.reshape(1, d), ln_g.reshape(1, d), ln_b.reshape(1, d),
             w_out.astype(BF16), b_out.reshape(1, d), g_ffn.reshape(1, d), w_router.T]
    out_shape, out_specs = _post_specs(bsz, n, d, tm)
    return pl.pallas_call(
        functools.partial(_conv_mixer_kernel, has_pe=pe is not None, tm=tm),
        out_shape=out_shape, grid=(bsz, n // tm), in_specs=in_specs, out_specs=out_specs,
        scratch_shapes=[pltpu.VMEM((tm + 2 * CONV_HALO, d), F32), pltpu.VMEM((tm, d), F32)],
        compiler_params=_cparams("parallel", "parallel"), name="conv_mixer",
    )(*args)


GROUPS_PER_TILE = LANES // S5_GROUP
STEPS_PER_TILE = LANES // S5_GROUP


def _s5_in_kernel(x_ref, mod_ref, g_ref, ug_ref, slab_ref, *, tm):
    u = _rms_mod(_stream_rows(x_ref, tm, g_ref.shape[1]), g_ref[...], mod_ref[0, 1:2, :], mod_ref[0, 0:1, :])
    nl = u.shape[1] // LANES
    nc = tm // S5_CHUNK
    for lt in range(nl):
        slab_ref[lt] = u[:, lt * LANES:(lt + 1) * LANES]
    lane_step = lax.broadcasted_iota(I32, (nc, LANES), 1) // S5_GROUP
    for lt in range(nl):
        steps = [slab_ref[lt, pl.ds(t, nc, stride=S5_CHUNK), :] for t in range(S5_CHUNK)]
        for g8 in range(GROUPS_PER_TILE):
            for half in range(S5_CHUNK // STEPS_PER_TILE):
                acc = jnp.zeros((nc, LANES), F32)
                for tq in range(STEPS_PER_TILE):
                    shift = ((tq - g8) * S5_GROUP) % LANES
                    src = steps[half * STEPS_PER_TILE + tq]
                    acc = jnp.where(lane_step == tq, pltpu.roll(src, shift, 1) if shift else src, acc)
                ug_ref[0, lt * GROUPS_PER_TILE + g8, :, half * LANES:(half + 1) * LANES] = acc.astype(BF16)


def _s5_in(x, mod, g_mix):
    d = g_mix.shape[0]
    bsz, n = _stream_dims(x, d)
    tm = _row_block(n)
    groups = d // S5_GROUP
    lw = S5_CHUNK * S5_GROUP
    return pl.pallas_call(
        functools.partial(_s5_in_kernel, tm=tm),
        out_shape=jax.ShapeDtypeStruct((bsz, groups, n // S5_CHUNK, lw), BF16),
        grid=(bsz, n // tm),
        in_specs=[_stream_spec(x, tm, d),
                  pl.BlockSpec((1, 6, d), lambda b, i: (b, 0, 0)), _full((1, d))],
        out_specs=pl.BlockSpec((1, groups, tm // S5_CHUNK, lw), lambda b, i: (b, 0, i, 0)),
        scratch_shapes=[pltpu.VMEM((d // LANES, tm, LANES), F32)],
        compiler_params=_cparams("parallel", "parallel"), name="s5_in",
    )(x, mod, g_mix.reshape(1, d))


def _s5_tables(lam_re, lam_im, log_dt, b_re, b_im, c_re, c_im):
    L = S5_CHUNK
    dt = jnp.exp(log_dt)[:, :, None]
    mag = jnp.exp(lam_re * dt)
    ang = lam_im * dt
    lb_re = mag * jnp.cos(ang)
    lb_im = mag * jnp.sin(ang)
    nr = lb_re - 1.0
    ni = lb_im
    den = lam_re * lam_re + lam_im * lam_im
    coef_re = ((nr * lam_re + ni * lam_im) / den)[..., None]
    coef_im = ((ni * lam_re - nr * lam_im) / den)[..., None]
    bb_re = coef_re * b_re - coef_im * b_im
    bb_im = coef_re * b_im + coef_im * b_re

    def powers(steps):
        st = steps.astype(F32)
        m = jnp.exp((lam_re * dt)[..., None] * st)
        a = (lam_im * dt)[..., None] * st
        return m * jnp.cos(a), m * jnp.sin(a)

    pw_re, pw_im = powers(jnp.arange(L + 1))
    cp_re = c_re[:, :, None] * pw_re.transpose(0, 1, 3, 2)[:, :, :, None, :] \
        - c_im[:, :, None] * pw_im.transpose(0, 1, 3, 2)[:, :, :, None, :]
    cp_im = c_re[:, :, None] * pw_im.transpose(0, 1, 3, 2)[:, :, :, None, :] \
        + c_im[:, :, None] * pw_re.transpose(0, 1, 3, 2)[:, :, :, None, :]
    kern = (jnp.einsum("dgtkp,dgpj->dgtkj", cp_re[:, :, :L], bb_re, precision=HIGHEST)
            - jnp.einsum("dgtkp,dgpj->dgtkj", cp_im[:, :, :L], bb_im, precision=HIGHEST))
    lbb_re = pw_re.transpose(0, 1, 3, 2)[..., None] * bb_re[:, :, None] \
        - pw_im.transpose(0, 1, 3, 2)[..., None] * bb_im[:, :, None]
    lbb_im = pw_re.transpose(0, 1, 3, 2)[..., None] * bb_im[:, :, None] \
        + pw_im.transpose(0, 1, 3, 2)[..., None] * bb_re[:, :, None]

    s_idx = jnp.arange(L)[:, None]
    t_idx = jnp.arange(L)[None, :]
    G = lam_re.shape[1]
    K = S5_GROUP
    P = S5_STATE
    tes, fos = [], []
    for direction in range(2):
        if direction == 0:
            lag = t_idx - s_idx
            e_pow = (L - 1) - jnp.arange(L)
            f_pow = jnp.arange(L) + 1
        else:
            lag = s_idx - t_idx
            e_pow = jnp.arange(L)
            f_pow = L - jnp.arange(L)
        pick = (lag[:, :, None] == jnp.arange(L)).astype(F32)
        kd = jnp.einsum("stl,glkj->gstkj", pick, kern[direction], precision=HIGHEST)
        tmat = kd.transpose(0, 1, 4, 2, 3).reshape(G, L * K, L * K)
        e_re = lbb_re[direction][:, e_pow]
        e_im = lbb_im[direction][:, e_pow]
        emat = jnp.concatenate([e_re.transpose(0, 1, 3, 2).reshape(G, L * K, P),
                                e_im.transpose(0, 1, 3, 2).reshape(G, L * K, P)], axis=-1)
        tes.append(jnp.concatenate([tmat, emat], axis=-1))
        f_re = cp_re[direction][:, f_pow]
        f_im = cp_im[direction][:, f_pow]
        fos.append(jnp.concatenate([f_re.transpose(0, 3, 1, 2).reshape(G, P, L * K),
                                    -f_im.transpose(0, 3, 1, 2).reshape(G, P, L * K)], axis=1))
    te = jnp.stack(tes).astype(BF16)
    fo = jnp.stack(fos).astype(BF16)
    a_re, a_im = powers(jnp.full((1,), L))
    a_re, a_im = a_re[..., 0], a_im[..., 0]
    tab = jnp.stack([jnp.concatenate([a_re, a_re], axis=-1),
                     jnp.concatenate([-a_im, a_im], axis=-1)], axis=1)
    return te, fo, tab


S5_GROUPS_PER_STEP = 8


S5_SCAN_UNROLL = 4


def _s5_core_kernel(u_ref, tef_ref, teb_ref, fof_ref, fob_ref, tab_ref, yl_ref, yc_ref,
                    yf_ref, yb_ref, hf_ref, hb_ref, sf_ref, sb_ref, *, nctx, nlat):
    gb = S5_GROUPS_PER_STEP
    rows = nctx + nlat
    lw = S5_CHUNK * S5_GROUP
    for g in range(gb):
        u = u_ref[0, g]
        zf = jnp.dot(u[0:rows], tef_ref[g], preferred_element_type=F32)
        yf_ref[g] = zf[:, :lw]
        hf_ref[pl.ds(g, rows, stride=gb), :] = zf[:, lw:]
        sf_ref[pl.ds(g, rows, stride=gb), :] = pltpu.roll(zf[:, lw:], S5_STATE, 1)
        zb = jnp.dot(u[nctx:nctx + rows], teb_ref[g], preferred_element_type=F32)
        yb_ref[g] = zb[:, :lw]
        hb_ref[pl.ds(g, rows, stride=gb), :] = zb[:, lw:]
        sb_ref[pl.ds(g, rows, stride=gb), :] = pltpu.roll(zb[:, lw:], S5_STATE, 1)

    cat_f, swp_f = tab_ref[0, 0], tab_ref[0, 1]
    cat_b, swp_b = tab_ref[1, 0], tab_ref[1, 1]

    def scan_rows(i, carry):
        hf, sf, hb, sb = carry
        rf = pl.multiple_of(i * gb, gb)
        rb = pl.multiple_of((rows - 1 - i) * gb, gb)
        ef, esf = hf_ref[pl.ds(rf, gb), :], sf_ref[pl.ds(rf, gb), :]
        eb, esb = hb_ref[pl.ds(rb, gb), :], sb_ref[pl.ds(rb, gb), :]
        hf_ref[pl.ds(rf, gb), :] = hf
        hb_ref[pl.ds(rb, gb), :] = hb
        return (cat_f * hf + swp_f * sf + ef, cat_f * sf - swp_f * hf + esf,
                cat_b * hb + swp_b * sb + eb, cat_b * sb - swp_b * hb + esb)

    zero = jnp.zeros((gb, 2 * S5_STATE), F32)
    lax.fori_loop(0, rows, scan_rows, (zero, zero, zero, zero), unroll=S5_SCAN_UNROLL)

    for g in range(gb):
        hin_f = hf_ref[pl.ds(g, rows, stride=gb), :].astype(BF16)
        hin_b = hb_ref[pl.ds(g, rows, stride=gb), :].astype(BF16)
        yf = yf_ref[g] + jnp.dot(hin_f, fof_ref[g], preferred_element_type=F32)
        yb = yb_ref[g] + jnp.dot(hin_b, fob_ref[g], preferred_element_type=F32)
        yl_ref[0, g] = yf[nctx:nctx + nlat] + yb[0:nlat]
        yc_ref[0, g] = yf[0:nctx] + yb[nlat:nlat + nctx]


def _s5_core(u_all, te, fo, tab, nctx, nlat):
    bsz, G, rtot, lw = u_all.shape
    gb = S5_GROUPS_PER_STEP
    rows = nctx + nlat
    w = 2 * S5_STATE
    gspec3 = lambda shape: pl.BlockSpec((gb,) + shape, lambda b, j: (j, 0, 0))
    return pl.pallas_call(
        functools.partial(_s5_core_kernel, nctx=nctx, nlat=nlat),
        out_shape=(jax.ShapeDtypeStruct((bsz, G, nlat, lw), F32),
                   jax.ShapeDtypeStruct((bsz, G, nctx, lw), F32)),
        grid=(bsz, G // gb),
        in_specs=[pl.BlockSpec((1, gb, rtot, lw), lambda b, j: (b, j, 0, 0)),
                  gspec3((lw, lw + w)), gspec3((lw, lw + w)), gspec3((w, lw)), gspec3((w, lw)),
                  pl.BlockSpec((2, 2, gb, w), lambda b, j: (0, 0, j, 0))],
        out_specs=(pl.BlockSpec((1, gb, nlat, lw), lambda b, j: (b, j, 0, 0)),
                   pl.BlockSpec((1, gb, nctx, lw), lambda b, j: (b, j, 0, 0))),
        scratch_shapes=[pltpu.VMEM((gb, rows, lw), F32), pltpu.VMEM((gb, rows, lw), F32)]
        + [pltpu.VMEM((rows * gb, w), F32)] * 4,
        compiler_params=_cparams("parallel", "parallel"), name="s5_core",
    )(u_all, te[0], te[1], fo[0], fo[1], tab)


def _s5_out_kernel(yg_ref, x_ref, mod_ref, gmix_ref, dsk_ref, w_ref, b_ref, gffn_ref, wr_ref,
                   xo_ref, m_ref, aff_ref, slab_ref, *, tm):
    nl = gmix_ref.shape[1] // LANES
    nc = tm // S5_CHUNK
    lane_group = lax.broadcasted_iota(I32, (nc, LANES), 1) // S5_GROUP
    for lt in range(nl):
        for t in range(S5_CHUNK):
            half, tq = divmod(t, STEPS_PER_TILE)
            acc = jnp.zeros((nc, LANES), F32)
            for g8 in range(GROUPS_PER_TILE):
                src = yg_ref[0, lt * GROUPS_PER_TILE + g8, :, half * LANES:(half + 1) * LANES]
                shift = ((g8 - tq) * S5_GROUP) % LANES
                acc = jnp.where(lane_group == g8, pltpu.roll(src, shift, 1) if shift else src, acc)
            slab_ref[lt, pl.ds(t, nc, stride=S5_CHUNK), :] = acc
    ssm = jnp.concatenate([slab_ref[lt] for lt in range(nl)], axis=1)
    x = _stream_rows(x_ref, tm, gmix_ref.shape[1])
    u = _rms_mod(x, gmix_ref[...], mod_ref[0, 1:2, :], mod_ref[0, 0:1, :])
    y = ssm + dsk_ref[...] * u
    z = jnp.dot(jax.nn.gelu(y).astype(BF16), w_ref[...], preferred_element_type=F32) + b_ref[...]
    d = z.shape[1] // 2
    out = z[:, :d] * jax.nn.sigmoid(z[:, d:])
    _post(x + mod_ref[0, 2:3, :] * out, mod_ref, gffn_ref, wr_ref, xo_ref, m_ref, aff_ref)


def _s5_out(yg, x, mod, g_mix, d_skip, w_glu, b_glu, g_ffn, w_router):
    d = g_mix.shape[0]
    bsz, n = _stream_dims(x, d)
    tm = _row_block(n)
    groups = d // S5_GROUP
    lw = S5_CHUNK * S5_GROUP
    row = pl.BlockSpec((1, tm, d), lambda b, i: (b, i, 0))
    vec = _full((1, d))
    out_shape, out_specs = _post_specs(bsz, n, d, tm)
    return pl.pallas_call(
        functools.partial(_s5_out_kernel, tm=tm), out_shape=out_shape, grid=(bsz, n // tm),
        in_specs=[pl.BlockSpec((1, groups, tm // S5_CHUNK, lw), lambda b, i: (b, 0, i, 0)),
                  _stream_spec(x, tm, d), pl.BlockSpec((1, 6, d), lambda b, i: (b, 0, 0)), vec, vec,
                  _full((d, 2 * d)), _full((1, 2 * d)), vec, _full((N_EXPERTS, d))],
        out_specs=out_specs,
        scratch_shapes=[pltpu.VMEM((d // LANES, tm, LANES), F32)],
        compiler_params=_cparams("parallel", "parallel"), name="s5_out",
    )(yg, x, mod, g_mix.reshape(1, d), d_skip.reshape(1, d), w_glu.astype(BF16), b_glu.reshape(1, 2 * d),
      g_ffn.reshape(1, d), w_router.T)


def _lru_in_kernel(x_ref, mod_ref, g_ref, wx_ref, bx_ref, wy_ref, by_ref, xp_ref, gate_ref):
    x = _stream_rows(x_ref, xp_ref.shape[1], xp_ref.shape[2])
    h = _rms_mod(x, g_ref[...], mod_ref[0, 1:2, :], mod_ref[0, 0:1, :]).astype(BF16)
    xp_ref[0] = jnp.dot(h, wx_ref[...], preferred_element_type=F32) + bx_ref[...]
    gate_ref[0] = jax.nn.gelu(jnp.dot(h, wy_ref[...], preferred_element_type=F32) + by_ref[...])


def _lru_in(x, mod, g_mix, w_x, b_x, w_y, b_y):
    d = g_mix.shape[0]
    bsz, n = _stream_dims(x, d)
    tm = _row_block(n)
    row = pl.BlockSpec((1, tm, d), lambda b, i: (b, i, 0))
    vec = _full((1, d))
    return pl.pallas_call(
        _lru_in_kernel,
        out_shape=(jax.ShapeDtypeStruct((bsz, n, d), F32), jax.ShapeDtypeStruct((bsz, n, d), F32)),
        grid=(bsz, n // tm),
        in_specs=[_stream_spec(x, tm, d), pl.BlockSpec((1, 6, d), lambda b, i: (b, 0, 0)), vec,
                  _full((d, d)), vec, _full((d, d)), vec],
        out_specs=(row, row), compiler_params=_cparams("parallel", "parallel"), name="lru_in",
    )(x, mod, g_mix.reshape(1, d), w_x.astype(BF16), b_x.reshape(1, d), w_y.astype(BF16), b_y.reshape(1, d))


LRU_HALO = 8
LRU_SCAN_UNROLL = 8


def _lru_scan_kernel(xf_ref, xfp_ref, xfn_ref, xb_ref, xbp_ref, xbn_ref, h0_ref, cw_ref, cb_ref,
                     wai_ref, bai_ref, lam_ref, rf_ref, rb_ref, hfin_ref,
                     ext_ref, xl_ref, a_ref, b_ref, r_ref, h_ref, *, tm):
    i = pl.program_id(1)
    last = pl.num_programs(1) - 1
    hw = LANES
    d = LRU_HEADS * hw

    @pl.when(i == 0)
    def _():
        h_ref[...] = h0_ref[0]

    lam = lam_ref[...]
    nl = -lam
    softplus = jnp.maximum(nl, 0.0) + jnp.log1p(jnp.exp(-jnp.abs(nl)))
    c8 = -LRU_C * softplus

    def sigmoid(v):
        return 0.5 * jnp.tanh(0.5 * v) + 0.5

    def gates(dr, main_ref, prev_ref, next_ref, blk):
        _store_token_major(ext_ref, (), jnp.where(blk > 0, prev_ref[0], 0.0))
        _store_token_major(ext_ref, (), main_ref[0], row0=LRU_HALO)
        _store_token_major(ext_ref, (), jnp.where(blk < last, next_ref[0], 0.0), row0=LRU_HALO + tm)
        xl3 = jnp.broadcast_to(cb_ref[...], (tm, LRU_HEADS, hw))
        for k in range(LRU_CONV):
            off = (LRU_HALO - LRU_CONV // 2 + k) * LRU_HEADS
            xl3 = xl3 + cw_ref[k:k + 1] * ext_ref[off:off + tm * LRU_HEADS, :].reshape(tm, LRU_HEADS, hw)
        xl_ref[...] = xl3.reshape(tm * LRU_HEADS, hw)
        for hd in range(LRU_HEADS):
            sl = slice(hd * hw, (hd + 1) * hw)
            xl = xl_ref[pl.ds(hd, tm, stride=LRU_HEADS), :]
            z = jnp.dot(xl.astype(BF16), wai_ref[dr, hd], preferred_element_type=F32)
            r = sigmoid(z[:, :hw] + bai_ref[dr, 0:1, sl])
            ig = sigmoid(z[:, hw:] + bai_ref[dr, 1:2, sl])
            log_a = c8[dr:dr + 1, sl] * r
            th = jnp.tanh(log_a)
            a_ref[dr, pl.ds(hd, tm, stride=LRU_HEADS), :] = jnp.exp(log_a)
            b_ref[dr, pl.ds(hd, tm, stride=LRU_HEADS), :] = jnp.sqrt(-2.0 * th / (1.0 - th)) * (ig * xl)

    gates(0, xf_ref, xfp_ref, xfn_ref, i)
    gates(1, xb_ref, xbp_ref, xbn_ref, last - i)

    def step(t, carry):
        hf, hb = carry
        rf = pl.multiple_of(t * LRU_HEADS, LRU_HEADS)
        rb = pl.multiple_of((tm - 1 - t) * LRU_HEADS, LRU_HEADS)
        hf = a_ref[0, pl.ds(rf, LRU_HEADS), :] * hf + b_ref[0, pl.ds(rf, LRU_HEADS), :]
        hb = a_ref[1, pl.ds(rb, LRU_HEADS), :] * hb + b_ref[1, pl.ds(rb, LRU_HEADS), :]
        r_ref[0, pl.ds(rf, LRU_HEADS), :] = hf
        r_ref[1, pl.ds(rb, LRU_HEADS), :] = hb
        return hf, hb

    hf, hb = lax.fori_loop(0, tm, step, (h_ref[0], h_ref[1]), unroll=LRU_SCAN_UNROLL)
    h_ref[0] = hf
    h_ref[1] = hb
    hfin_ref[0] = h_ref[...]
    rf_ref[0] = _load_token_major(r_ref, (0,), tm, d)
    rb_ref[0] = _load_token_major(r_ref, (1,), tm, d)


def _lru_scan(xp, h0, conv_w, conv_b, w_a, b_a, w_i, b_i, lam):
    bsz, n, d = xp.shape
    tm = _row_block(n)
    nb = n // tm
    hb = tm // LRU_HALO
    nh = n // LRU_HALO
    fwd = lambda b, i: (b, i, 0)
    bwd = lambda b, i: (b, nb - 1 - i, 0)
    halo = lambda f: pl.BlockSpec((1, LRU_HALO, d), f)
    wai = jnp.concatenate([w_a, w_i], axis=-1).astype(BF16)
    bai = jnp.stack([b_a, b_i], axis=1)
    assert d // LRU_HEADS == LANES, "token-major scan assumes one 128-lane tile per head"
    state = pl.BlockSpec((1, 2, LRU_HEADS, LANES), lambda b, i: (b, 0, 0, 0))
    return pl.pallas_call(
        functools.partial(_lru_scan_kernel, tm=tm),
        out_shape=(jax.ShapeDtypeStruct((bsz, n, d), F32), jax.ShapeDtypeStruct((bsz, n, d), F32),
                   jax.ShapeDtypeStruct((bsz, 2, LRU_HEADS, LANES), F32)),
        grid=(bsz, nb),
        in_specs=[pl.BlockSpec((1, tm, d), fwd),
                  halo(lambda b, i: (b, jnp.maximum(i * hb - 1, 0), 0)),
                  halo(lambda b, i: (b, jnp.minimum((i + 1) * hb, nh - 1), 0)),
                  pl.BlockSpec((1, tm, d), bwd),
                  halo(lambda b, i: (b, jnp.maximum((nb - 1 - i) * hb - 1, 0), 0)),
                  halo(lambda b, i: (b, jnp.minimum((nb - i) * hb, nh - 1), 0)),
                  state,
                  _full((LRU_CONV, LRU_HEADS, LANES)), _full((1, LRU_HEADS, LANES)),
                  _full(wai.shape), _full(bai.shape), _full((2, d))],
        out_specs=(pl.BlockSpec((1, tm, d), fwd), pl.BlockSpec((1, tm, d), bwd), state),
        scratch_shapes=[pltpu.VMEM(((tm + 2 * LRU_HALO) * LRU_HEADS, LANES), F32),
                        pltpu.VMEM((tm * LRU_HEADS, LANES), F32)]
        + [pltpu.VMEM((2, tm * LRU_HEADS, LANES), F32)] * 3
        + [pltpu.VMEM((2, LRU_HEADS, LANES), F32)],
        compiler_params=_cparams("parallel", "arbitrary"), name="lru_scan",
    )(xp, xp, xp, xp, xp, xp, h0, conv_w.reshape(LRU_CONV, LRU_HEADS, LANES), conv_b.reshape(1, LRU_HEADS, LANES),
      wai, bai, lam)


def _lru_out_kernel(rf_ref, rb_ref, gate_ref, x_ref, mod_ref, w_ref, b_ref, gffn_ref, wr_ref,
                    xo_ref, m_ref, aff_ref):
    r = (rf_ref[0] + rb_ref[0]) * gate_ref[0]
    y = jnp.dot(r.astype(BF16), w_ref[...], preferred_element_type=F32) + b_ref[...]
    x = _stream_rows(x_ref, rf_ref.shape[1], rf_ref.shape[2])
    _post(x + mod_ref[0, 2:3, :] * y, mod_ref, gffn_ref, wr_ref, xo_ref, m_ref, aff_ref)


def _lru_out(rf, rb, gate, x, mod, w_out, b_out, g_ffn, w_router):
    bsz, n, d = rf.shape
    tm = _row_block(n)
    row = pl.BlockSpec((1, tm, d), lambda b, i: (b, i, 0))
    vec = _full((1, d))
    out_shape, out_specs = _post_specs(bsz, n, d, tm)
    return pl.pallas_call(
        _lru_out_kernel, out_shape=out_shape, grid=(bsz, n // tm),
        in_specs=[row, row, row, _stream_spec(x, tm, d), pl.BlockSpec((1, 6, d), lambda b, i: (b, 0, 0)),
                  _full((d, d)), vec, vec, _full((N_EXPERTS, d))],
        out_specs=out_specs, compiler_params=_cparams("parallel", "parallel"), name="lru_out",
    )(rf, rb, gate, x, mod, w_out.astype(BF16), b_out.reshape(1, d), g_ffn.reshape(1, d), w_router.T)


INF_BITS = 0x7F800000
SLOT_UNROLL = 32


def _tile_cumsum(mask, tri):
    return jnp.dot(mask.astype(BF16), tri, preferred_element_type=F32)


def _select_kernel(aff_ref, lpos_ref, off_ref, sel_ref, *, n, cap):
    nt = n // LANES
    bits = pltpu.bitcast(aff_ref[0], I32)
    capf = float(cap)

    def bisect(_, lohi):
        lo, hi = lohi
        mid = lo + ((hi - lo + 1) >> 1)
        cnt = jnp.sum(jnp.where(bits >= mid, 1.0, 0.0), axis=1, keepdims=True)
        ok = cnt >= capf
        return jnp.where(ok, mid, lo), jnp.where(ok, hi, mid - 1)

    e = bits.shape[0]
    thr, _ = lax.fori_loop(0, 31, bisect, (jnp.zeros((e, 1), I32), jnp.full((e, 1), INF_BITS, I32)))
    gt = jnp.where(bits > thr, 1.0, 0.0)
    eq = jnp.where(bits == thr, 1.0, 0.0)
    need = capf - jnp.sum(gt, axis=1, keepdims=True)

    r_i = lax.broadcasted_iota(I32, (LANES, LANES), 0)
    c_i = lax.broadcasted_iota(I32, (LANES, LANES), 1)
    tri = jnp.where(r_i <= c_i, 1.0, 0.0).astype(BF16)
    sup = jnp.where(r_i < c_i, 1.0, 0.0).astype(BF16)
    t_i = lax.broadcasted_iota(I32, (n, LANES), 0)
    k_i = lax.broadcasted_iota(I32, (n, LANES), 1)
    tile_of = jnp.where((t_i >> 7) == k_i, 1.0, 0.0).astype(BF16)

    def tile_offsets(mask):
        counts = jnp.dot(mask.astype(BF16), tile_of, preferred_element_type=F32)
        return jnp.dot(counts.astype(BF16), sup, preferred_element_type=F32)

    eq_off = tile_offsets(eq)
    for k in range(nt):
        sl = slice(k * LANES, (k + 1) * LANES)
        eqk = eq[:, sl]
        rank = eq_off[:, k:k + 1] + _tile_cumsum(eqk, tri) - eqk
        sel_ref[:, sl] = gt[:, sl] + eqk * jnp.where(rank < need, 1.0, 0.0)
    sel = sel_ref[...]
    off_ref[0] = tile_offsets(sel).astype(I32)
    for k in range(nt):
        sl = slice(k * LANES, (k + 1) * LANES)
        sk = sel[:, sl]
        lpos_ref[0, :, sl] = jnp.where(sk > 0.0, _tile_cumsum(sk, tri) - 1.0, -1.0).astype(I32)


def _select(aff_t, cap):
    bsz, e, n = aff_t.shape
    return pl.pallas_call(
        functools.partial(_select_kernel, n=n, cap=cap),
        out_shape=(jax.ShapeDtypeStruct((bsz, e, n), I32), jax.ShapeDtypeStruct((bsz, e, LANES), I32)),
        grid=(bsz,),
        in_specs=[pl.BlockSpec((1, e, n), lambda b: (b, 0, 0))],
        out_specs=(pl.BlockSpec((1, e, n), lambda b: (b, 0, 0)),
                   pl.BlockSpec((1, e, LANES), lambda b: (b, 0, 0))),
        scratch_shapes=[pltpu.VMEM((e, n), F32)],
        compiler_params=_cparams("parallel"), name="moe_select",
    )(aff_t)


def _slots_kernel(off_ref, lpos_ref, aff_ref, out_ref, scr_ref, *, nt, cap):
    b = pl.program_id(0)
    e = pl.program_id(1)
    base = (b * pl.num_programs(1) + e) * LANES
    j_i = lax.broadcasted_iota(I32, (LANES, LANES), 0)
    row = lax.broadcasted_iota(I32, (LANES, LANES), 0)
    lane = lax.broadcasted_iota(I32, (LANES, LANES), 1)

    def tile(k, carry):
        lp = lpos_ref[0, 0, k]
        av = aff_ref[0, 0, k]
        onehot = jnp.where(lp == j_i, 1.0, 0.0).astype(BF16)
        a1 = av.astype(BF16).astype(F32)
        r1 = av - a1
        a2 = r1.astype(BF16).astype(F32)
        a3 = r1 - a2
        q = jnp.where(row == 0, lane.astype(F32),
                      jnp.where(row == 1, a1, jnp.where(row == 2, a2, jnp.where(row == 3, a3, 0.0))))
        res = lax.dot_general(onehot, q.astype(BF16), (((1,), (1,)), ((), ())),
                              preferred_element_type=F32)
        res = res + jnp.where(lane == 0, jnp.asarray(k * LANES, F32), 0.0)
        scr_ref[pl.ds(off_ref[base + k], LANES), :] = res
        return carry

    lax.fori_loop(0, nt, tile, 0, unroll=min(SLOT_UNROLL, nt))
    out_ref[0, 0] = scr_ref[0:cap, :]


def _slots(lpos, tile_off, aff_t, cap):
    bsz, e, n = lpos.shape
    nt = n // LANES
    lp5 = lpos.reshape(bsz, e, nt, 1, LANES)
    af5 = aff_t.reshape(bsz, e, nt, 1, LANES)
    blk = pl.BlockSpec((1, 1, nt, 1, LANES), lambda b, j, off: (b, j, 0, 0, 0))
    return pl.pallas_call(
        functools.partial(_slots_kernel, nt=nt, cap=cap),
        out_shape=jax.ShapeDtypeStruct((bsz, e, cap, LANES), F32),
        grid_spec=pltpu.PrefetchScalarGridSpec(
            num_scalar_prefetch=1, grid=(bsz, e), in_specs=[blk, blk],
            out_specs=pl.BlockSpec((1, 1, cap, LANES), lambda b, j, off: (b, j, 0, 0)),
            scratch_shapes=[pltpu.VMEM((cap + LANES, LANES), F32)]),
        compiler_params=_cparams("parallel", "parallel"), name="moe_slots",
    )(tile_off.reshape(-1), lp5, af5)


ROW_UNROLL = 32


def _gather_kernel(idx_ref, m_ref, xs_ref, scr_ref, *, cap, nl):
    b = pl.program_id(0)
    e = pl.program_id(1)
    base = (b * pl.num_programs(1) + e) * cap

    def rows(ju, carry):
        j0 = ju * ROW_UNROLL
        for r in range(ROW_UNROLL):
            t = idx_ref[base + j0 + r]
            scr_ref[pl.ds(pl.multiple_of((j0 + r) * nl, nl), nl), :] = \
                m_ref[0, pl.ds(pl.multiple_of(t * nl, nl), nl), :]
        return carry

    lax.fori_loop(0, cap // ROW_UNROLL, rows, 0)
    xs_ref[0, 0] = _load_token_major(scr_ref, (), cap, nl * LANES).astype(BF16)


def _gather(idx_flat, m_tok, e, cap):
    bsz, rows_, _ = m_tok.shape
    nl = SUBLANES
    d = nl * LANES
    return pl.pallas_call(
        functools.partial(_gather_kernel, cap=cap, nl=nl),
        out_shape=jax.ShapeDtypeStruct((bsz, e, cap, d), BF16),
        grid_spec=pltpu.PrefetchScalarGridSpec(
            num_scalar_prefetch=1, grid=(bsz, e),
            in_specs=[pl.BlockSpec((1, rows_, LANES), lambda b, j, idx: (b, 0, 0),
                                   pipeline_mode=pl.Buffered(1))],
            out_specs=pl.BlockSpec((1, 1, cap, d), lambda b, j, idx: (b, j, 0, 0)),
            scratch_shapes=[pltpu.VMEM((cap * nl, LANES), F32)]),
        compiler_params=_cparams("parallel", "arbitrary"), name="moe_gather",
    )(idx_flat, m_tok)


FFN_CHUNK = 512


def _ffn_rows(xs, gs, w1b_ref, w3b_ref, w2b_ref):
    de = w1b_ref.shape[1]
    fc = min(FFN_CHUNK, de)
    acc = None
    for c in range(de // fc):
        sl = slice(c * fc, (c + 1) * fc)
        h1 = jnp.dot(xs, w1b_ref[:, sl], preferred_element_type=F32)
        h3 = jnp.dot(xs, w3b_ref[:, sl], preferred_element_type=F32)
        h = (_silu(h1) * h3).astype(BF16)
        part = jnp.dot(h, w2b_ref[sl, :], preferred_element_type=F32)
        acc = part if acc is None else acc + part
    gate = gs[:, 1:2] + gs[:, 2:3] + gs[:, 3:4]
    return acc * gate


def _ffn_kernel(*refs, n_streams, batched):
    ins = refs[:2 * n_streams]
    w1_ref, w3_ref, w2_ref = refs[2 * n_streams:2 * n_streams + 3]
    outs = refs[2 * n_streams + 3:3 * n_streams + 3]
    w1b_ref, w3b_ref, w2b_ref = refs[3 * n_streams + 3:]

    @pl.when(pl.program_id(1) == 0)
    def _():
        w1b_ref[...] = w1_ref[0, 0].astype(BF16)
        w3b_ref[...] = w3_ref[0, 0].astype(BF16)
        w2b_ref[...] = w2_ref[0, 0].astype(BF16)

    for k in range(n_streams):
        xs_ref, gs_ref, y_ref = ins[2 * k], ins[2 * k + 1], outs[k]
        if batched[k]:
            @pl.when(pl.program_id(1) == 0)
            def _(xs_ref=xs_ref, gs_ref=gs_ref, y_ref=y_ref):
                nb, _, cap, d = xs_ref.shape
                y = _ffn_rows(xs_ref[:, 0].reshape(nb * cap, d), gs_ref[:, 0].reshape(nb * cap, LANES),
                              w1b_ref, w3b_ref, w2b_ref)
                for s in range(nb):
                    _store_token_major(y_ref, (s, 0), y[s * cap:(s + 1) * cap])
        else:
            _store_token_major(y_ref, (0, 0), _ffn_rows(xs_ref[0, 0], gs_ref[0, 0], w1b_ref, w3b_ref, w2b_ref))


FFN_BATCH_ROWS = 128


def _ffn(streams, experts, layer):
    w1, w3, w2 = experts
    bsz, e, _, d = streams[0][0].shape
    de = w1.shape[3]
    nl = d // LANES
    in_specs, args, out_shape, out_specs, batched = [], [], [], [], []
    for xs, slots in streams:
        cap = xs.shape[2]
        whole = cap < FFN_BATCH_ROWS
        batched.append(whole)
        nb = bsz if whole else 1
        sample = (lambda j, b: (0, j, 0, 0)) if whole else (lambda j, b: (b, j, 0, 0))
        in_specs += [pl.BlockSpec((nb, 1, cap, d), sample), pl.BlockSpec((nb, 1, cap, LANES), sample)]
        args += [xs, slots]
        out_shape.append(jax.ShapeDtypeStruct((bsz, e, cap * nl, LANES), F32))
        out_specs.append(pl.BlockSpec((nb, 1, cap * nl, LANES), sample))
    in_specs += [pl.BlockSpec((1, 1, d, de), lambda j, b: (layer, j, 0, 0)),
                 pl.BlockSpec((1, 1, d, de), lambda j, b: (layer, j, 0, 0)),
                 pl.BlockSpec((1, 1, de, d), lambda j, b: (layer, j, 0, 0))]
    return pl.pallas_call(
        functools.partial(_ffn_kernel, n_streams=len(streams), batched=tuple(batched)),
        out_shape=tuple(out_shape),
        grid=(e, bsz), in_specs=in_specs, out_specs=tuple(out_specs),
        scratch_shapes=[pltpu.VMEM((d, de), BF16), pltpu.VMEM((d, de), BF16), pltpu.VMEM((de, d), BF16)],
        compiler_params=_cparams("parallel", "arbitrary"), name="moe_ffn",
    )(*args, w1, w3, w2)


COMBINE_UNROLL = 4


def _combine_kernel(idx_ref, y_ref, g2_ref, xin_hbm, acc_ref, sem, *, cap, nl):
    b = pl.program_id(0)
    e = pl.program_id(1)
    base = (b * pl.num_programs(1) + e) * cap

    @pl.when(e == 0)
    def _():
        load = pltpu.make_async_copy(xin_hbm.at[b], acc_ref.at[0], sem)
        load.start()
        load.wait()

    g2 = g2_ref[0]

    def rows(ju, carry):
        j0 = ju * COMBINE_UNROLL
        ts = [pl.multiple_of(idx_ref[base + j0 + r] * nl, nl) for r in range(COMBINE_UNROLL)]
        vals = [acc_ref[0, pl.ds(ts[r], nl), :]
                + g2 * y_ref[0, 0, pl.ds(pl.multiple_of((j0 + r) * nl, nl), nl), :]
                for r in range(COMBINE_UNROLL)]
        for r in range(COMBINE_UNROLL):
            acc_ref[0, pl.ds(ts[r], nl), :] = vals[r]
        return carry

    lax.fori_loop(0, cap // COMBINE_UNROLL, rows, 0)


def _combine(idx_flat, y_tok, x_tok, gate2, cap):
    bsz, e, rows_, _ = y_tok.shape
    nl = rows_ // cap
    return pl.pallas_call(
        functools.partial(_combine_kernel, cap=cap, nl=nl),
        out_shape=jax.ShapeDtypeStruct(x_tok.shape, F32),
        grid_spec=pltpu.PrefetchScalarGridSpec(
            num_scalar_prefetch=1, grid=(bsz, e),
            in_specs=[pl.BlockSpec((1, 1, rows_, LANES), lambda b, j, idx: (b, j, 0, 0)),
                      pl.BlockSpec((1, nl, LANES), lambda b, j, idx: (b, 0, 0)),
                      pl.BlockSpec(memory_space=pl.ANY)],
            out_specs=pl.BlockSpec((1,) + x_tok.shape[1:], lambda b, j, idx: (b, 0, 0),
                                   pipeline_mode=pl.Buffered(1)),
            scratch_shapes=[pltpu.SemaphoreType.DMA(())]),
        compiler_params=_cparams("parallel", "arbitrary"), name="moe_combine",
    )(idx_flat, y_tok, gate2.reshape(bsz, nl, LANES), x_tok)


def _final_norm_kernel(x_ref, g_ref, o_ref):
    x = _stream_rows(x_ref, o_ref.shape[1], o_ref.shape[2])
    o_ref[0] = x * lax.rsqrt(jnp.mean(x * x, axis=-1, keepdims=True) + EPS) * g_ref[...]


def _final_norm(x, final_g):
    d = final_g.shape[0]
    bsz, n = _stream_dims(x, d)
    tm = _row_block(n)
    return pl.pallas_call(
        _final_norm_kernel, out_shape=jax.ShapeDtypeStruct((bsz, n, d), F32), grid=(bsz, n // tm),
        in_specs=[_stream_spec(x, tm, d), _full((1, d))],
        out_specs=pl.BlockSpec((1, tm, d), lambda b, i: (b, i, 0)),
        compiler_params=_cparams("parallel", "parallel"), name="final_norm",
    )(x, final_g.reshape(1, d))


def _moe(streams, experts, layer):
    routed = []
    for x_tok, m_tok, aff_t, mod in streams:
        e, n = aff_t.shape[1], aff_t.shape[2]
        cap = EC_CAPACITY * n // e
        lpos, tile_off = _select(aff_t, cap)
        slots = _slots(lpos, tile_off, aff_t, cap)
        idx_flat = slots[..., 0].astype(I32).reshape(-1)
        routed.append((idx_flat, _gather(idx_flat, m_tok, e, cap), slots, cap))
    ys = _ffn([(xs, slots) for _, xs, slots, _ in routed], experts, layer)
    return [_combine(idx_flat, y_tok, x_tok, mod[:, 5], cap)
            for (idx_flat, _, _, cap), y_tok, (x_tok, _, _, mod) in zip(routed, ys, streams)]


def _pos_embed(n, d):
    rows = n // GRID_W
    quarter = d // 4
    omega = 1.0 / (POS_BASE ** (jnp.arange(quarter, dtype=F32) / quarter))
    r = jnp.arange(rows, dtype=F32)[:, None] * omega
    cc = jnp.arange(GRID_W, dtype=F32)[:, None] * omega
    row_emb = jnp.concatenate([jnp.sin(r), jnp.cos(r)], axis=-1)
    col_emb = jnp.concatenate([jnp.sin(cc), jnp.cos(cc)], axis=-1)
    emb = jnp.concatenate([
        jnp.broadcast_to(row_emb[:, None, :], (rows, GRID_W, d // 2)),
        jnp.broadcast_to(col_emb[None, :, :], (rows, GRID_W, d // 2))], axis=-1)
    return emb.reshape(rows * GRID_W, d)


def kernel(x, c, ctx, c_ctx, w_mod, b_mod, g_mix, g_ffn, conv_w_in, conv_b_in, conv_dw, conv_dw_b, conv_ln_g, conv_ln_b, conv_w_out, conv_b_out, s5_lam_re, s5_lam_im, s5_log_dt, s5_b_re, s5_b_im, s5_c_re, s5_c_im, s5_d, s5_w_glu, s5_b_glu, lru_w_y, lru_b_y, lru_w_x, lru_b_x, lru_conv_w, lru_conv_b, lru_w_a, lru_b_a, lru_w_i, lru_b_i, lru_lam, lru_w_out, lru_b_out, moe_router, moe_w1, moe_w3, moe_w2, final_g):
    bsz, n, d = x.shape
    nctx = ctx.shape[1]
    depth = w_mod.shape[0]
    n_mixers = 3
    reader_layers = [i for i in range(depth) if i % n_mixers != 0]
    last_reader = max(reader_layers) if reader_layers else -1

    rows = -(-(bsz + 1) // SUBLANES) * SUBLANES
    cond = jnp.zeros((rows, d), F32).at[:bsz].set(c).at[bsz].set(c_ctx)
    mods = _modulation(cond, w_mod, b_mod).reshape(depth, rows, 6, d)
    pe = _pos_embed(n, d)
    assert d == SUBLANES * LANES, "token-major MoE rows assume one (8, 128) tile per token"
    experts = (moe_w1, moe_w3, moe_w2)

    x_lat, x_ctx = x, ctx
    for i in range(depth):
        kind, j = i % n_mixers, i // n_mixers
        ctx_in = i <= last_reader
        ctx_out = i < last_reader
        mod_lat = mods[i, :bsz]
        mod_ctx = jnp.broadcast_to(mods[i, bsz][None], (bsz, 6, d))
        lat_pe = pe if i == 0 else None
        streams = [(x_lat, mod_lat, lat_pe)]
        if ctx_out:
            streams.append((x_ctx, mod_ctx, None))

        if kind == 0:
            post = []
            for xs_, mod_, pe_ in streams:
                post.append(_conv_mixer(xs_, pe_, mod_, g_mix[i], conv_w_in[j], conv_b_in[j], conv_dw[j],
                                        conv_dw_b[j], conv_ln_g[j], conv_ln_b[j], conv_w_out[j], conv_b_out[j],
                                        g_ffn[i], moe_router[i]))
        elif kind == 1:
            te, fo, tab = _s5_tables(s5_lam_re[j], s5_lam_im[j], s5_log_dt[j], s5_b_re[j], s5_b_im[j],
                                     s5_c_re[j], s5_c_im[j])
            uc = _s5_in(x_ctx, mod_ctx, g_mix[i])
            u_all = jnp.concatenate([uc, _s5_in(x_lat, mod_lat, g_mix[i]), uc], axis=2)
            y_lat, y_ctx = _s5_core(u_all, te, fo, tab, nctx // S5_CHUNK, n // S5_CHUNK)
            post = [_s5_out(y_lat, x_lat, mod_lat, g_mix[i], s5_d[j], s5_w_glu[j], s5_b_glu[j],
                            g_ffn[i], moe_router[i])]
            if ctx_out:
                post.append(_s5_out(y_ctx, x_ctx, mod_ctx, g_mix[i], s5_d[j], s5_w_glu[j],
                                    s5_b_glu[j], g_ffn[i], moe_router[i]))
        else:
            scan_args = (lru_conv_w[j], lru_conv_b[j], lru_w_a[j], lru_b_a[j], lru_w_i[j], lru_b_i[j], lru_lam[j])
            xp_ctx, gate_ctx = _lru_in(x_ctx, mod_ctx, g_mix[i], lru_w_x[j], lru_b_x[j], lru_w_y[j], lru_b_y[j])
            rf_c, rb_c, h_ctx = _lru_scan(xp_ctx, jnp.zeros((bsz, 2, LRU_HEADS, d // LRU_HEADS), F32), *scan_args)
            xp_lat, gate_lat = _lru_in(x_lat, mod_lat, g_mix[i], lru_w_x[j], lru_b_x[j], lru_w_y[j], lru_b_y[j])
            rf_l, rb_l, _ = _lru_scan(xp_lat, h_ctx, *scan_args)
            post = [_lru_out(rf_l, rb_l, gate_lat, x_lat, mod_lat, lru_w_out[j], lru_b_out[j],
                             g_ffn[i], moe_router[i])]
            if ctx_out:
                post.append(_lru_out(rf_c, rb_c, gate_ctx, x_ctx, mod_ctx, lru_w_out[j], lru_b_out[j],
                                     g_ffn[i], moe_router[i]))

        mods_ = [mod_lat, mod_ctx]
        outs = _moe([(xn, m, aff, mods_[k]) for k, (xn, m, aff) in enumerate(post)], experts, i)
        x_lat = outs[0]
        if ctx_out:
            x_ctx = outs[1]
    return _final_norm(x_lat, final_g)
```

```python
import functools
import math

import jax
import jax.numpy as jnp
from jax import lax
from jax.experimental import pallas as pl
from jax.experimental.pallas import tpu as pltpu

F32 = jnp.float32
BF16 = jnp.bfloat16
I32 = jnp.int32
HIGHEST = lax.Precision.HIGHEST

EPS = 1e-6
N_EXPERTS = 16
EC_CAPACITY = 2
CONV_WIDTH = 31
LRU_CONV = 4
LRU_HEADS = 8
LRU_C = 8.0
S5_GROUP = 16
S5_STATE = 64
S5_CHUNK = 16
GRID_W = 64
POS_BASE = 10000.0

LANES = 128
SUBLANES = 8
ROW_BLOCK = 512
VMEM_LIMIT = 56 * 1024 * 1024


def _cparams(*sem):
    return pltpu.CompilerParams(dimension_semantics=sem, vmem_limit_bytes=VMEM_LIMIT)


def _full(shape):
    nd = len(shape)
    return pl.BlockSpec(shape, lambda *_: (0,) * nd)


def _rms_mod(x, g, scale, shift):
    y = x * lax.rsqrt(jnp.mean(x * x, axis=-1, keepdims=True) + EPS)
    return (y * g) * (1.0 + scale) + shift


def _silu(x):
    return x * jax.nn.sigmoid(x)


def _row_block(n):
    return min(ROW_BLOCK, n)


def _mod_kernel(c_ref, w_ref, b_ref, o_ref):
    s = _silu(c_ref[...])
    o_ref[0] = jnp.dot(s, w_ref[0], precision=HIGHEST, preferred_element_type=F32) + b_ref[0]


def _modulation(cond, w_mod, b_mod):
    depth, d, d6 = w_mod.shape
    rows = cond.shape[0]
    tn = 1536
    return pl.pallas_call(
        _mod_kernel,
        out_shape=jax.ShapeDtypeStruct((depth, rows, d6), F32),
        grid=(depth, d6 // tn),
        in_specs=[pl.BlockSpec((rows, d), lambda i, j: (0, 0)),
                  pl.BlockSpec((1, d, tn), lambda i, j: (i, 0, j)),
                  pl.BlockSpec((1, 1, tn), lambda i, j: (i, 0, j))],
        out_specs=pl.BlockSpec((1, rows, tn), lambda i, j: (i, 0, j)),
        compiler_params=_cparams("parallel", "parallel"),
        name="modulation",
    )(cond, w_mod, b_mod.reshape(depth, 1, d6))


def _store_token_major(ref, lead, val, row0=0):
    rows, d = val.shape
    nl = d // LANES
    for c in range(nl):
        ref[lead + (pl.ds(row0 * nl + c, rows, stride=nl), slice(None))] = val[:, c * LANES:(c + 1) * LANES]


def _load_token_major(ref, lead, rows, d):
    nl = d // LANES
    return jnp.concatenate([ref[lead + (pl.ds(c, rows, stride=nl), slice(None))] for c in range(nl)], axis=1)


def _is_token_major(shape):
    return shape[-1] == LANES


def _stream_dims(x, d):
    return (x.shape[0], x.shape[1] * LANES // d) if _is_token_major(x.shape) else x.shape[:2]


def _stream_spec(x, tm, d):
    if _is_token_major(x.shape):
        return pl.BlockSpec((1, tm * (d // LANES), LANES), lambda b, i: (b, i, 0))
    return pl.BlockSpec((1, tm, d), lambda b, i: (b, i, 0))


def _stream_rows(x_ref, tm, d):
    if _is_token_major(x_ref.shape):
        return _load_token_major(x_ref, (0,), tm, d)
    return x_ref[0]


def _post(xn, mod_ref, gffn_ref, wr_ref, xo_ref, m_ref, aff_ref):
    _store_token_major(xo_ref, (0,), xn)
    m = _rms_mod(xn, gffn_ref[...], mod_ref[0, 4:5, :], mod_ref[0, 3:4, :])
    _store_token_major(m_ref, (0,), m)
    logits = lax.dot_general(wr_ref[...], m, (((1,), (1,)), ((), ())),
                             precision=HIGHEST, preferred_element_type=F32)
    ex = jnp.exp(logits - jnp.max(logits, axis=0, keepdims=True))
    aff_ref[0] = ex / jnp.sum(ex, axis=0, keepdims=True)


def _post_specs(bsz, n, d, tm):
    nl = d // LANES
    out_shape = (jax.ShapeDtypeStruct((bsz, n * nl, LANES), F32),
                 jax.ShapeDtypeStruct((bsz, n * nl, LANES), F32),
                 jax.ShapeDtypeStruct((bsz, N_EXPERTS, n), F32))
    out_specs = (pl.BlockSpec((1, tm * nl, LANES), lambda b, i: (b, i, 0)),
                 pl.BlockSpec((1, tm * nl, LANES), lambda b, i: (b, i, 0)),
                 pl.BlockSpec((1, N_EXPERTS, tm), lambda b, i: (b, 0, i)))
    return out_shape, out_specs


CONV_HALO = 16
CONV_ROWS = 128


def _conv_mixer_kernel(*refs, has_pe, tm):
    if has_pe:
        (x_ref, xp_ref, xn_ref, pe_ref, pep_ref, pen_ref, mod_ref, g_ref, win_ref, bin_ref, dw_ref, dwb_ref,
         lng_ref, lnb_ref, w_ref, b_ref, gffn_ref, wr_ref, xo_ref, m_ref, aff_ref, ext_ref, cv_ref) = refs
    else:
        (x_ref, xp_ref, xn_ref, mod_ref, g_ref, win_ref, bin_ref, dw_ref, dwb_ref,
         lng_ref, lnb_ref, w_ref, b_ref, gffn_ref, wr_ref, xo_ref, m_ref, aff_ref, ext_ref, cv_ref) = refs
    d = win_ref.shape[0]
    i = pl.program_id(1)
    last = pl.num_programs(1) - 1
    x = _stream_rows(x_ref, tm, d)
    x_prev = _stream_rows(xp_ref, CONV_HALO, d)
    x_next = _stream_rows(xn_ref, CONV_HALO, d)
    if has_pe:
        x = x + pe_ref[...]
        x_prev = x_prev + pep_ref[...]
        x_next = x_next + pen_ref[...]

    xe = jnp.concatenate([x_prev, x, x_next], axis=0)
    h = _rms_mod(xe, g_ref[...], mod_ref[0, 1:2, :], mod_ref[0, 0:1, :]).astype(BF16)
    z = jnp.dot(h, win_ref[...], preferred_element_type=F32) + bin_ref[...]
    u = z[:, :d] * jax.nn.sigmoid(z[:, d:])
    rid = lax.broadcasted_iota(I32, (tm + 2 * CONV_HALO, 1), 0)
    lo = jnp.where(i > 0, 0, CONV_HALO)
    hi = jnp.where(i < last, tm + 2 * CONV_HALO, tm + CONV_HALO)
    ext_ref[...] = jnp.where((rid >= lo) & (rid < hi), u, 0.0)

    first_tap = CONV_HALO - CONV_WIDTH // 2

    rows = min(CONV_ROWS, tm)
    for lt in range(d // LANES):
        ls = slice(lt * LANES, (lt + 1) * LANES)
        for c in range(tm // rows):
            base = c * rows
            acc = jnp.zeros((rows, LANES), F32)
            for s in range(SUBLANES):
                part = None
                for o in range(first_tap, first_tap + CONV_WIDTH):
                    if o % SUBLANES != s:
                        continue
                    lo_row = base + o - s
                    term = dw_ref[o - first_tap:o - first_tap + 1, ls] * ext_ref[lo_row:lo_row + rows + SUBLANES, ls]
                    part = term if part is None else part + term
                acc = acc + part[s:s + rows]
            cv_ref[base:base + rows, ls] = acc
    cv = cv_ref[...] + dwb_ref[...]
    mu = jnp.mean(cv, axis=-1, keepdims=True)
    xc = cv - mu
    ln = xc * lax.rsqrt(jnp.mean(xc * xc, axis=-1, keepdims=True) + EPS) * lng_ref[...] + lnb_ref[...]
    y = jnp.dot(_silu(ln).astype(BF16), w_ref[...], preferred_element_type=F32) + b_ref[...]
    _post(x + mod_ref[0, 2:3, :] * y, mod_ref, gffn_ref, wr_ref, xo_ref, m_ref, aff_ref)


def _conv_mixer(x, pe, mod, g_mix, w_in, b_in, dw, dw_b, ln_g, ln_b, w_out, b_out, g_ffn, w_router):
    d = w_in.shape[0]
    bsz, n = _stream_dims(x, d)
    tm = _row_block(n)
    hb = tm // CONV_HALO
    nh = n // CONV_HALO
    prev_blk = lambda i: jnp.maximum(i * hb - 1, 0)
    next_blk = lambda i: jnp.minimum((i + 1) * hb, nh - 1)
    if _is_token_major(x.shape):
        halo = lambda f: pl.BlockSpec((1, CONV_HALO * (d // LANES), LANES), lambda b, i: (b, f(i), 0))
    else:
        halo = lambda f: pl.BlockSpec((1, CONV_HALO, d), lambda b, i: (b, f(i), 0))
    in_specs = [_stream_spec(x, tm, d), halo(prev_blk), halo(next_blk)]
    args = [x, x, x]
    if pe is not None:
        in_specs += [pl.BlockSpec((tm, d), lambda b, i: (i, 0)),
                     pl.BlockSpec((CONV_HALO, d), lambda b, i: (prev_blk(i), 0)),
                     pl.BlockSpec((CONV_HALO, d), lambda b, i: (next_blk(i), 0))]
        args += [pe, pe, pe]
    vec = _full((1, d))
    in_specs += [pl.BlockSpec((1, 6, d), lambda b, i: (b, 0, 0)), vec, _full((d, 2 * d)), _full((1, 2 * d)),
                 _full((CONV_WIDTH, d)), vec, vec, vec, _full((d, d)), vec, vec, _full((N_EXPERTS, d))]
    args += [mod, g_mix.reshape(1, d), w_in.astype(BF16), b_in.reshape(1, 2 * d),
             dw, dw_b.reshape(1, d), ln_g.reshape(1, d), ln_b.reshape(1, d),
             w_out.astype(BF16), b_out.reshape(1, d), g_ffn.reshape(1, d), w_router.T]
    out_shape, out_specs = _post_specs(bsz, n, d, tm)
    return pl.pallas_call(
        functools.partial(_conv_mixer_kernel, has_pe=pe is not None, tm=tm),
        out_shape=out_shape, grid=(bsz, n // tm), in_specs=in_specs, out_specs=out_specs,
        scratch_shapes=[pltpu.VMEM((tm + 2 * CONV_HALO, d), F32), pltpu.VMEM((tm, d), F32)],
        compiler_params=_cparams("parallel", "parallel"), name="conv_mixer",
    )(*args)


GROUPS_PER_TILE = LANES // S5_GROUP
STEPS_PER_TILE = LANES // S5_GROUP


def _s5_in_kernel(x_ref, mod_ref, g_ref, ug_ref, slab_ref, *, tm):
    u = _rms_mod(_stream_rows(x_ref, tm, g_ref.shape[1]), g_ref[...], mod_ref[0, 1:2, :], mod_ref[0, 0:1, :])
    nl = u.shape[1] // LANES
    nc = tm // S5_CHUNK
    for lt in range(nl):
        slab_ref[lt] = u[:, lt * LANES:(lt + 1) * LANES]
    lane_step = lax.broadcasted_iota(I32, (nc, LANES), 1) // S5_GROUP
    for lt in range(nl):
        steps = [slab_ref[lt, pl.ds(t, nc, stride=S5_CHUNK), :] for t in range(S5_CHUNK)]
        for g8 in range(GROUPS_PER_TILE):
            for half in range(S5_CHUNK // STEPS_PER_TILE):
                acc = jnp.zeros((nc, LANES), F32)
                for tq in range(STEPS_PER_TILE):
                    shift = ((tq - g8) * S5_GROUP) % LANES
                    src = steps[half * STEPS_PER_TILE + tq]
                    acc = jnp.where(lane_step == tq, pltpu.roll(src, shift, 1) if shift else src, acc)
                ug_ref[0, lt * GROUPS_PER_TILE + g8, :, half * LANES:(half + 1) * LANES] = acc.astype(BF16)


def _s5_in(x, mod, g_mix):
    d = g_mix.shape[0]
    bsz, n = _stream_dims(x, d)
    tm = _row_block(n)
    groups = d // S5_GROUP
    lw = S5_CHUNK * S5_GROUP
    return pl.pallas_call(
        functools.partial(_s5_in_kernel, tm=tm),
        out_shape=jax.ShapeDtypeStruct((bsz, groups, n // S5_CHUNK, lw), BF16),
        grid=(bsz, n // tm),
        in_specs=[_stream_spec(x, tm, d),
                  pl.BlockSpec((1, 6, d), lambda b, i: (b, 0, 0)), _full((1, d))],
        out_specs=pl.BlockSpec((1, groups, tm // S5_CHUNK, lw), lambda b, i: (b, 0, i, 0)),
        scratch_shapes=[pltpu.VMEM((d // LANES, tm, LANES), F32)],
        compiler_params=_cparams("parallel", "parallel"), name="s5_in",
    )(x, mod, g_mix.reshape(1, d))


def _s5_tables(lam_re, lam_im, log_dt, b_re, b_im, c_re, c_im):
    L = S5_CHUNK
    dt = jnp.exp(log_dt)[:, :, None]
    mag = jnp.exp(lam_re * dt)
    ang = lam_im * dt
    lb_re = mag * jnp.cos(ang)
    lb_im = mag * jnp.sin(ang)
    nr = lb_re - 1.0
    ni = lb_im
    den = lam_re * lam_re + lam_im * lam_im
    coef_re = ((nr * lam_re + ni * lam_im) / den)[..., None]
    coef_im = ((ni * lam_re - nr * lam_im) / den)[..., None]
    bb_re = coef_re * b_re - coef_im * b_im
    bb_im = coef_re * b_im + coef_im * b_re

    def powers(steps):
        st = steps.astype(F32)
        m = jnp.exp((lam_re * dt)[..., None] * st)
        a = (lam_im * dt)[..., None] * st
        return m * jnp.cos(a), m * jnp.sin(a)

    pw_re, pw_im = powers(jnp.arange(L + 1))
    cp_re = c_re[:, :, None] * pw_re.transpose(0, 1, 3, 2)[:, :, :, None, :] \
        - c_im[:, :, None] * pw_im.transpose(0, 1, 3, 2)[:, :, :, None, :]
    cp_im = c_re[:, :, None] * pw_im.transpose(0, 1, 3, 2)[:, :, :, None, :] \
        + c_im[:, :, None] * pw_re.transpose(0, 1, 3, 2)[:, :, :, None, :]
    kern = (jnp.einsum("dgtkp,dgpj->dgtkj", cp_re[:, :, :L], bb_re, precision=HIGHEST)
            - jnp.einsum("dgtkp,dgpj->dgtkj", cp_im[:, :, :L], bb_im, precision=HIGHEST))
    lbb_re = pw_re.transpose(0, 1, 3, 2)[..., None] * bb_re[:, :, None] \
        - pw_im.transpose(0, 1, 3, 2)[..., None] * bb_im[:, :, None]
    lbb_im = pw_re.transpose(0, 1, 3, 2)[..., None] * bb_im[:, :, None] \
        + pw_im.transpose(0, 1, 3, 2)[..., None] * bb_re[:, :, None]

    s_idx = jnp.arange(L)[:, None]
    t_idx = jnp.arange(L)[None, :]
    G = lam_re.shape[1]
    K = S5_GROUP
    P = S5_STATE
    tes, fos = [], []
    for direction in range(2):
        if direction == 0:
            lag = t_idx - s_idx
            e_pow = (L - 1) - jnp.arange(L)
            f_pow = jnp.arange(L) + 1
        else:
            lag = s_idx - t_idx
            e_pow = jnp.arange(L)
            f_pow = L - jnp.arange(L)
        pick = (lag[:, :, None] == jnp.arange(L)).astype(F32)
        kd = jnp.einsum("stl,glkj->gstkj", pick, kern[direction], precision=HIGHEST)
        tmat = kd.transpose(0, 1, 4, 2, 3).reshape(G, L * K, L * K)
        e_re = lbb_re[direction][:, e_pow]
        e_im = lbb_im[direction][:, e_pow]
        emat = jnp.concatenate([e_re.transpose(0, 1, 3, 2).reshape(G, L * K, P),
                                e_im.transpose(0, 1, 3, 2).reshape(G, L * K, P)], axis=-1)
        tes.append(jnp.concatenate([tmat, emat], axis=-1))
        f_re = cp_re[direction][:, f_pow]
        f_im = cp_im[direction][:, f_pow]
        fos.append(jnp.concatenate([f_re.transpose(0, 3, 1, 2).reshape(G, P, L * K),
                                    -f_im.transpose(0, 3, 1, 2).reshape(G, P, L * K)], axis=1))
    te = jnp.stack(tes).astype(BF16)
    fo = jnp.stack(fos).astype(BF16)
    a_re, a_im = powers(jnp.full((1,), L))
    a_re, a_im = a_re[..., 0], a_im[..., 0]
    tab = jnp.stack([jnp.concatenate([a_re, a_re], axis=-1),
                     jnp.concatenate([-a_im, a_im], axis=-1)], axis=1)
    return te, fo, tab


S5_GROUPS_PER_STEP = 8


S5_SCAN_UNROLL = 4


def _s5_core_kernel(u_ref, tef_ref, teb_ref, fof_ref, fob_ref, tab_ref, yl_ref, yc_ref,
                    yf_ref, yb_ref, hf_ref, hb_ref, sf_ref, sb_ref, *, nctx, nlat):
    gb = S5_GROUPS_PER_STEP
    rows = nctx + nlat
    lw = S5_CHUNK * S5_GROUP
    for g in range(gb):
        u = u_ref[0, g]
        zf = jnp.dot(u[0:rows], tef_ref[g], preferred_element_type=F32)
        yf_ref[g] = zf[:, :lw]
        hf_ref[pl.ds(g, rows, stride=gb), :] = zf[:, lw:]
        sf_ref[pl.ds(g, rows, stride=gb), :] = pltpu.roll(zf[:, lw:], S5_STATE, 1)
        zb = jnp.dot(u[nctx:nctx + rows], teb_ref[g], preferred_element_type=F32)
        yb_ref[g] = zb[:, :lw]
        hb_ref[pl.ds(g, rows, stride=gb), :] = zb[:, lw:]
        sb_ref[pl.ds(g, rows, stride=gb), :] = pltpu.roll(zb[:, lw:], S5_STATE, 1)

    cat_f, swp_f = tab_ref[0, 0], tab_ref[0, 1]
    cat_b, swp_b = tab_ref[1, 0], tab_ref[1, 1]

    def scan_rows(i, carry):
        hf, sf, hb, sb = carry
        rf = pl.multiple_of(i * gb, gb)
        rb = pl.multiple_of((rows - 1 - i) * gb, gb)
        ef, esf = hf_ref[pl.ds(rf, gb), :], sf_ref[pl.ds(rf, gb), :]
        eb, esb = hb_ref[pl.ds(rb, gb), :], sb_ref[pl.ds(rb, gb), :]
        hf_ref[pl.ds(rf, gb), :] = hf
        hb_ref[pl.ds(rb, gb), :] = hb
        return (cat_f * hf + swp_f * sf + ef, cat_f * sf - swp_f * hf + esf,
                cat_b * hb + swp_b * sb + eb, cat_b * sb - swp_b * hb + esb)

    zero = jnp.zeros((gb, 2 * S5_STATE), F32)
    lax.fori_loop(0, rows, scan_rows, (zero, zero, zero, zero), unroll=S5_SCAN_UNROLL)

    for g in range(gb):
        hin_f = hf_ref[pl.ds(g, rows, stride=gb), :].astype(BF16)
        hin_b = hb_ref[pl.ds(g, rows, stride=gb), :].astype(BF16)
        yf = yf_ref[g] + jnp.dot(hin_f, fof_ref[g], preferred_element_type=F32)
        yb = yb_ref[g] + jnp.dot(hin_b, fob_ref[g], preferred_element_type=F32)
        yl_ref[0, g] = yf[nctx:nctx + nlat] + yb[0:nlat]
        yc_ref[0, g] = yf[0:nctx] + yb[nlat:nlat + nctx]


def _s5_core(u_all, te, fo, tab, nctx, nlat):
    bsz, G, rtot, lw = u_all.shape
    gb = S5_GROUPS_PER_STEP
    rows = nctx + nlat
    w = 2 * S5_STATE
    gspec3 = lambda shape: pl.BlockSpec((gb,) + shape, lambda b, j: (j, 0, 0))
    return pl.pallas_call(
        functools.partial(_s5_core_kernel, nctx=nctx, nlat=nlat),
        out_shape=(jax.ShapeDtypeStruct((bsz, G, nlat, lw), F32),
                   jax.ShapeDtypeStruct((bsz, G, nctx, lw), F32)),
        grid=(bsz, G // gb),
        in_specs=[pl.BlockSpec((1, gb, rtot, lw), lambda b, j: (b, j, 0, 0)),
                  gspec3((lw, lw + w)), gspec3((lw, lw + w)), gspec3((w, lw)), gspec3((w, lw)),
                  pl.BlockSpec((2, 2, gb, w), lambda b, j: (0, 0, j, 0))],
        out_specs=(pl.BlockSpec((1, gb, nlat, lw), lambda b, j: (b, j, 0, 0)),
                   pl.BlockSpec((1, gb, nctx, lw), lambda b, j: (b, j, 0, 0))),
        scratch_shapes=[pltpu.VMEM((gb, rows, lw), F32), pltpu.VMEM((gb, rows, lw), F32)]
        + [pltpu.VMEM((rows * gb, w), F32)] * 4,
        compiler_params=_cparams("parallel", "parallel"), name="s5_core",
    )(u_all, te[0], te[1], fo[0], fo[1], tab)


def _s5_out_kernel(yg_ref, x_ref, mod_ref, gmix_ref, dsk_ref, w_ref, b_ref, gffn_ref, wr_ref,
                   xo_ref, m_ref, aff_ref, slab_ref, *, tm):
    nl = gmix_ref.shape[1] // LANES
    nc = tm // S5_CHUNK
    lane_group = lax.broadcasted_iota(I32, (nc, LANES), 1) // S5_GROUP
    for lt in range(nl):
        for t in range(S5_CHUNK):
            half, tq = divmod(t, STEPS_PER_TILE)
            acc = jnp.zeros((nc, LANES), F32)
            for g8 in range(GROUPS_PER_TILE):
                src = yg_ref[0, lt * GROUPS_PER_TILE + g8, :, half * LANES:(half + 1) * LANES]
                shift = ((g8 - tq) * S5_GROUP) % LANES
                acc = jnp.where(lane_group == g8, pltpu.roll(src, shift, 1) if shift else src, acc)
            slab_ref[lt, pl.ds(t, nc, stride=S5_CHUNK), :] = acc
    ssm = jnp.concatenate([slab_ref[lt] for lt in range(nl)], axis=1)
    x = _stream_rows(x_ref, tm, gmix_ref.shape[1])
    u = _rms_mod(x, gmix_ref[...], mod_ref[0, 1:2, :], mod_ref[0, 0:1, :])
    y = ssm + dsk_ref[...] * u
    z = jnp.dot(jax.nn.gelu(y).astype(BF16), w_ref[...], preferred_element_type=F32) + b_ref[...]
    d = z.shape[1] // 2
    out = z[:, :d] * jax.nn.sigmoid(z[:, d:])
    _post(x + mod_ref[0, 2:3, :] * out, mod_ref, gffn_ref, wr_ref, xo_ref, m_ref, aff_ref)


def _s5_out(yg, x, mod, g_mix, d_skip, w_glu, b_glu, g_ffn, w_router):
    d = g_mix.shape[0]
    bsz, n = _stream_dims(x, d)
    tm = _row_block(n)
    groups = d // S5_GROUP
    lw = S5_CHUNK * S5_GROUP
    row = pl.BlockSpec((1, tm, d), lambda b, i: (b, i, 0))
    vec = _full((1, d))
    out_shape, out_specs = _post_specs(bsz, n, d, tm)
    return pl.pallas_call(
        functools.partial(_s5_out_kernel, tm=tm), out_shape=out_shape, grid=(bsz, n // tm),
        in_specs=[pl.BlockSpec((1, groups, tm // S5_CHUNK, lw), lambda b, i: (b, 0, i, 0)),
                  _stream_spec(x, tm, d), pl.BlockSpec((1, 6, d), lambda b, i: (b, 0, 0)), vec, vec,
                  _full((d, 2 * d)), _full((1, 2 * d)), vec, _full((N_EXPERTS, d))],
        out_specs=out_specs,
        scratch_shapes=[pltpu.VMEM((d // LANES, tm, LANES), F32)],
        compiler_params=_cparams("parallel", "parallel"), name="s5_out",
    )(yg, x, mod, g_mix.reshape(1, d), d_skip.reshape(1, d), w_glu.astype(BF16), b_glu.reshape(1, 2 * d),
      g_ffn.reshape(1, d), w_router.T)


def _lru_in_kernel(x_ref, mod_ref, g_ref, wx_ref, bx_ref, xp_ref):
    x = _stream_rows(x_ref, xp_ref.shape[1], xp_ref.shape[2])
    h = _rms_mod(x, g_ref[...], mod_ref[0, 1:2, :], mod_ref[0, 0:1, :]).astype(BF16)
    xp_ref[0] = jnp.dot(h, wx_ref[...], preferred_element_type=F32) + bx_ref[...]


def _lru_in(x, mod, g_mix, w_x, b_x):
    d = g_mix.shape[0]
    bsz, n = _stream_dims(x, d)
    tm = _row_block(n)
    row = pl.BlockSpec((1, tm, d), lambda b, i: (b, i, 0))
    vec = _full((1, d))
    return pl.pallas_call(
        _lru_in_kernel,
        out_shape=jax.ShapeDtypeStruct((bsz, n, d), F32),
        grid=(bsz, n // tm),
        in_specs=[_stream_spec(x, tm, d), pl.BlockSpec((1, 6, d), lambda b, i: (b, 0, 0)), vec,
                  _full((d, d)), vec],
        out_specs=row, compiler_params=_cparams("parallel", "parallel"), name="lru_in",
    )(x, mod, g_mix.reshape(1, d), w_x.astype(BF16), b_x.reshape(1, d))


LRU_HALO = 8
LRU_SCAN_UNROLL = 8


def _lru_scan_kernel(xf_ref, xfp_ref, xfn_ref, xb_ref, xbp_ref, xbn_ref, h0_ref, cw_ref, cb_ref,
                     wai_ref, bai_ref, lam_ref, rf_ref, rb_ref, hfin_ref,
                     ext_ref, xl_ref, a_ref, b_ref, r_ref, h_ref, *, tm):
    i = pl.program_id(1)
    last = pl.num_programs(1) - 1
    hw = LANES
    d = LRU_HEADS * hw

    @pl.when(i == 0)
    def _():
        h_ref[...] = h0_ref[0]

    lam = lam_ref[...]
    nl = -lam
    softplus = jnp.maximum(nl, 0.0) + jnp.log1p(jnp.exp(-jnp.abs(nl)))
    c8 = -LRU_C * softplus

    def sigmoid(v):
        return 0.5 * jnp.tanh(0.5 * v) + 0.5

    def gates(dr, main_ref, prev_ref, next_ref, blk):
        _store_token_major(ext_ref, (), jnp.where(blk > 0, prev_ref[0], 0.0))
        _store_token_major(ext_ref, (), main_ref[0], row0=LRU_HALO)
        _store_token_major(ext_ref, (), jnp.where(blk < last, next_ref[0], 0.0), row0=LRU_HALO + tm)
        xl3 = jnp.broadcast_to(cb_ref[...], (tm, LRU_HEADS, hw))
        for k in range(LRU_CONV):
            off = (LRU_HALO - LRU_CONV // 2 + k) * LRU_HEADS
            xl3 = xl3 + cw_ref[k:k + 1] * ext_ref[off:off + tm * LRU_HEADS, :].reshape(tm, LRU_HEADS, hw)
        xl_ref[...] = xl3.reshape(tm * LRU_HEADS, hw)
        for hd in range(LRU_HEADS):
            sl = slice(hd * hw, (hd + 1) * hw)
            xl = xl_ref[pl.ds(hd, tm, stride=LRU_HEADS), :]
            z = jnp.dot(xl.astype(BF16), wai_ref[dr, hd], preferred_element_type=F32)
            r = sigmoid(z[:, :hw] + bai_ref[dr, 0:1, sl])
            ig = sigmoid(z[:, hw:] + bai_ref[dr, 1:2, sl])
            log_a = c8[dr:dr + 1, sl] * r
            th = jnp.tanh(log_a)
            a_ref[dr, pl.ds(hd, tm, stride=LRU_HEADS), :] = jnp.exp(log_a)
            b_ref[dr, pl.ds(hd, tm, stride=LRU_HEADS), :] = jnp.sqrt(-2.0 * th / (1.0 - th)) * (ig * xl)

    gates(0, xf_ref, xfp_ref, xfn_ref, i)
    gates(1, xb_ref, xbp_ref, xbn_ref, last - i)

    def step(t, carry):
        hf, hb = carry
        rf = pl.multiple_of(t * LRU_HEADS, LRU_HEADS)
        rb = pl.multiple_of((tm - 1 - t) * LRU_HEADS, LRU_HEADS)
        hf = a_ref[0, pl.ds(rf, LRU_HEADS), :] * hf + b_ref[0, pl.ds(rf, LRU_HEADS), :]
        hb = a_ref[1, pl.ds(rb, LRU_HEADS), :] * hb + b_ref[1, pl.ds(rb, LRU_HEADS), :]
        r_ref[0, pl.ds(rf, LRU_HEADS), :] = hf
        r_ref[1, pl.ds(rb, LRU_HEADS), :] = hb
        return hf, hb

    hf, hb = lax.fori_loop(0, tm, step, (h_ref[0], h_ref[1]), unroll=LRU_SCAN_UNROLL)
    h_ref[0] = hf
    h_ref[1] = hb
    hfin_ref[0] = h_ref[...]
    rf_ref[0] = _load_token_major(r_ref, (0,), tm, d)
    rb_ref[0] = _load_token_major(r_ref, (1,), tm, d)


def _lru_scan(xp, h0, conv_w, conv_b, w_a, b_a, w_i, b_i, lam):
    bsz, n, d = xp.shape
    tm = _row_block(n)
    nb = n // tm
    hb = tm // LRU_HALO
    nh = n // LRU_HALO
    fwd = lambda b, i: (b, i, 0)
    bwd = lambda b, i: (b, nb - 1 - i, 0)
    halo = lambda f: pl.BlockSpec((1, LRU_HALO, d), f)
    wai = jnp.concatenate([w_a, w_i], axis=-1).astype(BF16)
    bai = jnp.stack([b_a, b_i], axis=1)
    assert d // LRU_HEADS == LANES, "token-major scan assumes one 128-lane tile per head"
    state = pl.BlockSpec((1, 2, LRU_HEADS, LANES), lambda b, i: (b, 0, 0, 0))
    return pl.pallas_call(
        functools.partial(_lru_scan_kernel, tm=tm),
        out_shape=(jax.ShapeDtypeStruct((bsz, n, d), F32), jax.ShapeDtypeStruct((bsz, n, d), F32),
                   jax.ShapeDtypeStruct((bsz, 2, LRU_HEADS, LANES), F32)),
        grid=(bsz, nb),
        in_specs=[pl.BlockSpec((1, tm, d), fwd),
                  halo(lambda b, i: (b, jnp.maximum(i * hb - 1, 0), 0)),
                  halo(lambda b, i: (b, jnp.minimum((i + 1) * hb, nh - 1), 0)),
                  pl.BlockSpec((1, tm, d), bwd),
                  halo(lambda b, i: (b, jnp.maximum((nb - 1 - i) * hb - 1, 0), 0)),
                  halo(lambda b, i: (b, jnp.minimum((nb - i) * hb, nh - 1), 0)),
                  state,
                  _full((LRU_CONV, LRU_HEADS, LANES)), _full((1, LRU_HEADS, LANES)),
                  _full(wai.shape), _full(bai.shape), _full((2, d))],
        out_specs=(pl.BlockSpec((1, tm, d), fwd), pl.BlockSpec((1, tm, d), bwd), state),
        scratch_shapes=[pltpu.VMEM(((tm + 2 * LRU_HALO) * LRU_HEADS, LANES), F32),
                        pltpu.VMEM((tm * LRU_HEADS, LANES), F32)]
        + [pltpu.VMEM((2, tm * LRU_HEADS, LANES), F32)] * 3
        + [pltpu.VMEM((2, LRU_HEADS, LANES), F32)],
        compiler_params=_cparams("parallel", "arbitrary"), name="lru_scan",
    )(xp, xp, xp, xp, xp, xp, h0, conv_w.reshape(LRU_CONV, LRU_HEADS, LANES), conv_b.reshape(1, LRU_HEADS, LANES),
      wai, bai, lam)


def _lru_out_kernel(rf_ref, rb_ref, x_ref, mod_ref, gmix_ref, wy_ref, by_ref, w_ref, b_ref, gffn_ref, wr_ref,
                    xo_ref, m_ref, aff_ref):
    x = _stream_rows(x_ref, rf_ref.shape[1], rf_ref.shape[2])
    h = _rms_mod(x, gmix_ref[...], mod_ref[0, 1:2, :], mod_ref[0, 0:1, :]).astype(BF16)
    gate = jax.nn.gelu(jnp.dot(h, wy_ref[...], preferred_element_type=F32) + by_ref[...])
    r = (rf_ref[0] + rb_ref[0]) * gate
    y = jnp.dot(r.astype(BF16), w_ref[...], preferred_element_type=F32) + b_ref[...]
    _post(x + mod_ref[0, 2:3, :] * y, mod_ref, gffn_ref, wr_ref, xo_ref, m_ref, aff_ref)


def _lru_out(rf, rb, x, mod, g_mix, w_y, b_y, w_out, b_out, g_ffn, w_router):
    bsz, n, d = rf.shape
    tm = _row_block(n)
    row = pl.BlockSpec((1, tm, d), lambda b, i: (b, i, 0))
    vec = _full((1, d))
    out_shape, out_specs = _post_specs(bsz, n, d, tm)
    return pl.pallas_call(
        _lru_out_kernel, out_shape=out_shape, grid=(bsz, n // tm),
        in_specs=[row, row, _stream_spec(x, tm, d), pl.BlockSpec((1, 6, d), lambda b, i: (b, 0, 0)),
                  vec, _full((d, d)), vec, _full((d, d)), vec, vec, _full((N_EXPERTS, d))],
        out_specs=out_specs, compiler_params=_cparams("parallel", "parallel"), name="lru_out",
    )(rf, rb, x, mod, g_mix.reshape(1, d), w_y.astype(BF16), b_y.reshape(1, d),
      w_out.astype(BF16), b_out.reshape(1, d), g_ffn.reshape(1, d), w_router.T)


INF_BITS = 0x7F800000
SLOT_UNROLL = 32


def _tile_cumsum(mask, tri):
    return jnp.dot(mask.astype(BF16), tri, preferred_element_type=F32)


def _select_kernel(aff_ref, lpos_ref, off_ref, sel_ref, *, n, cap):
    nt = n // LANES
    bits = pltpu.bitcast(aff_ref[0], I32)
    capf = float(cap)

    def bisect(_, lohi):
        lo, hi = lohi
        mid = lo + ((hi - lo + 1) >> 1)
        cnt = jnp.sum(jnp.where(bits >= mid, 1.0, 0.0), axis=1, keepdims=True)
        ok = cnt >= capf
        return jnp.where(ok, mid, lo), jnp.where(ok, hi, mid - 1)

    e = bits.shape[0]
    thr, _ = lax.fori_loop(0, 31, bisect, (jnp.zeros((e, 1), I32), jnp.full((e, 1), INF_BITS, I32)))
    gt = jnp.where(bits > thr, 1.0, 0.0)
    eq = jnp.where(bits == thr, 1.0, 0.0)
    need = capf - jnp.sum(gt, axis=1, keepdims=True)

    r_i = lax.broadcasted_iota(I32, (LANES, LANES), 0)
    c_i = lax.broadcasted_iota(I32, (LANES, LANES), 1)
    tri = jnp.where(r_i <= c_i, 1.0, 0.0).astype(BF16)
    sup = jnp.where(r_i < c_i, 1.0, 0.0).astype(BF16)
    t_i = lax.broadcasted_iota(I32, (n, LANES), 0)
    k_i = lax.broadcasted_iota(I32, (n, LANES), 1)
    tile_of = jnp.where((t_i >> 7) == k_i, 1.0, 0.0).astype(BF16)

    def tile_offsets(mask):
        counts = jnp.dot(mask.astype(BF16), tile_of, preferred_element_type=F32)
        return jnp.dot(counts.astype(BF16), sup, preferred_element_type=F32)

    eq_off = tile_offsets(eq)
    for k in range(nt):
        sl = slice(k * LANES, (k + 1) * LANES)
        eqk = eq[:, sl]
        rank = eq_off[:, k:k + 1] + _tile_cumsum(eqk, tri) - eqk
        sel_ref[:, sl] = gt[:, sl] + eqk * jnp.where(rank < need, 1.0, 0.0)
    sel = sel_ref[...]
    off_ref[0] = tile_offsets(sel).astype(I32)
    for k in range(nt):
        sl = slice(k * LANES, (k + 1) * LANES)
        sk = sel[:, sl]
        lpos_ref[0, :, sl] = jnp.where(sk > 0.0, _tile_cumsum(sk, tri) - 1.0, -1.0).astype(I32)


def _select(aff_t, cap):
    bsz, e, n = aff_t.shape
    return pl.pallas_call(
        functools.partial(_select_kernel, n=n, cap=cap),
        out_shape=(jax.ShapeDtypeStruct((bsz, e, n), I32), jax.ShapeDtypeStruct((bsz, e, LANES), I32)),
        grid=(bsz,),
        in_specs=[pl.BlockSpec((1, e, n), lambda b: (b, 0, 0))],
        out_specs=(pl.BlockSpec((1, e, n), lambda b: (b, 0, 0)),
                   pl.BlockSpec((1, e, LANES), lambda b: (b, 0, 0))),
        scratch_shapes=[pltpu.VMEM((e, n), F32)],
        compiler_params=_cparams("parallel"), name="moe_select",
    )(aff_t)


def _slots_kernel(off_ref, lpos_ref, aff_ref, out_ref, scr_ref, *, nt, cap):
    b = pl.program_id(0)
    e = pl.program_id(1)
    base = (b * pl.num_programs(1) + e) * LANES
    j_i = lax.broadcasted_iota(I32, (LANES, LANES), 0)
    row = lax.broadcasted_iota(I32, (LANES, LANES), 0)
    lane = lax.broadcasted_iota(I32, (LANES, LANES), 1)

    def tile(k, carry):
        lp = lpos_ref[0, 0, k]
        av = aff_ref[0, 0, k]
        onehot = jnp.where(lp == j_i, 1.0, 0.0).astype(BF16)
        a1 = av.astype(BF16).astype(F32)
        r1 = av - a1
        a2 = r1.astype(BF16).astype(F32)
        a3 = r1 - a2
        q = jnp.where(row == 0, lane.astype(F32),
                      jnp.where(row == 1, a1, jnp.where(row == 2, a2, jnp.where(row == 3, a3, 0.0))))
        res = lax.dot_general(onehot, q.astype(BF16), (((1,), (1,)), ((), ())),
                              preferred_element_type=F32)
        res = res + jnp.where(lane == 0, jnp.asarray(k * LANES, F32), 0.0)
        scr_ref[pl.ds(off_ref[base + k], LANES), :] = res
        return carry

    lax.fori_loop(0, nt, tile, 0, unroll=min(SLOT_UNROLL, nt))
    out_ref[0, 0] = scr_ref[0:cap, :]


def _slots(lpos, tile_off, aff_t, cap):
    bsz, e, n = lpos.shape
    nt = n // LANES
    lp5 = lpos.reshape(bsz, e, nt, 1, LANES)
    af5 = aff_t.reshape(bsz, e, nt, 1, LANES)
    blk = pl.BlockSpec((1, 1, nt, 1, LANES), lambda b, j, off: (b, j, 0, 0, 0))
    return pl.pallas_call(
        functools.partial(_slots_kernel, nt=nt, cap=cap),
        out_shape=jax.ShapeDtypeStruct((bsz, e, cap, LANES), F32),
        grid_spec=pltpu.PrefetchScalarGridSpec(
            num_scalar_prefetch=1, grid=(bsz, e), in_specs=[blk, blk],
            out_specs=pl.BlockSpec((1, 1, cap, LANES), lambda b, j, off: (b, j, 0, 0)),
            scratch_shapes=[pltpu.VMEM((cap + LANES, LANES), F32)]),
        compiler_params=_cparams("parallel", "parallel"), name="moe_slots",
    )(tile_off.reshape(-1), lp5, af5)


ROW_UNROLL = 32


def _gather_kernel(idx_ref, m_ref, xs_ref, scr_ref, *, cap, nl):
    b = pl.program_id(0)
    e = pl.program_id(1)
    base = (b * pl.num_programs(1) + e) * cap

    def rows(ju, carry):
        j0 = ju * ROW_UNROLL
        for r in range(ROW_UNROLL):
            t = idx_ref[base + j0 + r]
            scr_ref[pl.ds(pl.multiple_of((j0 + r) * nl, nl), nl), :] = \
                m_ref[0, pl.ds(pl.multiple_of(t * nl, nl), nl), :]
        return carry

    lax.fori_loop(0, cap // ROW_UNROLL, rows, 0)
    xs_ref[0, 0] = _load_token_major(scr_ref, (), cap, nl * LANES).astype(BF16)


def _gather(idx_flat, m_tok, e, cap):
    bsz, rows_, _ = m_tok.shape
    nl = SUBLANES
    d = nl * LANES
    return pl.pallas_call(
        functools.partial(_gather_kernel, cap=cap, nl=nl),
        out_shape=jax.ShapeDtypeStruct((bsz, e, cap, d), BF16),
        grid_spec=pltpu.PrefetchScalarGridSpec(
            num_scalar_prefetch=1, grid=(bsz, e),
            in_specs=[pl.BlockSpec((1, rows_, LANES), lambda b, j, idx: (b, 0, 0),
                                   pipeline_mode=pl.Buffered(1))],
            out_specs=pl.BlockSpec((1, 1, cap, d), lambda b, j, idx: (b, j, 0, 0)),
            scratch_shapes=[pltpu.VMEM((cap * nl, LANES), F32)]),
        compiler_params=_cparams("parallel", "arbitrary"), name="moe_gather",
    )(idx_flat, m_tok)


FFN_CHUNK = 512


def _ffn_rows(xs, gs, w1b_ref, w3b_ref, w2b_ref):
    de = w1b_ref.shape[1]
    fc = min(FFN_CHUNK, de)
    acc = None
    for c in range(de // fc):
        sl = slice(c * fc, (c + 1) * fc)
        h1 = jnp.dot(xs, w1b_ref[:, sl], preferred_element_type=F32)
        h3 = jnp.dot(xs, w3b_ref[:, sl], preferred_element_type=F32)
        h = (_silu(h1) * h3).astype(BF16)
        part = jnp.dot(h, w2b_ref[sl, :], preferred_element_type=F32)
        acc = part if acc is None else acc + part
    gate = gs[:, 1:2] + gs[:, 2:3] + gs[:, 3:4]
    return acc * gate


def _ffn_kernel(*refs, n_streams, batched):
    ins = refs[:2 * n_streams]
    w1_ref, w3_ref, w2_ref = refs[2 * n_streams:2 * n_streams + 3]
    outs = refs[2 * n_streams + 3:3 * n_streams + 3]
    w1b_ref, w3b_ref, w2b_ref = refs[3 * n_streams + 3:]

    @pl.when(pl.program_id(1) == 0)
    def _():
        w1b_ref[...] = w1_ref[0, 0].astype(BF16)
        w3b_ref[...] = w3_ref[0, 0].astype(BF16)
        w2b_ref[...] = w2_ref[0, 0].astype(BF16)

    for k in range(n_streams):
        xs_ref, gs_ref, y_ref = ins[2 * k], ins[2 * k + 1], outs[k]
        if batched[k]:
            @pl.when(pl.program_id(1) == 0)
            def _(xs_ref=xs_ref, gs_ref=gs_ref, y_ref=y_ref):
                nb, _, cap, d = xs_ref.shape
                y = _ffn_rows(xs_ref[:, 0].reshape(nb * cap, d), gs_ref[:, 0].reshape(nb * cap, LANES),
                              w1b_ref, w3b_ref, w2b_ref)
                for s in range(nb):
                    _store_token_major(y_ref, (s, 0), y[s * cap:(s + 1) * cap])
        else:
            _store_token_major(y_ref, (0, 0), _ffn_rows(xs_ref[0, 0], gs_ref[0, 0], w1b_ref, w3b_ref, w2b_ref))


FFN_BATCH_ROWS = 128


def _ffn(streams, experts, layer):
    w1, w3, w2 = experts
    bsz, e, _, d = streams[0][0].shape
    de = w1.shape[3]
    nl = d // LANES
    in_specs, args, out_shape, out_specs, batched = [], [], [], [], []
    for xs, slots in streams:
        cap = xs.shape[2]
        whole = cap < FFN_BATCH_ROWS
        batched.append(whole)
        nb = bsz if whole else 1
        sample = (lambda j, b: (0, j, 0, 0)) if whole else (lambda j, b: (b, j, 0, 0))
        in_specs += [pl.BlockSpec((nb, 1, cap, d), sample), pl.BlockSpec((nb, 1, cap, LANES), sample)]
        args += [xs, slots]
        out_shape.append(jax.ShapeDtypeStruct((bsz, e, cap * nl, LANES), F32))
        out_specs.append(pl.BlockSpec((nb, 1, cap * nl, LANES), sample))
    in_specs += [pl.BlockSpec((1, 1, d, de), lambda j, b: (layer, j, 0, 0)),
                 pl.BlockSpec((1, 1, d, de), lambda j, b: (layer, j, 0, 0)),
                 pl.BlockSpec((1, 1, de, d), lambda j, b: (layer, j, 0, 0))]
    return pl.pallas_call(
        functools.partial(_ffn_kernel, n_streams=len(streams), batched=tuple(batched)),
        out_shape=tuple(out_shape),
        grid=(e, bsz), in_specs=in_specs, out_specs=tuple(out_specs),
        scratch_shapes=[pltpu.VMEM((d, de), BF16), pltpu.VMEM((d, de), BF16), pltpu.VMEM((de, d), BF16)],
        compiler_params=_cparams("parallel", "arbitrary"), name="moe_ffn",
    )(*args, w1, w3, w2)


COMBINE_UNROLL = 4


def _combine_kernel(idx_ref, y_ref, g2_ref, xin_hbm, acc_ref, sem, *, cap, nl):
    b = pl.program_id(0)
    e = pl.program_id(1)
    base = (b * pl.num_programs(1) + e) * cap

    @pl.when(e == 0)
    def _():
        load = pltpu.make_async_copy(xin_hbm.at[b], acc_ref.at[0], sem)
        load.start()
        load.wait()

    g2 = g2_ref[0]

    def rows(ju, carry):
        j0 = ju * COMBINE_UNROLL
        ts = [pl.multiple_of(idx_ref[base + j0 + r] * nl, nl) for r in range(COMBINE_UNROLL)]
        vals = [acc_ref[0, pl.ds(ts[r], nl), :]
                + g2 * y_ref[0, 0, pl.ds(pl.multiple_of((j0 + r) * nl, nl), nl), :]
                for r in range(COMBINE_UNROLL)]
        for r in range(COMBINE_UNROLL):
            acc_ref[0, pl.ds(ts[r], nl), :] = vals[r]
        return carry

    lax.fori_loop(0, cap // COMBINE_UNROLL, rows, 0)


def _combine(idx_flat, y_tok, x_tok, gate2, cap):
    bsz, e, rows_, _ = y_tok.shape
    nl = rows_ // cap
    return pl.pallas_call(
        functools.partial(_combine_kernel, cap=cap, nl=nl),
        out_shape=jax.ShapeDtypeStruct(x_tok.shape, F32),
        grid_spec=pltpu.PrefetchScalarGridSpec(
            num_scalar_prefetch=1, grid=(bsz, e),
            in_specs=[pl.BlockSpec((1, 1, rows_, LANES), lambda b, j, idx: (b, j, 0, 0)),
                      pl.BlockSpec((1, nl, LANES), lambda b, j, idx: (b, 0, 0)),
                      pl.BlockSpec(memory_space=pl.ANY)],
            out_specs=pl.BlockSpec((1,) + x_tok.shape[1:], lambda b, j, idx: (b, 0, 0),
                                   pipeline_mode=pl.Buffered(1)),
            scratch_shapes=[pltpu.SemaphoreType.DMA(())]),
        compiler_params=_cparams("parallel", "arbitrary"), name="moe_combine",
    )(idx_flat, y_tok, gate2.reshape(bsz, nl, LANES), x_tok)


def _final_norm_kernel(x_ref, g_ref, o_ref):
    x = _stream_rows(x_ref, o_ref.shape[1], o_ref.shape[2])
    o_ref[0] = x * lax.rsqrt(jnp.mean(x * x, axis=-1, keepdims=True) + EPS) * g_ref[...]


def _final_norm(x, final_g):
    d = final_g.shape[0]
    bsz, n = _stream_dims(x, d)
    tm = _row_block(n)
    return pl.pallas_call(
        _final_norm_kernel, out_shape=jax.ShapeDtypeStruct((bsz, n, d), F32), grid=(bsz, n // tm),
        in_specs=[_stream_spec(x, tm, d), _full((1, d))],
        out_specs=pl.BlockSpec((1, tm, d), lambda b, i: (b, i, 0)),
        compiler_params=_cparams("parallel", "parallel"), name="final_norm",
    )(x, final_g.reshape(1, d))


def _moe(streams, experts, layer):
    routed = []
    for x_tok, m_tok, aff_t, mod in streams:
        e, n = aff_t.shape[1], aff_t.shape[2]
        cap = EC_CAPACITY * n // e
        lpos, tile_off = _select(aff_t, cap)
        slots = _slots(lpos, tile_off, aff_t, cap)
        idx_flat = slots[..., 0].astype(I32).reshape(-1)
        routed.append((idx_flat, _gather(idx_flat, m_tok, e, cap), slots, cap))
    ys = _ffn([(xs, slots) for _, xs, slots, _ in routed], experts, layer)
    return [_combine(idx_flat, y_tok, x_tok, mod[:, 5], cap)
            for (idx_flat, _, _, cap), y_tok, (x_tok, _, _, mod) in zip(routed, ys, streams)]


def _pos_embed(n, d):
    rows = n // GRID_W
    quarter = d // 4
    omega = 1.0 / (POS_BASE ** (jnp.arange(quarter, dtype=F32) / quarter))
    r = jnp.arange(rows, dtype=F32)[:, None] * omega
    cc = jnp.arange(GRID_W, dtype=F32)[:, None] * omega
    row_emb = jnp.concatenate([jnp.sin(r), jnp.cos(r)], axis=-1)
    col_emb = jnp.concatenate([jnp.sin(cc), jnp.cos(cc)], axis=-1)
    emb = jnp.concatenate([
        jnp.broadcast_to(row_emb[:, None, :], (rows, GRID_W, d // 2)),
        jnp.broadcast_to(col_emb[None, :, :], (rows, GRID_W, d // 2))], axis=-1)
    return emb.reshape(rows * GRID_W, d)


def kernel(x, c, ctx, c_ctx, w_mod, b_mod, g_mix, g_ffn, conv_w_in, conv_b_in, conv_dw, conv_dw_b, conv_ln_g, conv_ln_b, conv_w_out, conv_b_out, s5_lam_re, s5_lam_im, s5_log_dt, s5_b_re, s5_b_im, s5_c_re, s5_c_im, s5_d, s5_w_glu, s5_b_glu, lru_w_y, lru_b_y, lru_w_x, lru_b_x, lru_conv_w, lru_conv_b, lru_w_a, lru_b_a, lru_w_i, lru_b_i, lru_lam, lru_w_out, lru_b_out, moe_router, moe_w1, moe_w3, moe_w2, final_g):
    bsz, n, d = x.shape
    nctx = ctx.shape[1]
    depth = w_mod.shape[0]
    n_mixers = 3
    reader_layers = [i for i in range(depth) if i % n_mixers != 0]
    last_reader = max(reader_layers) if reader_layers else -1

    rows = -(-(bsz + 1) // SUBLANES) * SUBLANES
    cond = jnp.zeros((rows, d), F32).at[:bsz].set(c).at[bsz].set(c_ctx)
    mods = _modulation(cond, w_mod, b_mod).reshape(depth, rows, 6, d)
    pe = _pos_embed(n, d)
    assert d == SUBLANES * LANES, "token-major MoE rows assume one (8, 128) tile per token"
    experts = (moe_w1, moe_w3, moe_w2)

    x_lat, x_ctx = x, ctx
    for i in range(depth):
        kind, j = i % n_mixers, i // n_mixers
        ctx_in = i <= last_reader
        ctx_out = i < last_reader
        mod_lat = mods[i, :bsz]
        mod_ctx = jnp.broadcast_to(mods[i, bsz][None], (bsz, 6, d))
        lat_pe = pe if i == 0 else None
        streams = [(x_lat, mod_lat, lat_pe)]
        if ctx_out:
            streams.append((x_ctx, mod_ctx, None))

        if kind == 0:
            post = []
            for xs_, mod_, pe_ in streams:
                post.append(_conv_mixer(xs_, pe_, mod_, g_mix[i], conv_w_in[j], conv_b_in[j], conv_dw[j],
                                        conv_dw_b[j], conv_ln_g[j], conv_ln_b[j], conv_w_out[j], conv_b_out[j],
                                        g_ffn[i], moe_router[i]))
        elif kind == 1:
            te, fo, tab = _s5_tables(s5_lam_re[j], s5_lam_im[j], s5_log_dt[j], s5_b_re[j], s5_b_im[j],
                                     s5_c_re[j], s5_c_im[j])
            uc = _s5_in(x_ctx, mod_ctx, g_mix[i])
            u_all = jnp.concatenate([uc, _s5_in(x_lat, mod_lat, g_mix[i]), uc], axis=2)
            y_lat, y_ctx = _s5_core(u_all, te, fo, tab, nctx // S5_CHUNK, n // S5_CHUNK)
            post = [_s5_out(y_lat, x_lat, mod_lat, g_mix[i], s5_d[j], s5_w_glu[j], s5_b_glu[j],
                            g_ffn[i], moe_router[i])]
            if ctx_out:
                post.append(_s5_out(y_ctx, x_ctx, mod_ctx, g_mix[i], s5_d[j], s5_w_glu[j],
                                    s5_b_glu[j], g_ffn[i], moe_router[i]))
        else:
            scan_args = (lru_conv_w[j], lru_conv_b[j], lru_w_a[j], lru_b_a[j], lru_w_i[j], lru_b_i[j], lru_lam[j])
            out_args = (g_mix[i], lru_w_y[j], lru_b_y[j], lru_w_out[j], lru_b_out[j], g_ffn[i], moe_router[i])
            xp_ctx = _lru_in(x_ctx, mod_ctx, g_mix[i], lru_w_x[j], lru_b_x[j])
            rf_c, rb_c, h_ctx = _lru_scan(xp_ctx, jnp.zeros((bsz, 2, LRU_HEADS, d // LRU_HEADS), F32), *scan_args)
            xp_lat = _lru_in(x_lat, mod_lat, g_mix[i], lru_w_x[j], lru_b_x[j])
            rf_l, rb_l, _ = _lru_scan(xp_lat, h_ctx, *scan_args)
            post = [_lru_out(rf_l, rb_l, x_lat, mod_lat, *out_args)]
            if ctx_out:
                post.append(_lru_out(rf_c, rb_c, x_ctx, mod_ctx, *out_args))

        mods_ = [mod_lat, mod_ctx]
        outs = _moe([(xn, m, aff, mods_[k]) for k, (xn, m, aff) in enumerate(post)], experts, i)
        x_lat = outs[0]
        if ctx_out:
            x_ctx = outs[1]
    return _final_norm(x_lat, final_g)
```
